```python
import jax, jax.numpy as jnp
from jax import lax
import numpy as np

D_MODEL = 2048
BATCH = 8
SEQ = 2048
DEPTH = 1

CTX_LEN = 256
GRID_W = 64
NORM_EPS = 1e-6
ROPE_THETA = 10000.0

RET_HEADS = 4
RET_QK_DIM = 256
RET_V_DIM = 512
RET_CHUNK = 128

ATTN_HEADS = 16
ATTN_KV_HEADS = 4
ATTN_HEAD_DIM = 128
ATTN_GROUP = ATTN_HEADS // ATTN_KV_HEADS
Q_BLOCK = 128

N_EXPERTS = 256
TOP_K = 8
N_GROUPS = 8
TOPK_GROUPS = 4
EXPERT_DIM = 512
ROUTED_SCALE = 2.5
MOE_BLOCK = 128

RET_Q_W = RET_HEADS * RET_QK_DIM
RET_V_W = RET_HEADS * RET_V_DIM
ATTN_Q_W = ATTN_HEADS * ATTN_HEAD_DIM
ATTN_KV_W = ATTN_KV_HEADS * ATTN_HEAD_DIM
PROJ_SPLITS = (RET_Q_W, RET_Q_W, RET_V_W, RET_V_W, ATTN_Q_W, ATTN_KV_W, ATTN_KV_W, D_MODEL, D_MODEL)
PROJ_DIM = 13312

kernel_name = "hybrid_retention_gqa_moe_dit_layer"


def _rmsnorm(x, w=None):
    xf = x.astype(jnp.float32)
    y = xf * lax.rsqrt(jnp.mean(xf * xf, axis=-1, keepdims=True) + NORM_EPS)
    if w is not None:
        y = y * w.astype(jnp.float32)
    return y.astype(x.dtype)


def _split_proj(p):
    parts, start = [], 0
    for width in PROJ_SPLITS:
        parts.append(p[..., start:start + width])
        start += width
    return parts


def _heads(a, n_heads):
    b, t, _ = a.shape
    return a.reshape(b, t, n_heads, -1).transpose(0, 2, 1, 3)


def _rope_angles(pos, dim):
    inv = ROPE_THETA ** (-jnp.arange(0, dim, 2, dtype=jnp.float32) / dim)
    return pos.astype(jnp.float32)[:, None] * inv[None, :]


def _rotate(x, ang):
    m = x.shape[-1] // 2
    cos = jnp.cos(ang).astype(x.dtype)
    sin = jnp.sin(ang).astype(x.dtype)
    x1, x2 = x[..., :m], x[..., m:]
    return jnp.concatenate([x1 * cos - x2 * sin, x1 * sin + x2 * cos], axis=-1)


def _rope_2d(x, row, col):
    half = x.shape[-1] // 2
    return jnp.concatenate([_rotate(x[..., :half], _rope_angles(row, half)),
                            _rotate(x[..., half:], _rope_angles(col, half))], axis=-1)


def _retention_scan(q, k, v, log_gamma, state0):
    b, h, t, dk = q.shape
    n_chunks = t // RET_CHUNK

    def chunks(a):
        return jnp.moveaxis(a.reshape(b, h, n_chunks, RET_CHUNK, a.shape[-1]), 2, 0)

    pos = jnp.arange(RET_CHUNK, dtype=jnp.float32)
    diff = pos[:, None] - pos[None, :]
    lg = log_gamma[:, None]
    inner_decay = jnp.where(diff >= 0, jnp.exp(log_gamma[:, None, None] * jnp.maximum(diff, 0.0)), 0.0).astype(q.dtype)
    q_decay = jnp.exp(lg * (pos + 1.0)).astype(q.dtype)[:, :, None]
    k_decay = jnp.exp(lg * (RET_CHUNK - 1.0 - pos)).astype(q.dtype)[:, :, None]
    chunk_decay = jnp.exp(log_gamma * RET_CHUNK).astype(q.dtype)[:, None, None]

    def step(state, blk):
        qb, kb, vb = blk
        scores = jnp.einsum('bhid,bhjd->bhij', qb, kb) * inner_decay
        out = (jnp.einsum('bhij,bhje->bhie', scores, vb)
               + jnp.einsum('bhid,bhde->bhie', qb * q_decay, state))
        state = state * chunk_decay + jnp.einsum('bhjd,bhje->bhde', kb * k_decay, vb)
        return state, out

    state, out = lax.scan(step, state0, (chunks(q), chunks(k), chunks(v)))
    out = jnp.moveaxis(out, 0, 2).reshape(b, h, t, v.shape[-1])
    return out, state


def _bidir_retention(q_x, k_x, v_x, q_c, k_c, v_c, lg_f, lg_b):
    b, h, _, dk = q_c.shape
    zero = jnp.zeros((b, h, dk, v_c.shape[-1]), v_c.dtype)
    flip = lambda a: jnp.flip(a, axis=2)
    oc_f, s_f = _retention_scan(q_c, k_c, v_c, lg_f, zero)
    oc_b, s_b = _retention_scan(flip(q_c), flip(k_c), flip(v_c), lg_b, zero)
    ox_f, _ = _retention_scan(q_x, k_x, v_x, lg_f, s_f)
    ox_b, _ = _retention_scan(flip(q_x), flip(k_x), flip(v_x), lg_b, s_b)
    return ox_f + flip(ox_b), oc_f + flip(oc_b)


def _attend(q, k, v):
    s = jnp.einsum('bkgqd,bksd->bkgqs', q, k).astype(jnp.float32) * (ATTN_HEAD_DIM ** -0.5)
    p = jax.nn.softmax(s, axis=-1).astype(v.dtype)
    return jnp.einsum('bkgqs,bksd->bkgqd', p, v)


def _latent_attention(q_x, keys, vals):
    b, kvh, g, t, hd = q_x.shape
    n_blocks = t // Q_BLOCK
    q_blocks = jnp.moveaxis(q_x.reshape(b, kvh, g, n_blocks, Q_BLOCK, hd), 3, 0)
    out = lax.map(lambda qb: _attend(qb, keys, vals), q_blocks)
    return jnp.moveaxis(out, 0, 3).reshape(b, kvh, g, t, hd)


def _merge_branches(ret_o, ret_g, attn_o, gate_r, gate_a, w_ret_o, w_attn_o, w_out):
    b, h, t, dv = ret_o.shape
    ret = _rmsnorm(ret_o).transpose(0, 2, 1, 3).reshape(b, t, h * dv)
    ret = (jax.nn.silu(ret_g) * ret) @ w_ret_o
    att = attn_o.reshape(b, ATTN_HEADS, t, ATTN_HEAD_DIM).transpose(0, 2, 1, 3).reshape(b, t, ATTN_Q_W)
    att = att @ w_attn_o
    return (jax.nn.sigmoid(gate_r) * ret + jax.nn.sigmoid(gate_a) * att) @ w_out


def _hybrid_mixer(h_x, h_c, row, col, w_in, ret_decay_fwd, ret_decay_bwd, attn_q_norm, attn_k_norm,
                  w_ret_o, w_attn_o, w_out, with_ctx):
    b, t, _ = h_x.shape
    rq_x, rk_x, rv_x, rg_x, aq_x, ak_x, av_x, gr_x, ga_x = _split_proj(h_x @ w_in)
    rq_c, rk_c, rv_c, rg_c, aq_c, ak_c, av_c, gr_c, ga_c = _split_proj(h_c @ w_in)

    k_scale = RET_QK_DIM ** -0.5
    lg_f = jax.nn.log_sigmoid(ret_decay_fwd.astype(jnp.float32))
    lg_b = jax.nn.log_sigmoid(ret_decay_bwd.astype(jnp.float32))
    ro_x, ro_c = _bidir_retention(
        _rope_2d(_heads(rq_x, RET_HEADS), row, col),
        _rope_2d(_heads(rk_x, RET_HEADS), row, col) * k_scale,
        _heads(rv_x, RET_HEADS),
        _heads(rq_c, RET_HEADS), _heads(rk_c, RET_HEADS) * k_scale, _heads(rv_c, RET_HEADS),
        lg_f, lg_b)

    qa_x = _rope_2d(_rmsnorm(_heads(aq_x, ATTN_HEADS), attn_q_norm), row, col)
    ka_x = _rope_2d(_rmsnorm(_heads(ak_x, ATTN_KV_HEADS), attn_k_norm), row, col)
    ka_c = _rmsnorm(_heads(ak_c, ATTN_KV_HEADS), attn_k_norm)
    va_c = _heads(av_c, ATTN_KV_HEADS)
    keys = jnp.concatenate([ka_c, ka_x], axis=2)
    vals = jnp.concatenate([va_c, _heads(av_x, ATTN_KV_HEADS)], axis=2)
    ao_x = _latent_attention(qa_x.reshape(b, ATTN_KV_HEADS, ATTN_GROUP, t, ATTN_HEAD_DIM), keys, vals)
    y_x = _merge_branches(ro_x, rg_x, ao_x, gr_x, ga_x, w_ret_o, w_attn_o, w_out)
    if not with_ctx:
        return y_x, None

    qa_c = _rmsnorm(_heads(aq_c, ATTN_HEADS), attn_q_norm)
    ao_c = _attend(qa_c.reshape(b, ATTN_KV_HEADS, ATTN_GROUP, h_c.shape[1], ATTN_HEAD_DIM), ka_c, va_c)
    y_c = _merge_branches(ro_c, rg_c, ao_c, gr_c, ga_c, w_ret_o, w_attn_o, w_out)
    return y_x, y_c


def _swiglu(h, w1, w3, w2):
    return (jax.nn.silu(h @ w1) * (h @ w3)) @ w2


def _moe(h, layer, router_w, router_b, exp_w1, exp_w3, exp_w2, shared_w1, shared_w3, shared_w2):
    b, t, d = h.shape
    n_tok = b * t
    h2 = h.reshape(n_tok, d)
    scores = jax.nn.sigmoid((h2 @ router_w[layer]).astype(jnp.float32))
    choice = scores + router_b[layer].astype(jnp.float32)
    grp_score = lax.top_k(choice.reshape(n_tok, N_GROUPS, N_EXPERTS // N_GROUPS), 2)[0].sum(-1)
    _, grp_idx = lax.top_k(grp_score, TOPK_GROUPS)
    grp_mask = jnp.any(grp_idx[:, :, None] == jnp.arange(N_GROUPS)[None, None, :], axis=1)
    exp_mask = jnp.repeat(grp_mask, N_EXPERTS // N_GROUPS, axis=1)
    _, top_idx = lax.top_k(jnp.where(exp_mask, choice, -jnp.inf), TOP_K)
    top_w = jnp.take_along_axis(scores, top_idx, axis=1)
    top_w = top_w / jnp.sum(top_w, axis=-1, keepdims=True) * ROUTED_SCALE

    n_slots = n_tok * TOP_K
    n_blocks = (n_slots + N_EXPERTS * (MOE_BLOCK - 1) + MOE_BLOCK - 1) // MOE_BLOCK
    e_flat = top_idx.reshape(-1)
    order = jnp.argsort(e_flat)
    e_sorted = e_flat[order]
    tok_sorted = (jnp.arange(n_slots, dtype=jnp.int32) // TOP_K)[order]
    w_sorted = top_w.reshape(-1)[order].astype(h.dtype)
    counts = jnp.bincount(e_flat, length=N_EXPERTS)
    starts = jnp.cumsum(counts) - counts
    padded = (counts + MOE_BLOCK - 1) // MOE_BLOCK * MOE_BLOCK
    pad_ends = jnp.cumsum(padded)
    dest = (pad_ends - padded)[e_sorted] + jnp.arange(n_slots, dtype=jnp.int32) - starts[e_sorted]
    buf_tok = jnp.zeros((n_blocks * MOE_BLOCK,), jnp.int32).at[dest].set(tok_sorted)
    buf_w = jnp.zeros((n_blocks * MOE_BLOCK,), h.dtype).at[dest].set(w_sorted)
    block_start = jnp.arange(n_blocks, dtype=jnp.int32) * MOE_BLOCK
    block_expert = jnp.minimum(jnp.searchsorted(pad_ends, block_start, side='right'), N_EXPERTS - 1)

    def expert_block(acc, blk):
        tok, wt, e = blk
        xb = h2[tok]
        hid = jax.nn.silu(xb @ exp_w1[layer, e]) * (xb @ exp_w3[layer, e])
        return acc.at[tok].add((hid @ exp_w2[layer, e]) * wt[:, None]), None

    routed, _ = lax.scan(expert_block, jnp.zeros_like(h2),
                         (buf_tok.reshape(n_blocks, MOE_BLOCK), buf_w.reshape(n_blocks, MOE_BLOCK), block_expert))
    shared = _swiglu(h2, shared_w1[layer], shared_w3[layer], shared_w2[layer])
    return (routed + shared).reshape(b, t, d)


def setup_inputs(seed: int = 0) -> dict:
    key = jax.random.key(seed)
    ks = jax.random.split(key, 26)
    f32 = jnp.float32

    def nrm(k, shape, scale):
        return jax.random.normal(k, shape, f32) * scale

    def gain(k, shape):
        return 1.0 + 0.1 * jax.random.normal(k, shape, f32)

    decay_logit = jnp.asarray(np.log(2.0 ** (5 + np.arange(RET_HEADS)) - 1.0).astype(np.float32))
    return {
        "x": nrm(ks[0], (BATCH, SEQ, D_MODEL), 1.0),
        "c": nrm(ks[1], (BATCH, D_MODEL), 1.0),
        "ctx": nrm(ks[2], (BATCH, CTX_LEN, D_MODEL), 1.0),
        "c_ctx": nrm(ks[3], (D_MODEL,), 1.0),
        "ada_w": nrm(ks[4], (DEPTH, D_MODEL, 6 * D_MODEL), 0.5 * D_MODEL ** -0.5),
        "ada_b": nrm(ks[5], (DEPTH, 6 * D_MODEL), 0.02),
        "mix_pre_norm": gain(ks[6], (DEPTH, D_MODEL)),
        "mix_post_norm": gain(ks[7], (DEPTH, D_MODEL)),
        "ffn_pre_norm": gain(ks[8], (DEPTH, D_MODEL)),
        "ffn_post_norm": gain(ks[9], (DEPTH, D_MODEL)),
        "w_in": nrm(ks[10], (DEPTH, D_MODEL, PROJ_DIM), D_MODEL ** -0.5),
        "ret_decay_fwd": decay_logit + nrm(ks[11], (DEPTH, RET_HEADS), 0.1),
        "ret_decay_bwd": decay_logit + nrm(ks[12], (DEPTH, RET_HEADS), 0.1),
        "attn_q_norm": gain(ks[13], (DEPTH, ATTN_HEAD_DIM)),
        "attn_k_norm": gain(ks[14], (DEPTH, ATTN_HEAD_DIM)),
        "w_ret_o": nrm(ks[15], (DEPTH, RET_V_W, D_MODEL), RET_V_W ** -0.5),
        "w_attn_o": nrm(ks[16], (DEPTH, ATTN_Q_W, D_MODEL), ATTN_Q_W ** -0.5),
        "w_out": nrm(ks[17], (DEPTH, D_MODEL, D_MODEL), D_MODEL ** -0.5),
        "router_w": nrm(ks[18], (DEPTH, D_MODEL, N_EXPERTS), D_MODEL ** -0.5),
        "router_b": nrm(ks[19], (DEPTH, N_EXPERTS), 0.01),
        "exp_w1": nrm(ks[20], (DEPTH, N_EXPERTS, D_MODEL, EXPERT_DIM), D_MODEL ** -0.5),
        "exp_w3": nrm(ks[21], (DEPTH, N_EXPERTS, D_MODEL, EXPERT_DIM), D_MODEL ** -0.5),
        "exp_w2": nrm(ks[22], (DEPTH, N_EXPERTS, EXPERT_DIM, D_MODEL), EXPERT_DIM ** -0.5),
        "shared_w1": nrm(ks[23], (DEPTH, D_MODEL, EXPERT_DIM), D_MODEL ** -0.5),
        "shared_w3": nrm(ks[24], (DEPTH, D_MODEL, EXPERT_DIM), D_MODEL ** -0.5),
        "shared_w2": nrm(ks[25], (DEPTH, EXPERT_DIM, D_MODEL), EXPERT_DIM ** -0.5),
    }


def reference(x, c, ctx, c_ctx, ada_w, ada_b, mix_pre_norm, mix_post_norm, ffn_pre_norm, ffn_post_norm,
              w_in, ret_decay_fwd, ret_decay_bwd, attn_q_norm, attn_k_norm, w_ret_o, w_attn_o, w_out,
              router_w, router_b, exp_w1, exp_w3, exp_w2, shared_w1, shared_w3, shared_w2):
    t = x.shape[1]
    ROWS = t // GRID_W
    row = jnp.repeat(jnp.arange(ROWS, dtype=jnp.int32), GRID_W)
    col = jnp.tile(jnp.arange(GRID_W, dtype=jnp.int32), ROWS)
    silu_c = jax.nn.silu(c)
    silu_cc = jax.nn.silu(c_ctx)
    for i in range(DEPTH):
        last = i == DEPTH - 1
        sh1, sc1, g1, sh2, sc2, g2 = jnp.split((silu_c @ ada_w[i] + ada_b[i])[:, None, :], 6, axis=-1)
        ch1, cs1, cg1, ch2, cs2, cg2 = jnp.split(silu_cc @ ada_w[i] + ada_b[i], 6)

        h_x = _rmsnorm(x, mix_pre_norm[i]) * (1.0 + sc1) + sh1
        h_c = _rmsnorm(ctx, mix_pre_norm[i]) * (1.0 + cs1) + ch1
        y_x, y_c = _hybrid_mixer(h_x, h_c, row, col, w_in[i], ret_decay_fwd[i], ret_decay_bwd[i],
                                 attn_q_norm[i], attn_k_norm[i], w_ret_o[i], w_attn_o[i], w_out[i],
                                 with_ctx=not last)
        x = x + g1 * _rmsnorm(y_x, mix_post_norm[i])

        f_x = _rmsnorm(x, ffn_pre_norm[i]) * (1.0 + sc2) + sh2
        if last:
            x = x + g2 * _rmsnorm(_moe(f_x, i, router_w, router_b, exp_w1, exp_w3, exp_w2,
                                       shared_w1, shared_w3, shared_w2), ffn_post_norm[i])
        else:
            ctx = ctx + cg1 * _rmsnorm(y_c, mix_post_norm[i])
            f_c = _rmsnorm(ctx, ffn_pre_norm[i]) * (1.0 + cs2) + ch2
            f = _moe(jnp.concatenate([f_x, f_c], axis=1), i, router_w, router_b, exp_w1, exp_w3, exp_w2,
                     shared_w1, shared_w3, shared_w2)
            x = x + g2 * _rmsnorm(f[:, :t], ffn_post_norm[i])
            ctx = ctx + cg2 * _rmsnorm(f[:, t:], ffn_post_norm[i])
    return x
```

```python
import functools

import jax
import jax.numpy as jnp
import numpy as np
from jax import lax
from jax.experimental import pallas as pl
from jax.experimental.pallas import tpu as pltpu

F32 = jnp.float32
BF16 = jnp.bfloat16
U32 = jnp.uint32
I32 = jnp.int32

GRID_W = 64
NORM_EPS = 1e-6
ROPE_THETA = 10000.0
RET_HEADS = 4
RET_QK_DIM = 256
RET_V_DIM = 512
ATTN_HEADS = 16
ATTN_KV_HEADS = 4
ATTN_HEAD_DIM = 128
N_EXPERTS = 256
TOP_K = 8
N_GROUPS = 8
TOPK_GROUPS = 4
EXPERT_DIM = 512
ROUTED_SCALE = 2.5

V7X_VMEM_BYTES = 64 * 1024 * 1024
VMEM_LIMIT = V7X_VMEM_BYTES - 12 * 1024 * 1024
LANES = 128

RET_CHUNK = 256
MOE_BLOCK = 256


def _cparams(sem):
    return pltpu.CompilerParams(dimension_semantics=sem, vmem_limit_bytes=VMEM_LIMIT)


def _tile(dim, pref):
    if dim <= pref:
        return dim
    for t in range(pref, 0, -LANES):
        if dim % t == 0:
            return t
    raise ValueError((dim, pref))


def _blk(off, width):
    assert off % width == 0, (off, width)
    return off // width


def _ada_kernel(s_ref, w_ref, b_ref, o_ref):
    s = s_ref[...]
    s = (s * jax.nn.sigmoid(s)).astype(BF16)
    o_ref[...] = jnp.dot(s, w_ref[...].astype(BF16), preferred_element_type=F32) + b_ref[...]


def _adaln(cond, w, b):
    r, d = cond.shape
    n = w.shape[1]
    tn = _tile(n, 1024)
    return pl.pallas_call(
        _ada_kernel,
        grid=(n // tn,),
        in_specs=[pl.BlockSpec((r, d), lambda j: (0, 0)),
                  pl.BlockSpec((d, tn), lambda j: (0, j)),
                  pl.BlockSpec((1, tn), lambda j: (0, j))],
        out_specs=pl.BlockSpec((r, tn), lambda j: (0, j)),
        out_shape=jax.ShapeDtypeStruct((r, n), F32),
        compiler_params=_cparams(("arbitrary",)),
    )(cond, w, b.reshape(1, n))


def _rms(x):
    return x * lax.rsqrt(jnp.mean(x * x, axis=-1, keepdims=True) + NORM_EPS)


def _norm_mod_kernel(x_ref, w_ref, sc_ref, sh_ref, o_ref):
    y = _rms(x_ref[...]) * w_ref[...]
    o_ref[...] = (y * (1.0 + sc_ref[0]) + sh_ref[0]).astype(o_ref.dtype)


def _norm_mod(x2, w, sc, sh, rows_per_mod):
    m, d = x2.shape
    tm = _tile(rows_per_mod, 512)
    per = rows_per_mod // tm
    g = sc.shape[0]
    return pl.pallas_call(
        _norm_mod_kernel,
        grid=(m // tm,),
        in_specs=[pl.BlockSpec((tm, d), lambda i: (i, 0)),
                  pl.BlockSpec((1, d), lambda i: (0, 0)),
                  pl.BlockSpec((1, 1, d), lambda i: (i // per, 0, 0)),
                  pl.BlockSpec((1, 1, d), lambda i: (i // per, 0, 0))],
        out_specs=pl.BlockSpec((tm, d), lambda i: (i, 0)),
        out_shape=jax.ShapeDtypeStruct((m, d), BF16),
        compiler_params=_cparams(("arbitrary",)),
    )(x2, w.reshape(1, d), sc.reshape(g, 1, d), sh.reshape(g, 1, d))


def _mm_kernel(a_ref, b_ref, o_ref):
    o_ref[...] = jnp.dot(a_ref[...], b_ref[...], preferred_element_type=F32).astype(o_ref.dtype)


def _matmul(a, b, out_dtype, tm_pref=1024, tn_pref=1024):
    m, k = a.shape
    n = b.shape[1]
    tm, tn = _tile(m, tm_pref), _tile(n, tn_pref)
    return pl.pallas_call(
        _mm_kernel,
        grid=(m // tm, n // tn),
        in_specs=[pl.BlockSpec((tm, k), lambda i, j: (i, 0)),
                  pl.BlockSpec((k, tn), lambda i, j: (0, j))],
        out_specs=pl.BlockSpec((tm, tn), lambda i, j: (i, j)),
        out_shape=jax.ShapeDtypeStruct((m, n), out_dtype),
        compiler_params=_cparams(("arbitrary", "arbitrary")),
    )(a, b)


def _rope_tables(t, dim):
    half = dim // 2
    pos = jnp.arange(t, dtype=I32)
    row = (pos // GRID_W).astype(F32)
    col = (pos % GRID_W).astype(F32)
    inv = ROPE_THETA ** (-jnp.arange(0, half, 2, dtype=F32) / half)
    ar = row[:, None] * inv[None, :]
    ac = col[:, None] * inv[None, :]
    cr, sr, cc, sc = jnp.cos(ar), jnp.sin(ar), jnp.cos(ac), jnp.sin(ac)
    z = jnp.zeros_like(sr)
    c = jnp.concatenate([cr, cr, cc, cc], axis=1)
    s_lo = jnp.concatenate([-sr, z, -sc, z], axis=1)
    s_hi = jnp.concatenate([z, sr, z, sc], axis=1)
    return c, s_lo, s_hi


def _rope128(x, c, s_lo, s_hi):
    q4 = ATTN_HEAD_DIM // 4
    return (x * c + pltpu.roll(x, LANES - q4, axis=1) * s_lo + pltpu.roll(x, q4, axis=1) * s_hi)


def _rope256(x, c, s):
    parts = [pltpu.roll(x[:, i * LANES:(i + 1) * LANES], LANES // 2, axis=1)
             for i in range(RET_QK_DIM // LANES)]
    return x * c + jnp.concatenate(parts, axis=1) * s


def _ret_kernel(lg_ref, q_ref, k_ref, v_ref, g_ref, kc_ref, vc_ref, c_ref, s_ref, o_ref,
                qs, ks, acc, st):
    h = pl.program_id(1)
    lgf = lg_ref[0, h]
    lgb = lg_ref[1, h]
    t = q_ref.shape[0]
    ch = RET_CHUNK if t % RET_CHUNK == 0 else t
    nch = t // ch
    lc = kc_ref.shape[0]
    k_scale = RET_QK_DIM ** -0.5
    tn_dims = (((0,), (0,)), ((), ()))

    ii = lax.broadcasted_iota(I32, (ch, ch), 0)
    jj = lax.broadcasted_iota(I32, (ch, ch), 1)
    dif = (ii - jj).astype(F32)
    dmat = (jnp.where(dif >= 0, jnp.exp(lgf * jnp.maximum(dif, 0.0)), 0.0)
            + jnp.where(dif <= 0, jnp.exp(lgb * jnp.maximum(-dif, 0.0)), 0.0))
    pos = lax.broadcasted_iota(I32, (ch, 1), 0).astype(F32)
    qf_dec = jnp.exp(lgf * (pos + 1.0))
    kf_dec = jnp.exp(lgf * (ch - 1.0 - pos))
    qb_dec = jnp.exp(lgb * (ch - pos))
    kb_dec = jnp.exp(lgb * pos)
    chv = jnp.full((1, 1), float(ch), F32)
    cf_dec = jnp.exp(lgf * chv)
    cb_dec = jnp.exp(lgb * chv)

    cpos = lax.broadcasted_iota(I32, (lc, 1), 0).astype(F32)
    kc = kc_ref[...].astype(F32) * k_scale
    vc = vc_ref[...]
    st[0] = lax.dot_general((kc * jnp.exp(lgf * (lc - 1.0 - cpos))).astype(BF16), vc, tn_dims,
                            preferred_element_type=F32)
    st[1] = lax.dot_general((kc * jnp.exp(lgb * cpos)).astype(BF16), vc, tn_dims,
                            preferred_element_type=F32)

    def fwd(c, carry):
        r0 = pl.multiple_of(c * ch, ch)
        cs = c_ref[pl.ds(r0, ch), :]
        sn = s_ref[pl.ds(r0, ch), :]
        q = _rope256(q_ref[pl.ds(r0, ch), :].astype(F32), cs, sn)
        k = _rope256(k_ref[pl.ds(r0, ch), :].astype(F32), cs, sn) * k_scale
        v = v_ref[pl.ds(r0, ch), :]
        qb = q.astype(BF16)
        kb = k.astype(BF16)
        qs[pl.ds(r0, ch), :] = q
        ks[pl.ds(r0, ch), :] = k
        sc = lax.dot_general(qb, kb, (((1,), (1,)), ((), ())), preferred_element_type=F32) * dmat
        o = jnp.dot(sc.astype(BF16), v, preferred_element_type=F32)
        o += jnp.dot((q * qf_dec).astype(BF16), st[0].astype(BF16), preferred_element_type=F32)
        acc[pl.ds(r0, ch), :] = o
        st[0] = st[0] * cf_dec + lax.dot_general((k * kf_dec).astype(BF16), v, tn_dims,
                                                 preferred_element_type=F32)
        return carry

    lax.fori_loop(0, nch, fwd, 0)

    def bwd(i, carry):
        c = nch - 1 - i
        r0 = pl.multiple_of(c * ch, ch)
        q = qs[pl.ds(r0, ch), :]
        k = ks[pl.ds(r0, ch), :]
        v = v_ref[pl.ds(r0, ch), :]
        o = jnp.dot((q * qb_dec).astype(BF16), st[1].astype(BF16), preferred_element_type=F32)
        tot = acc[pl.ds(r0, ch), :] + o
        gate = g_ref[pl.ds(r0, ch), :].astype(F32)
        o_ref[pl.ds(r0, ch), :] = (gate * jax.nn.sigmoid(gate) * _rms(tot)).astype(o_ref.dtype)
        st[1] = st[1] * cb_dec + lax.dot_general((k * kb_dec).astype(BF16), v, tn_dims,
                                                 preferred_element_type=F32)
        return carry

    lax.fori_loop(0, nch, bwd, 0)


def _retention(p, pc, lg, tab_c, tab_s, b, t, lc, offs, coffs):
    h, dk, dv = RET_HEADS, RET_QK_DIM, RET_V_DIM
    qo, ko, vo, go = (_blk(offs["rq"], dk), _blk(offs["rk"], dk), _blk(offs["rv"], dv), _blk(offs["rg"], dv))
    kco, vco = _blk(coffs["rk"], dk), _blk(coffs["rv"], dv)
    grid_spec = pltpu.PrefetchScalarGridSpec(
        num_scalar_prefetch=1,
        grid=(b, h),
        in_specs=[pl.BlockSpec((t, dk), lambda bi, hi, lg: (bi, qo + hi)),
                  pl.BlockSpec((t, dk), lambda bi, hi, lg: (bi, ko + hi)),
                  pl.BlockSpec((t, dv), lambda bi, hi, lg: (bi, vo + hi)),
                  pl.BlockSpec((t, dv), lambda bi, hi, lg: (bi, go + hi)),
                  pl.BlockSpec((lc, dk), lambda bi, hi, lg: (bi, kco + hi)),
                  pl.BlockSpec((lc, dv), lambda bi, hi, lg: (bi, vco + hi)),
                  pl.BlockSpec((t, dk), lambda bi, hi, lg: (0, 0)),
                  pl.BlockSpec((t, dk), lambda bi, hi, lg: (0, 0))],
        out_specs=pl.BlockSpec((t, dv), lambda bi, hi, lg: (bi, hi)),
        scratch_shapes=[pltpu.VMEM((t, dk), F32), pltpu.VMEM((t, dk), F32),
                        pltpu.VMEM((t, dv), F32), pltpu.VMEM((2, dk, dv), F32)],
    )
    return pl.pallas_call(
        _ret_kernel,
        grid_spec=grid_spec,
        out_shape=jax.ShapeDtypeStruct((b * t, h * dv), BF16),
        compiler_params=_cparams(("arbitrary", "arbitrary")),
    )(lg, p, p, p, p, pc, pc, tab_c, tab_s)


def _attn_kernel(q_ref, kx_ref, vx_ref, kc_ref, vc_ref, qn_ref, kn_ref, c_ref, sl_ref, sh_ref, o_ref,
                 k_s, v_s):
    qt = pl.program_id(2)
    tq = q_ref.shape[0]
    lc = kc_ref.shape[0]
    hd = ATTN_HEAD_DIM
    scale = hd ** -0.5

    @pl.when(qt == 0)
    def _():
        kn = kn_ref[...]
        k_s[0:lc, :] = (_rms(kc_ref[...].astype(F32)) * kn).astype(BF16)
        kx = _rms(kx_ref[...].astype(F32)) * kn
        k_s[lc:, :] = _rope128(kx, c_ref[...], sl_ref[...], sh_ref[...]).astype(BF16)
        v_s[0:lc, :] = vc_ref[...]
        v_s[lc:, :] = vx_ref[...]

    r0 = pl.multiple_of(qt * tq, tq)
    c = c_ref[pl.ds(r0, tq), :]
    sl = sl_ref[pl.ds(r0, tq), :]
    sh = sh_ref[pl.ds(r0, tq), :]
    qn = qn_ref[...]
    kk = k_s[...]
    vv = v_s[...]
    for g in range(ATTN_HEADS // ATTN_KV_HEADS):
        q = _rms(q_ref[:, g * hd:(g + 1) * hd].astype(F32)) * qn
        q = (_rope128(q, c, sl, sh) * scale).astype(BF16)
        s = lax.dot_general(q, kk, (((1,), (1,)), ((), ())), preferred_element_type=F32)
        m = jnp.max(s, axis=-1, keepdims=True)
        p = jnp.exp(s - m)
        l = jnp.sum(p, axis=-1, keepdims=True)
        o = jnp.dot(p.astype(BF16), vv, preferred_element_type=F32) / l
        o_ref[:, g * hd:(g + 1) * hd] = o.astype(o_ref.dtype)


def _attention(p, pc, qn, kn, tabs, b, t, lc, offs, coffs):
    hd, kvh = ATTN_HEAD_DIM, ATTN_KV_HEADS
    gw = (ATTN_HEADS // kvh) * hd
    tq = _tile(t, 256)
    qo, ko, vo = _blk(offs["aq"], gw), _blk(offs["ak"], hd), _blk(offs["av"], hd)
    kco, vco = _blk(coffs["ak"], hd), _blk(coffs["av"], hd)
    tab = pl.BlockSpec((t, hd), lambda bi, ki, qi: (0, 0))
    return pl.pallas_call(
        _attn_kernel,
        grid=(b, kvh, t // tq),
        in_specs=[pl.BlockSpec((tq, gw), lambda bi, ki, qi: (bi * (t // tq) + qi, qo + ki)),
                  pl.BlockSpec((t, hd), lambda bi, ki, qi: (bi, ko + ki)),
                  pl.BlockSpec((t, hd), lambda bi, ki, qi: (bi, vo + ki)),
                  pl.BlockSpec((lc, hd), lambda bi, ki, qi: (bi, kco + ki)),
                  pl.BlockSpec((lc, hd), lambda bi, ki, qi: (bi, vco + ki)),
                  pl.BlockSpec((1, hd), lambda bi, ki, qi: (0, 0)),
                  pl.BlockSpec((1, hd), lambda bi, ki, qi: (0, 0)),
                  tab, tab, tab],
        out_specs=pl.BlockSpec((tq, gw), lambda bi, ki, qi: (bi * (t // tq) + qi, ki)),
        out_shape=jax.ShapeDtypeStruct((b * t, ATTN_HEADS * hd), BF16),
        scratch_shapes=[pltpu.VMEM((lc + t, hd), BF16), pltpu.VMEM((lc + t, hd), BF16)],
        compiler_params=_cparams(("arbitrary", "arbitrary", "arbitrary")),
    )(p, p, p, pc, pc, qn.reshape(1, hd), kn.reshape(1, hd), *tabs)


def _merge_kernel(r_ref, a_ref, gr_ref, ga_ref, wr_ref, wa_ref, o_ref):
    ret = jnp.dot(r_ref[...], wr_ref[...], preferred_element_type=F32)
    att = jnp.dot(a_ref[...], wa_ref[...], preferred_element_type=F32)
    z = (jax.nn.sigmoid(gr_ref[...].astype(F32)) * ret + jax.nn.sigmoid(ga_ref[...].astype(F32)) * att)
    o_ref[...] = z.astype(o_ref.dtype)


def _merge(retg, att, p, w_ret_o, w_attn_o, offs):
    m, kr = retg.shape
    ka = att.shape[1]
    d = w_ret_o.shape[1]
    tm, tn = _tile(m, 1024), _tile(d, 256 if d < 2048 else 512)
    gro, gao = _blk(offs["gr"], tn), _blk(offs["ga"], tn)
    return pl.pallas_call(
        _merge_kernel,
        grid=(m // tm, d // tn),
        in_specs=[pl.BlockSpec((tm, kr), lambda i, j: (i, 0)),
                  pl.BlockSpec((tm, ka), lambda i, j: (i, 0)),
                  pl.BlockSpec((tm, tn), lambda i, j: (i, gro + j)),
                  pl.BlockSpec((tm, tn), lambda i, j: (i, gao + j)),
                  pl.BlockSpec((kr, tn), lambda i, j: (0, j)),
                  pl.BlockSpec((ka, tn), lambda i, j: (0, j))],
        out_specs=pl.BlockSpec((tm, tn), lambda i, j: (i, j)),
        out_shape=jax.ShapeDtypeStruct((m, d), BF16),
        compiler_params=_cparams(("arbitrary", "arbitrary")),
    )(retg, att, p, p, w_ret_o, w_attn_o)


def _pack_pairs(x):
    hw = x.shape[1] // 2
    lo = pltpu.bitcast(x[:, :hw].astype(BF16).astype(F32), U32)
    hi = pltpu.bitcast(x[:, hw:].astype(BF16).astype(F32), U32)
    return (hi & jnp.uint32(0xFFFF0000)) | (lo >> 16)


def _unpack_pairs(w):
    lo = pltpu.bitcast(w << 16, F32)
    hi = pltpu.bitcast(w & jnp.uint32(0xFFFF0000), F32)
    return lo, hi


def _post_kernel(z_ref, wo_ref, x_ref, g1_ref, pw_ref, fw_ref, sc_ref, sh_ref, rw_ref,
                 x1_ref, f_ref, lg_ref):
    y = jnp.dot(z_ref[...], wo_ref[...], preferred_element_type=F32)
    x1 = x_ref[...] + g1_ref[0] * (_rms(y) * pw_ref[...])
    x1_ref[...] = x1
    f = (_rms(x1) * fw_ref[...]) * (1.0 + sc_ref[0]) + sh_ref[0]
    f_ref[...] = _pack_pairs(f)
    lg_ref[...] = lax.dot_general(rw_ref[...], f, (((1,), (1,)), ((), ())),
                                  precision=lax.Precision.HIGHEST, preferred_element_type=F32)


def _post(z, w_out, x2, g1, post_w, pre_w, sc2, sh2, router_wt, rows_per_mod):
    m, d = x2.shape
    e = router_wt.shape[0]
    tm = _tile(rows_per_mod, 256)
    per = rows_per_mod // tm
    g = g1.shape[0]
    vec = pl.BlockSpec((1, d), lambda i: (0, 0))
    mod = pl.BlockSpec((1, 1, d), lambda i: (i // per, 0, 0))
    return pl.pallas_call(
        _post_kernel,
        grid=(m // tm,),
        in_specs=[pl.BlockSpec((tm, d), lambda i: (i, 0)),
                  pl.BlockSpec((d, d), lambda i: (0, 0)),
                  pl.BlockSpec((tm, d), lambda i: (i, 0)),
                  mod, vec, vec, mod, mod,
                  pl.BlockSpec((e, d), lambda i: (0, 0))],
        out_specs=[pl.BlockSpec((tm, d), lambda i: (i, 0)),
                   pl.BlockSpec((tm, d // 2), lambda i: (i, 0)),
                   pl.BlockSpec((e, tm), lambda i: (0, i))],
        out_shape=[jax.ShapeDtypeStruct((m, d), F32),
                   jax.ShapeDtypeStruct((m, d // 2), U32),
                   jax.ShapeDtypeStruct((e, m), F32)],
        compiler_params=_cparams(("arbitrary",)),
    )(z, w_out, x2, g1.reshape(g, 1, d), post_w.reshape(1, d), pre_w.reshape(1, d),
      sc2.reshape(g, 1, d), sh2.reshape(g, 1, d), router_wt)


def _route_kernel(lg_ref, b_ref, tri_ref, idx_ref, w_ref, pos_ref, cnt_ref, carry):
    @pl.when(pl.program_id(0) == 0)
    def _():
        carry[...] = jnp.zeros_like(carry)

    e, tn = lg_ref.shape
    gs = e // N_GROUPS
    neg = -jnp.inf
    big = float(e)
    s = jax.nn.sigmoid(lg_ref[...])
    choice = s + b_ref[...]

    row_g = lax.broadcasted_iota(I32, (gs, tn), 0).astype(F32)
    rows = []
    for g in range(N_GROUPS):
        blk = choice[g * gs:(g + 1) * gs]
        m1 = jnp.max(blk, axis=0, keepdims=True)
        i1 = jnp.min(jnp.where(blk == m1, row_g, big), axis=0, keepdims=True)
        m2 = jnp.max(jnp.where(row_g == i1, neg, blk), axis=0, keepdims=True)
        rows.append(m1 + m2)
    gscore = jnp.concatenate(rows, axis=0)

    row_n = lax.broadcasted_iota(I32, (N_GROUPS, tn), 0).astype(F32)
    sel = jnp.zeros((N_GROUPS, tn), F32)
    cur = gscore
    for _ in range(TOPK_GROUPS):
        m = jnp.max(cur, axis=0, keepdims=True)
        gi = jnp.min(jnp.where(cur == m, row_n, big), axis=0, keepdims=True)
        hit = row_n == gi
        sel = jnp.where(hit, 1.0, sel)
        cur = jnp.where(hit, neg, cur)

    masked = jnp.concatenate(
        [jnp.where(sel[g:g + 1] > 0.0, choice[g * gs:(g + 1) * gs], neg) for g in range(N_GROUPS)], axis=0)

    row_e = lax.broadcasted_iota(I32, (e, tn), 0).astype(F32)
    onehot = jnp.zeros((e, tn), F32)
    cur = masked
    idxs, ws = [], []
    for _ in range(TOP_K):
        m = jnp.max(cur, axis=0, keepdims=True)
        ik = jnp.min(jnp.where(cur == m, row_e, big), axis=0, keepdims=True)
        hit = row_e == ik
        ws.append(jnp.sum(jnp.where(hit, s, 0.0), axis=0, keepdims=True))
        cur = jnp.where(hit, neg, cur)
        onehot = jnp.where(hit, 1.0, onehot)
        idxs.append(ik)
    w = jnp.concatenate(ws, axis=0)
    w = w / jnp.sum(w, axis=0, keepdims=True) * ROUTED_SCALE

    prefix = jnp.dot(onehot.astype(BF16), tri_ref[...], preferred_element_type=F32)
    base = prefix + carry[...]
    pos = jnp.concatenate(
        [jnp.sum(jnp.where(row_e == ik, base, 0.0), axis=0, keepdims=True) for ik in idxs], axis=0)
    carry[...] = carry[...] + jnp.sum(onehot, axis=1, keepdims=True)

    idx_ref[...] = jnp.concatenate(idxs, axis=0).astype(I32)
    w_ref[...] = w
    pos_ref[...] = pos.astype(I32)
    cnt_ref[...] = jnp.broadcast_to(carry[...], cnt_ref.shape).astype(I32)


def _route(logits_t, bias):
    e, n = logits_t.shape
    tn = _tile(n, 512)
    tri = (jnp.arange(tn)[:, None] < jnp.arange(tn)[None, :]).astype(BF16)
    kspec = pl.BlockSpec((TOP_K, tn), lambda i: (0, i))
    return pl.pallas_call(
        _route_kernel,
        grid=(n // tn,),
        in_specs=[pl.BlockSpec((e, tn), lambda i: (0, i)),
                  pl.BlockSpec((e, 1), lambda i: (0, 0)),
                  pl.BlockSpec((tn, tn), lambda i: (0, 0))],
        out_specs=[kspec, kspec, kspec, pl.BlockSpec((e, LANES), lambda i: (0, 0))],
        out_shape=[jax.ShapeDtypeStruct((TOP_K, n), I32), jax.ShapeDtypeStruct((TOP_K, n), F32),
                   jax.ShapeDtypeStruct((TOP_K, n), I32), jax.ShapeDtypeStruct((e, LANES), I32)],
        scratch_shapes=[pltpu.VMEM((e, 1), F32)],
        compiler_params=_cparams(("arbitrary",)),
    )(logits_t, bias.reshape(e, 1), tri)


def _expert_kernel(be_ref, nu_ref, tok_ref, tokn_ref, row_ref, rowp_ref, ws_ref, f_hbm, w1_ref, w3_ref, w2_ref,
                   y_hbm, xbuf, ybuf, w1b, w3b, w2b, gsem, ssem):
    b = pl.program_id(0)
    n_used = nu_ref[0]
    bm = xbuf.shape[1]
    hw = xbuf.shape[2]
    slot = b % 2

    def gather_copy(tok, r, s):
        return pltpu.make_async_copy(f_hbm.at[pl.ds(tok, 1), :], xbuf.at[s, pl.ds(r, 1), :], gsem.at[s])

    def scatter_copy(row, r, s):
        return pltpu.make_async_copy(ybuf.at[s, pl.ds(r, 1), :], y_hbm.at[pl.ds(row, 1), :], ssem.at[s])

    def start_gather(idx_ref, s):
        def body(r, c):
            gather_copy(idx_ref[0, 0, r], r, s).start()
            return c
        lax.fori_loop(0, bm, body, 0, unroll=8)

    def wait_gather(s):
        def body(r, c):
            gather_copy(0, r, s).wait()
            return c
        lax.fori_loop(0, bm, body, 0, unroll=8)

    def start_scatter(rows_ref, s):
        def body(r, c):
            row = rows_ref[0, 0, r]

            @pl.when(row >= 0)
            def _():
                scatter_copy(row, r, s).start()
            return c
        lax.fori_loop(0, bm, body, 0, unroll=8)

    def wait_scatter(rows_ref, s):
        def body(r, c):
            @pl.when(rows_ref[0, 0, r] >= 0)
            def _():
                scatter_copy(0, r, s).wait()
            return c
        lax.fori_loop(0, bm, body, 0, unroll=8)

    @pl.when(b < n_used)
    def _():
        @pl.when(b == 0)
        def _():
            start_gather(tok_ref, 0)

        @pl.when(b + 1 < n_used)
        def _():
            start_gather(tokn_ref, 1 - slot)

        first = jnp.logical_or(b == 0, be_ref[b] != be_ref[jnp.maximum(b - 1, 0)])

        @pl.when(first)
        def _():
            w1b[...] = w1_ref[0].astype(BF16)
            w3b[...] = w3_ref[0].astype(BF16)
            w2b[...] = w2_ref[0].astype(BF16)

        wait_gather(slot)
        lo, hi = _unpack_pairs(xbuf[slot])
        lo = lo.astype(BF16)
        hi = hi.astype(BF16)
        h1 = (jnp.dot(lo, w1b[0:hw, :], preferred_element_type=F32)
              + jnp.dot(hi, w1b[hw:, :], preferred_element_type=F32))
        h3 = (jnp.dot(lo, w3b[0:hw, :], preferred_element_type=F32)
              + jnp.dot(hi, w3b[hw:, :], preferred_element_type=F32))
        hid = (h1 * jax.nn.sigmoid(h1) * h3).astype(BF16)
        y = jnp.dot(hid, w2b[...], preferred_element_type=F32) * ws_ref[0]
        ybuf[slot] = _pack_pairs(y)
        start_scatter(row_ref, slot)

        @pl.when(b >= 1)
        def _():
            wait_scatter(rowp_ref, 1 - slot)

        @pl.when(b == n_used - 1)
        def _():
            wait_scatter(row_ref, slot)


def _experts(f_packed, blk_expert, n_used, slot_tok, slot_row, slot_w, w1, w3, w2, n_rows_out):
    nb = blk_expert.shape[0]
    bm = MOE_BLOCK
    n, hw = f_packed.shape
    e, d, fdim = w1.shape
    last = nb - 1
    smem = functools.partial(pl.BlockSpec, memory_space=pltpu.SMEM)
    grid_spec = pltpu.PrefetchScalarGridSpec(
        num_scalar_prefetch=2,
        grid=(nb,),
        in_specs=[smem((1, 1, bm), lambda b, be, nu: (b, 0, 0)),
                  smem((1, 1, bm), lambda b, be, nu: (jnp.minimum(b + 1, last), 0, 0)),
                  smem((1, 1, bm), lambda b, be, nu: (b, 0, 0)),
                  smem((1, 1, bm), lambda b, be, nu: (jnp.maximum(b - 1, 0), 0, 0)),
                  pl.BlockSpec((1, bm, 1), lambda b, be, nu: (b, 0, 0)),
                  pl.BlockSpec(memory_space=pl.ANY),
                  pl.BlockSpec((1, d, fdim), lambda b, be, nu: (be[b], 0, 0)),
                  pl.BlockSpec((1, d, fdim), lambda b, be, nu: (be[b], 0, 0)),
                  pl.BlockSpec((1, fdim, d), lambda b, be, nu: (be[b], 0, 0))],
        out_specs=pl.BlockSpec(memory_space=pl.ANY),
        scratch_shapes=[pltpu.VMEM((2, bm, hw), U32), pltpu.VMEM((2, bm, hw), U32),
                        pltpu.VMEM((d, fdim), BF16), pltpu.VMEM((d, fdim), BF16), pltpu.VMEM((fdim, d), BF16),
                        pltpu.SemaphoreType.DMA((2,)), pltpu.SemaphoreType.DMA((2,))],
    )
    return pl.pallas_call(
        _expert_kernel,
        grid_spec=grid_spec,
        out_shape=jax.ShapeDtypeStruct((n_rows_out, hw), U32),
        compiler_params=_cparams(("arbitrary",)),
    )(blk_expert, n_used, slot_tok.reshape(nb, 1, bm), slot_tok.reshape(nb, 1, bm),
      slot_row.reshape(nb, 1, bm), slot_row.reshape(nb, 1, bm), slot_w.reshape(nb, bm, 1), f_packed, w1, w3, w2)


def _combine_kernel(*refs):
    y_refs = refs[:TOP_K]
    f_ref, x1_ref, g2_ref, pw_ref, s1_ref, s3_ref, s2_ref, o_ref = refs[TOP_K:]
    hw = f_ref.shape[1]
    lo, hi = _unpack_pairs(f_ref[...])
    lo = lo.astype(BF16)
    hi = hi.astype(BF16)
    h1 = (jnp.dot(lo, s1_ref[0:hw, :], preferred_element_type=F32)
          + jnp.dot(hi, s1_ref[hw:, :], preferred_element_type=F32))
    h3 = (jnp.dot(lo, s3_ref[0:hw, :], preferred_element_type=F32)
          + jnp.dot(hi, s3_ref[hw:, :], preferred_element_type=F32))
    shared = jnp.dot((h1 * jax.nn.sigmoid(h1) * h3).astype(BF16), s2_ref[...], preferred_element_type=F32)
    r_lo, r_hi = _unpack_pairs(y_refs[0][...])
    for yr in y_refs[1:]:
        a, c = _unpack_pairs(yr[...])
        r_lo = r_lo + a
        r_hi = r_hi + c
    moe = jnp.concatenate([r_lo, r_hi], axis=1) + shared
    o_ref[...] = x1_ref[...] + g2_ref[0] * (_rms(moe) * pw_ref[...])


def _combine(y, f_packed, x1, g2, post_w, s1, s3, s2, rows_per_mod):
    m, d = x1.shape
    hw = d // 2
    fdim = s1.shape[1]
    tm = _tile(rows_per_mod, 256)
    per = rows_per_mod // tm
    g = g2.shape[0]
    nblk = m // tm
    y_specs = [pl.BlockSpec((tm, hw), functools.partial(lambda i, kk: (kk * nblk + i, 0), kk=k))
               for k in range(TOP_K)]
    return pl.pallas_call(
        _combine_kernel,
        grid=(nblk,),
        in_specs=y_specs + [pl.BlockSpec((tm, hw), lambda i: (i, 0)),
                            pl.BlockSpec((tm, d), lambda i: (i, 0)),
                            pl.BlockSpec((1, 1, d), lambda i: (i // per, 0, 0)),
                            pl.BlockSpec((1, d), lambda i: (0, 0)),
                            pl.BlockSpec((d, fdim), lambda i: (0, 0)),
                            pl.BlockSpec((d, fdim), lambda i: (0, 0)),
                            pl.BlockSpec((fdim, d), lambda i: (0, 0))],
        out_specs=pl.BlockSpec((tm, d), lambda i: (i, 0)),
        out_shape=jax.ShapeDtypeStruct((m, d), F32),
        compiler_params=_cparams(("arbitrary",)),
    )(*([y] * TOP_K), f_packed, x1, g2.reshape(g, 1, d), post_w.reshape(1, d), s1, s3, s2)


def _layer(x, c, ctx, c_ctx, ada_w, ada_b, mix_pre_norm, mix_post_norm, ffn_pre_norm, ffn_post_norm,
           w_in, ret_decay_fwd, ret_decay_bwd, attn_q_norm, attn_k_norm, w_ret_o, w_attn_o, w_out,
           router_w, router_b, exp_w1, exp_w3, exp_w2, shared_w1, shared_w3, shared_w2):
    b, t, d = x.shape
    lc = ctx.shape[1]
    n = b * t
    rq_w, rv_w = RET_HEADS * RET_QK_DIM, RET_HEADS * RET_V_DIM
    aq_w, akv_w = ATTN_HEADS * ATTN_HEAD_DIM, ATTN_KV_HEADS * ATTN_HEAD_DIM
    names = ("rq", "rk", "rv", "rg", "aq", "ak", "av", "gr", "ga")
    widths = (rq_w, rq_w, rv_w, rv_w, aq_w, akv_w, akv_w, d, d)
    offs, o = {}, 0
    for nm, wd in zip(names, widths):
        offs[nm] = o
        o += wd

    rows = -(-(b + 1) // 16) * 16
    cond = jnp.zeros((rows, d), F32).at[:b].set(c).at[b].set(c_ctx)
    mod = _adaln(cond, ada_w, ada_b)
    sh1, sc1, g1, sh2, sc2, g2 = [mod[:b, i * d:(i + 1) * d] for i in range(6)]
    ch1, cs1 = mod[b:b + 1, 0:d], mod[b:b + 1, d:2 * d]

    x2 = x.reshape(n, d)
    h_x = _norm_mod(x2, mix_pre_norm, sc1, sh1, t)
    h_c = _norm_mod(ctx.reshape(b * lc, d), mix_pre_norm, cs1, ch1, b * lc)

    w_in_b = w_in.astype(BF16)
    p = _matmul(h_x, w_in_b, BF16)
    cnames = ("rv", "rk", "ak", "av")
    coffs, o = {}, 0
    for nm in cnames:
        coffs[nm] = o
        o += widths[names.index(nm)]
    w_ctx = jnp.concatenate([w_in_b[:, offs[nm]:offs[nm] + widths[names.index(nm)]] for nm in cnames], axis=1)
    pc = _matmul(h_c, w_ctx, BF16)

    lg = jnp.stack([jax.nn.log_sigmoid(ret_decay_fwd.astype(F32)),
                    jax.nn.log_sigmoid(ret_decay_bwd.astype(F32))])
    rc, rs_lo, rs_hi = _rope_tables(t, RET_QK_DIM)
    retg = _retention(p, pc, lg, rc, rs_lo + rs_hi, b, t, lc, offs, coffs)
    att = _attention(p, pc, attn_q_norm, attn_k_norm, _rope_tables(t, ATTN_HEAD_DIM), b, t, lc, offs, coffs)

    z = _merge(retg, att, p, w_ret_o.astype(BF16), w_attn_o.astype(BF16), offs)
    x1, f_packed, logits_t = _post(z, w_out.astype(BF16), x2, g1, mix_post_norm, ffn_pre_norm, sc2, sh2,
                                   router_w.T, t)

    top_idx, top_w, pos, cnt = _route(logits_t, router_b)

    bm = MOE_BLOCK
    e = router_w.shape[1]
    nb = (n * TOP_K) // bm + e
    counts = cnt[:, 0]
    padded = (counts + bm - 1) // bm * bm
    pad_end = jnp.cumsum(padded)
    pad_off = pad_end - padded
    dest = (pad_off[top_idx] + pos).reshape(-1)
    tok = jnp.broadcast_to(jnp.arange(n, dtype=I32)[None, :], (TOP_K, n))
    out_row = (jnp.arange(TOP_K, dtype=I32)[:, None] * n + tok).reshape(-1)
    slot_tok = jnp.zeros((nb * bm,), I32).at[dest].set(tok.reshape(-1), unique_indices=True)
    slot_row = jnp.full((nb * bm,), -1, I32).at[dest].set(out_row, unique_indices=True)
    slot_w = jnp.zeros((nb * bm,), F32).at[dest].set(top_w.reshape(-1), unique_indices=True)
    blk_expert = jnp.minimum(
        jnp.searchsorted(pad_end, jnp.arange(nb, dtype=I32) * bm, side="right"), e - 1).astype(I32)
    n_used = (pad_end[-1] // bm).astype(I32).reshape(1)

    y = _experts(f_packed, blk_expert, n_used, slot_tok, slot_row, slot_w, exp_w1, exp_w3, exp_w2,
                 TOP_K * n)
    out = _combine(y, f_packed, x1, g2, ffn_post_norm, shared_w1.astype(BF16), shared_w3.astype(BF16),
                   shared_w2.astype(BF16), t)
    return out.reshape(b, t, d)


def kernel(x, c, ctx, c_ctx, ada_w, ada_b, mix_pre_norm, mix_post_norm, ffn_pre_norm, ffn_post_norm,
           w_in, ret_decay_fwd, ret_decay_bwd, attn_q_norm, attn_k_norm, w_ret_o, w_attn_o, w_out,
           router_w, router_b, exp_w1, exp_w3, exp_w2, shared_w1, shared_w3, shared_w2):
    assert ada_w.shape[0] == 1, "single-layer operation"
    return _layer(x, c, ctx, c_ctx, ada_w[0], ada_b[0], mix_pre_norm[0], mix_post_norm[0], ffn_pre_norm[0],
                  ffn_post_norm[0], w_in[0], ret_decay_fwd[0], ret_decay_bwd[0], attn_q_norm[0],
                  attn_k_norm[0], w_ret_o[0], w_attn_o[0], w_out[0], router_w[0], router_b[0],
                  exp_w1[0], exp_w3[0], exp_w2[0], shared_w1[0], shared_w3[0], shared_w2[0])
```

```python
import functools

import jax
import jax.numpy as jnp
import numpy as np
from jax import lax
from jax.experimental import pallas as pl
from jax.experimental.pallas import tpu as pltpu

F32 = jnp.float32
BF16 = jnp.bfloat16
U32 = jnp.uint32
I32 = jnp.int32

GRID_W = 64
NORM_EPS = 1e-6
ROPE_THETA = 10000.0
RET_HEADS = 4
RET_QK_DIM = 256
RET_V_DIM = 512
ATTN_HEADS = 16
ATTN_KV_HEADS = 4
ATTN_HEAD_DIM = 128
N_EXPERTS = 256
TOP_K = 8
N_GROUPS = 8
TOPK_GROUPS = 4
EXPERT_DIM = 512
ROUTED_SCALE = 2.5
LOG2E = 1.4426950408889634

V7X_VMEM_BYTES = 64 * 1024 * 1024
VMEM_LIMIT = V7X_VMEM_BYTES - 12 * 1024 * 1024
LANES = 128
SUBLANES = 8

RET_CHUNK = 256
MOE_BLOCK = 256
DMA_GROUP = 2


def _cparams(sem):
    return pltpu.CompilerParams(dimension_semantics=sem, vmem_limit_bytes=VMEM_LIMIT)


def _tile(dim, pref):
    if dim <= pref:
        return dim
    for t in range(pref, 0, -LANES):
        if dim % t == 0:
            return t
    raise ValueError((dim, pref))


def _blk(off, width):
    assert off % width == 0, (off, width)
    return off // width


def _ada_kernel(s_ref, w_ref, b_ref, o_ref):
    s = s_ref[...]
    s = (s * jax.nn.sigmoid(s)).astype(BF16)
    o_ref[...] = jnp.dot(s, w_ref[...].astype(BF16), preferred_element_type=F32) + b_ref[...]


def _adaln(cond, w, b):
    r, d = cond.shape
    n = w.shape[1]
    tn = _tile(n, 1024)
    return pl.pallas_call(
        _ada_kernel,
        grid=(n // tn,),
        in_specs=[pl.BlockSpec((r, d), lambda j: (0, 0)),
                  pl.BlockSpec((d, tn), lambda j: (0, j)),
                  pl.BlockSpec((1, tn), lambda j: (0, j))],
        out_specs=pl.BlockSpec((r, tn), lambda j: (0, j)),
        out_shape=jax.ShapeDtypeStruct((r, n), F32),
        compiler_params=_cparams(("arbitrary",)),
    )(cond, w, b.reshape(1, n))


def _rms(x):
    return x * lax.rsqrt(jnp.mean(x * x, axis=-1, keepdims=True) + NORM_EPS)


def _norm_mod_kernel(x_ref, w_ref, sc_ref, sh_ref, o_ref):
    y = _rms(x_ref[...]) * w_ref[...]
    o_ref[...] = (y * (1.0 + sc_ref[0]) + sh_ref[0]).astype(o_ref.dtype)


def _norm_mod(x2, w, sc, sh, rows_per_mod):
    m, d = x2.shape
    tm = _tile(rows_per_mod, 512)
    per = rows_per_mod // tm
    g = sc.shape[0]
    return pl.pallas_call(
        _norm_mod_kernel,
        grid=(m // tm,),
        in_specs=[pl.BlockSpec((tm, d), lambda i: (i, 0)),
                  pl.BlockSpec((1, d), lambda i: (0, 0)),
                  pl.BlockSpec((1, 1, d), lambda i: (i // per, 0, 0)),
                  pl.BlockSpec((1, 1, d), lambda i: (i // per, 0, 0))],
        out_specs=pl.BlockSpec((tm, d), lambda i: (i, 0)),
        out_shape=jax.ShapeDtypeStruct((m, d), BF16),
        compiler_params=_cparams(("arbitrary",)),
    )(x2, w.reshape(1, d), sc.reshape(g, 1, d), sh.reshape(g, 1, d))


def _mm_kernel(a_ref, b_ref, o_ref):
    o_ref[...] = jnp.dot(a_ref[...], b_ref[...], preferred_element_type=F32).astype(o_ref.dtype)


def _matmul(a, b, out_dtype, tm_pref=1024, tn_pref=1024):
    m, k = a.shape
    n = b.shape[1]
    tm, tn = _tile(m, tm_pref), _tile(n, tn_pref)
    return pl.pallas_call(
        _mm_kernel,
        grid=(m // tm, n // tn),
        in_specs=[pl.BlockSpec((tm, k), lambda i, j: (i, 0)),
                  pl.BlockSpec((k, tn), lambda i, j: (0, j))],
        out_specs=pl.BlockSpec((tm, tn), lambda i, j: (i, j)),
        out_shape=jax.ShapeDtypeStruct((m, n), out_dtype),
        compiler_params=_cparams(("arbitrary", "arbitrary")),
    )(a, b)


def _rope_tables(t, dim):
    half = dim // 2
    pos = jnp.arange(t, dtype=I32)
    row = (pos // GRID_W).astype(F32)
    col = (pos % GRID_W).astype(F32)
    inv = ROPE_THETA ** (-jnp.arange(0, half, 2, dtype=F32) / half)
    ar = row[:, None] * inv[None, :]
    ac = col[:, None] * inv[None, :]
    cr, sr, cc, sc = jnp.cos(ar), jnp.sin(ar), jnp.cos(ac), jnp.sin(ac)
    z = jnp.zeros_like(sr)
    c = jnp.concatenate([cr, cr, cc, cc], axis=1)
    s_lo = jnp.concatenate([-sr, z, -sc, z], axis=1)
    s_hi = jnp.concatenate([z, sr, z, sc], axis=1)
    return c, s_lo, s_hi


def _rope128(x, c, s_lo, s_hi):
    q4 = ATTN_HEAD_DIM // 4
    return (x * c + pltpu.roll(x, LANES - q4, axis=1) * s_lo + pltpu.roll(x, q4, axis=1) * s_hi)


def _rope256(x, c, s):
    parts = [pltpu.roll(x[:, i * LANES:(i + 1) * LANES], LANES // 2, axis=1)
             for i in range(RET_QK_DIM // LANES)]
    return x * c + jnp.concatenate(parts, axis=1) * s


def _ret_kernel(lg_ref, q_ref, k_ref, v_ref, g_ref, kc_ref, vc_ref, c_ref, s_ref, o_ref,
                qs, ks, acc, st):
    h = pl.program_id(1)
    lgf = lg_ref[0, h]
    lgb = lg_ref[1, h]
    t = q_ref.shape[0]
    ch = RET_CHUNK if t % RET_CHUNK == 0 else t
    nch = t // ch
    lc = kc_ref.shape[0]
    k_scale = RET_QK_DIM ** -0.5
    tn_dims = (((0,), (0,)), ((), ()))

    ii = lax.broadcasted_iota(I32, (ch, ch), 0)
    jj = lax.broadcasted_iota(I32, (ch, ch), 1)
    dif = (ii - jj).astype(F32)
    dmat = (jnp.where(dif >= 0, jnp.exp(lgf * jnp.maximum(dif, 0.0)), 0.0)
            + jnp.where(dif <= 0, jnp.exp(lgb * jnp.maximum(-dif, 0.0)), 0.0))
    pos = lax.broadcasted_iota(I32, (ch, 1), 0).astype(F32)
    qf_dec = jnp.exp(lgf * (pos + 1.0))
    kf_dec = jnp.exp(lgf * (ch - 1.0 - pos))
    qb_dec = jnp.exp(lgb * (ch - pos))
    kb_dec = jnp.exp(lgb * pos)
    chv = jnp.full((1, 1), float(ch), F32)
    cf_dec = jnp.exp(lgf * chv)
    cb_dec = jnp.exp(lgb * chv)

    cpos = lax.broadcasted_iota(I32, (lc, 1), 0).astype(F32)
    kc = kc_ref[...].astype(F32) * k_scale
    vc = vc_ref[...]
    st[0] = lax.dot_general((kc * jnp.exp(lgf * (lc - 1.0 - cpos))).astype(BF16), vc, tn_dims,
                            preferred_element_type=F32)
    st[1] = lax.dot_general((kc * jnp.exp(lgb * cpos)).astype(BF16), vc, tn_dims,
                            preferred_element_type=F32)

    def fwd(c, carry):
        r0 = pl.multiple_of(c * ch, ch)
        cs = c_ref[pl.ds(r0, ch), :]
        sn = s_ref[pl.ds(r0, ch), :]
        q = _rope256(q_ref[pl.ds(r0, ch), :].astype(F32), cs, sn)
        k = _rope256(k_ref[pl.ds(r0, ch), :].astype(F32), cs, sn) * k_scale
        v = v_ref[pl.ds(r0, ch), :]
        qb = q.astype(BF16)
        kb = k.astype(BF16)
        qs[pl.ds(r0, ch), :] = q
        ks[pl.ds(r0, ch), :] = k
        sc = lax.dot_general(qb, kb, (((1,), (1,)), ((), ())), preferred_element_type=F32) * dmat
        o = jnp.dot(sc.astype(BF16), v, preferred_element_type=F32)
        o += jnp.dot((q * qf_dec).astype(BF16), st[0].astype(BF16), preferred_element_type=F32)
        acc[pl.ds(r0, ch), :] = o
        st[0] = st[0] * cf_dec + lax.dot_general((k * kf_dec).astype(BF16), v, tn_dims,
                                                 preferred_element_type=F32)
        return carry

    lax.fori_loop(0, nch, fwd, 0)

    def bwd(i, carry):
        c = nch - 1 - i
        r0 = pl.multiple_of(c * ch, ch)
        q = qs[pl.ds(r0, ch), :]
        k = ks[pl.ds(r0, ch), :]
        v = v_ref[pl.ds(r0, ch), :]
        o = jnp.dot((q * qb_dec).astype(BF16), st[1].astype(BF16), preferred_element_type=F32)
        tot = acc[pl.ds(r0, ch), :] + o
        gate = g_ref[pl.ds(r0, ch), :].astype(F32)
        o_ref[pl.ds(r0, ch), :] = (gate * jax.nn.sigmoid(gate) * _rms(tot)).astype(o_ref.dtype)
        st[1] = st[1] * cb_dec + lax.dot_general((k * kb_dec).astype(BF16), v, tn_dims,
                                                 preferred_element_type=F32)
        return carry

    lax.fori_loop(0, nch, bwd, 0)


def _retention(p, pc, lg, tab_c, tab_s, b, t, lc, offs, coffs):
    h, dk, dv = RET_HEADS, RET_QK_DIM, RET_V_DIM
    qo, ko, vo, go = (_blk(offs["rq"], dk), _blk(offs["rk"], dk), _blk(offs["rv"], dv), _blk(offs["rg"], dv))
    kco, vco = _blk(coffs["rk"], dk), _blk(coffs["rv"], dv)
    grid_spec = pltpu.PrefetchScalarGridSpec(
        num_scalar_prefetch=1,
        grid=(b, h),
        in_specs=[pl.BlockSpec((t, dk), lambda bi, hi, lg: (bi, qo + hi)),
                  pl.BlockSpec((t, dk), lambda bi, hi, lg: (bi, ko + hi)),
                  pl.BlockSpec((t, dv), lambda bi, hi, lg: (bi, vo + hi)),
                  pl.BlockSpec((t, dv), lambda bi, hi, lg: (bi, go + hi)),
                  pl.BlockSpec((lc, dk), lambda bi, hi, lg: (bi, kco + hi)),
                  pl.BlockSpec((lc, dv), lambda bi, hi, lg: (bi, vco + hi)),
                  pl.BlockSpec((t, dk), lambda bi, hi, lg: (0, 0)),
                  pl.BlockSpec((t, dk), lambda bi, hi, lg: (0, 0))],
        out_specs=pl.BlockSpec((t, dv), lambda bi, hi, lg: (bi, hi)),
        scratch_shapes=[pltpu.VMEM((t, dk), F32), pltpu.VMEM((t, dk), F32),
                        pltpu.VMEM((t, dv), F32), pltpu.VMEM((2, dk, dv), F32)],
    )
    return pl.pallas_call(
        _ret_kernel,
        grid_spec=grid_spec,
        out_shape=jax.ShapeDtypeStruct((b * t, h * dv), BF16),
        compiler_params=_cparams(("arbitrary", "arbitrary")),
    )(lg, p, p, p, p, pc, pc, tab_c, tab_s)


def _attn_kernel(q_ref, kx_ref, vx_ref, kc_ref, vc_ref, qn_ref, kn_ref, c_ref, sl_ref, sh_ref, o_ref,
                 k_s, v_s):
    qt = pl.program_id(2)
    tq = q_ref.shape[0]
    lc = kc_ref.shape[0]
    hd = ATTN_HEAD_DIM
    scale = hd ** -0.5 * LOG2E

    @pl.when(qt == 0)
    def _():
        kn = kn_ref[...]
        k_s[0:lc, :] = (_rms(kc_ref[...].astype(F32)) * kn).astype(BF16)
        kx = _rms(kx_ref[...].astype(F32)) * kn
        k_s[lc:, :] = _rope128(kx, c_ref[...], sl_ref[...], sh_ref[...]).astype(BF16)
        v_s[0:lc, 0:hd] = vc_ref[...]
        v_s[lc:, 0:hd] = vx_ref[...]
        col = lax.broadcasted_iota(I32, (v_s.shape[0], hd), 1)
        v_s[:, hd:] = jnp.where(col == 0, 1.0, 0.0).astype(BF16)

    r0 = pl.multiple_of(qt * tq, tq)
    c = c_ref[pl.ds(r0, tq), :]
    sl = sl_ref[pl.ds(r0, tq), :]
    sh = sh_ref[pl.ds(r0, tq), :]
    qn = qn_ref[...]
    kk = k_s[...]
    vv = v_s[...]
    for g in range(ATTN_HEADS // ATTN_KV_HEADS):
        q = _rms(q_ref[:, g * hd:(g + 1) * hd].astype(F32)) * qn
        q = (_rope128(q, c, sl, sh) * scale).astype(BF16)
        s = lax.dot_general(q, kk, (((1,), (1,)), ((), ())), preferred_element_type=F32)
        m = jnp.max(s, axis=-1, keepdims=True)
        p = jnp.exp2(s - m).astype(BF16)
        ov = jnp.dot(p, vv, preferred_element_type=F32)
        o = ov[:, 0:hd] / ov[:, hd:hd + 1]
        o_ref[:, g * hd:(g + 1) * hd] = o.astype(o_ref.dtype)


def _attention(p, pc, qn, kn, tabs, b, t, lc, offs, coffs):
    hd, kvh = ATTN_HEAD_DIM, ATTN_KV_HEADS
    gw = (ATTN_HEADS // kvh) * hd
    tq = _tile(t, 256)
    qo, ko, vo = _blk(offs["aq"], gw), _blk(offs["ak"], hd), _blk(offs["av"], hd)
    kco, vco = _blk(coffs["ak"], hd), _blk(coffs["av"], hd)
    tab = pl.BlockSpec((t, hd), lambda bi, ki, qi: (0, 0))
    return pl.pallas_call(
        _attn_kernel,
        grid=(b, kvh, t // tq),
        in_specs=[pl.BlockSpec((tq, gw), lambda bi, ki, qi: (bi * (t // tq) + qi, qo + ki)),
                  pl.BlockSpec((t, hd), lambda bi, ki, qi: (bi, ko + ki)),
                  pl.BlockSpec((t, hd), lambda bi, ki, qi: (bi, vo + ki)),
                  pl.BlockSpec((lc, hd), lambda bi, ki, qi: (bi, kco + ki)),
                  pl.BlockSpec((lc, hd), lambda bi, ki, qi: (bi, vco + ki)),
                  pl.BlockSpec((1, hd), lambda bi, ki, qi: (0, 0)),
                  pl.BlockSpec((1, hd), lambda bi, ki, qi: (0, 0)),
                  tab, tab, tab],
        out_specs=pl.BlockSpec((tq, gw), lambda bi, ki, qi: (bi * (t // tq) + qi, ki)),
        out_shape=jax.ShapeDtypeStruct((b * t, ATTN_HEADS * hd), BF16),
        scratch_shapes=[pltpu.VMEM((lc + t, hd), BF16), pltpu.VMEM((lc + t, 2 * hd), BF16)],
        compiler_params=_cparams(("arbitrary", "arbitrary", "arbitrary")),
    )(p, p, p, pc, pc, qn.reshape(1, hd), kn.reshape(1, hd), *tabs)


def _merge_kernel(r_ref, a_ref, gr_ref, ga_ref, wr_ref, wa_ref, o_ref):
    ret = jnp.dot(r_ref[...], wr_ref[...], preferred_element_type=F32)
    att = jnp.dot(a_ref[...], wa_ref[...], preferred_element_type=F32)
    z = (jax.nn.sigmoid(gr_ref[...].astype(F32)) * ret + jax.nn.sigmoid(ga_ref[...].astype(F32)) * att)
    o_ref[...] = z.astype(o_ref.dtype)


def _merge(retg, att, p, w_ret_o, w_attn_o, offs):
    m, kr = retg.shape
    ka = att.shape[1]
    d = w_ret_o.shape[1]
    tm, tn = _tile(m, 1024), _tile(d, 256 if d < 2048 else 512)
    gro, gao = _blk(offs["gr"], tn), _blk(offs["ga"], tn)
    return pl.pallas_call(
        _merge_kernel,
        grid=(m // tm, d // tn),
        in_specs=[pl.BlockSpec((tm, kr), lambda i, j: (i, 0)),
                  pl.BlockSpec((tm, ka), lambda i, j: (i, 0)),
                  pl.BlockSpec((tm, tn), lambda i, j: (i, gro + j)),
                  pl.BlockSpec((tm, tn), lambda i, j: (i, gao + j)),
                  pl.BlockSpec((kr, tn), lambda i, j: (0, j)),
                  pl.BlockSpec((ka, tn), lambda i, j: (0, j))],
        out_specs=pl.BlockSpec((tm, tn), lambda i, j: (i, j)),
        out_shape=jax.ShapeDtypeStruct((m, d), BF16),
        compiler_params=_cparams(("arbitrary", "arbitrary")),
    )(retg, att, p, p, w_ret_o, w_attn_o)


def _pack_pairs(x):
    hw = x.shape[1] // 2
    lo = pltpu.bitcast(x[:, :hw].astype(BF16).astype(F32), U32)
    hi = pltpu.bitcast(x[:, hw:].astype(BF16).astype(F32), U32)
    return (hi & jnp.uint32(0xFFFF0000)) | (lo >> 16)


def _store_token_tiles(ref, packed):
    m = packed.shape[0]
    st = packed.shape[1] // LANES
    for s in range(st):
        ref[pl.ds(s, m, stride=st), :] = packed[:, s * LANES:(s + 1) * LANES]


def _load_token_tiles(ref, m):
    st = ref.shape[0] // m
    return jnp.concatenate([ref[pl.ds(s, m, stride=st), :] for s in range(st)], axis=1)


def _unpack_pairs(w):
    lo = pltpu.bitcast(w << 16, F32)
    hi = pltpu.bitcast(w & jnp.uint32(0xFFFF0000), F32)
    return lo, hi


def _post_kernel(z_ref, wo_ref, x_ref, g1_ref, pw_ref, fw_ref, sc_ref, sh_ref, rwh_ref, rwl_ref,
                 x1_ref, f_ref, lg_ref):
    y = jnp.dot(z_ref[...], wo_ref[...], preferred_element_type=F32)
    x1 = x_ref[...] + g1_ref[0] * (_rms(y) * pw_ref[...])
    x1_ref[...] = x1
    f = (_rms(x1) * fw_ref[...]) * (1.0 + sc_ref[0]) + sh_ref[0]
    _store_token_tiles(f_ref, _pack_pairs(f))
    fh = f.astype(BF16)
    fl = (f - fh.astype(F32)).astype(BF16)
    nt = (((1,), (1,)), ((), ()))
    lg_ref[...] = (lax.dot_general(rwh_ref[...], fh, nt, preferred_element_type=F32)
                   + lax.dot_general(rwh_ref[...], fl, nt, preferred_element_type=F32)
                   + lax.dot_general(rwl_ref[...], fh, nt, preferred_element_type=F32))


def _post(z, w_out, x2, g1, post_w, pre_w, sc2, sh2, router_wt, rows_per_mod):
    m, d = x2.shape
    e = router_wt.shape[0]
    st = d // 2 // LANES
    assert st == SUBLANES, "token rows must be whole (8, 128) tiles"
    rw_hi = router_wt.astype(BF16)
    rw_lo = (router_wt - rw_hi.astype(F32)).astype(BF16)
    tm = _tile(rows_per_mod, 256)
    per = rows_per_mod // tm
    g = g1.shape[0]
    vec = pl.BlockSpec((1, d), lambda i: (0, 0))
    mod = pl.BlockSpec((1, 1, d), lambda i: (i // per, 0, 0))
    return pl.pallas_call(
        _post_kernel,
        grid=(m // tm,),
        in_specs=[pl.BlockSpec((tm, d), lambda i: (i, 0)),
                  pl.BlockSpec((d, d), lambda i: (0, 0)),
                  pl.BlockSpec((tm, d), lambda i: (i, 0)),
                  mod, vec, vec, mod, mod,
                  pl.BlockSpec((e, d), lambda i: (0, 0)),
                  pl.BlockSpec((e, d), lambda i: (0, 0))],
        out_specs=[pl.BlockSpec((tm, d), lambda i: (i, 0)),
                   pl.BlockSpec((tm * st, LANES), lambda i: (i, 0)),
                   pl.BlockSpec((e, tm), lambda i: (0, i))],
        out_shape=[jax.ShapeDtypeStruct((m, d), F32),
                   jax.ShapeDtypeStruct((m * st, LANES), U32),
                   jax.ShapeDtypeStruct((e, m), F32)],
        compiler_params=_cparams(("arbitrary",)),
    )(z, w_out, x2, g1.reshape(g, 1, d), post_w.reshape(1, d), pre_w.reshape(1, d),
      sc2.reshape(g, 1, d), sh2.reshape(g, 1, d), rw_hi, rw_lo)


def _route_kernel(lg_ref, b_ref, tri_ref, idx_ref, w_ref, pos_ref, cnt_ref, carry):
    @pl.when(pl.program_id(0) == 0)
    def _():
        carry[...] = jnp.zeros_like(carry)

    e, tn = lg_ref.shape
    gs = e // N_GROUPS
    neg = -jnp.inf
    big = float(e)
    s = jax.nn.sigmoid(lg_ref[...])
    choice = s + b_ref[...]

    row_g = lax.broadcasted_iota(I32, (gs, tn), 0).astype(F32)
    rows = []
    for g in range(N_GROUPS):
        blk = choice[g * gs:(g + 1) * gs]
        m1 = jnp.max(blk, axis=0, keepdims=True)
        i1 = jnp.min(jnp.where(blk == m1, row_g, big), axis=0, keepdims=True)
        m2 = jnp.max(jnp.where(row_g == i1, neg, blk), axis=0, keepdims=True)
        rows.append(m1 + m2)
    gscore = jnp.concatenate(rows, axis=0)

    row_n = lax.broadcasted_iota(I32, (N_GROUPS, tn), 0).astype(F32)
    sel = jnp.zeros((N_GROUPS, tn), F32)
    cur = gscore
    for _ in range(TOPK_GROUPS):
        m = jnp.max(cur, axis=0, keepdims=True)
        gi = jnp.min(jnp.where(cur == m, row_n, big), axis=0, keepdims=True)
        hit = row_n == gi
        sel = jnp.where(hit, 1.0, sel)
        cur = jnp.where(hit, neg, cur)

    masked = jnp.concatenate(
        [jnp.where(sel[g:g + 1] > 0.0, choice[g * gs:(g + 1) * gs], neg) for g in range(N_GROUPS)], axis=0)

    row_e = lax.broadcasted_iota(I32, (e, tn), 0).astype(F32)
    onehot = jnp.zeros((e, tn), F32)
    cur = masked
    idxs, ws = [], []
    for _ in range(TOP_K):
        m = jnp.max(cur, axis=0, keepdims=True)
        ik = jnp.min(jnp.where(cur == m, row_e, big), axis=0, keepdims=True)
        hit = row_e == ik
        ws.append(jnp.sum(jnp.where(hit, s, 0.0), axis=0, keepdims=True))
        cur = jnp.where(hit, neg, cur)
        onehot = jnp.where(hit, 1.0, onehot)
        idxs.append(ik)
    w = jnp.concatenate(ws, axis=0)
    w = w / jnp.sum(w, axis=0, keepdims=True) * ROUTED_SCALE

    prefix = jnp.dot(onehot.astype(BF16), tri_ref[...], preferred_element_type=F32)
    base = prefix + carry[...]
    pos = jnp.concatenate(
        [jnp.sum(jnp.where(row_e == ik, base, 0.0), axis=0, keepdims=True) for ik in idxs], axis=0)
    carry[...] = carry[...] + jnp.sum(onehot, axis=1, keepdims=True)

    idx_ref[...] = jnp.concatenate(idxs, axis=0).astype(I32)
    w_ref[...] = w
    pos_ref[...] = pos.astype(I32)
    cnt_ref[...] = jnp.broadcast_to(carry[...], cnt_ref.shape).astype(I32)


def _route(logits_t, bias):
    e, n = logits_t.shape
    tn = _tile(n, 512)
    tri = (jnp.arange(tn)[:, None] < jnp.arange(tn)[None, :]).astype(BF16)
    kspec = pl.BlockSpec((TOP_K, tn), lambda i: (0, i))
    return pl.pallas_call(
        _route_kernel,
        grid=(n // tn,),
        in_specs=[pl.BlockSpec((e, tn), lambda i: (0, i)),
                  pl.BlockSpec((e, 1), lambda i: (0, 0)),
                  pl.BlockSpec((tn, tn), lambda i: (0, 0))],
        out_specs=[kspec, kspec, kspec, pl.BlockSpec((e, LANES), lambda i: (0, 0))],
        out_shape=[jax.ShapeDtypeStruct((TOP_K, n), I32), jax.ShapeDtypeStruct((TOP_K, n), F32),
                   jax.ShapeDtypeStruct((TOP_K, n), I32), jax.ShapeDtypeStruct((e, LANES), I32)],
        scratch_shapes=[pltpu.VMEM((e, 1), F32)],
        compiler_params=_cparams(("arbitrary",)),
    )(logits_t, bias.reshape(e, 1), tri)


def _dest_kernel(idx_ref, pos_ref, off_ref, o_ref):
    e = off_ref.shape[0]
    tn = idx_ref.shape[1]
    row_e = lax.broadcasted_iota(I32, (e, tn), 0)
    off = off_ref[...]
    idx = idx_ref[...]
    rows = [jnp.sum(jnp.where(row_e == idx[k:k + 1], off, 0.0), axis=0, keepdims=True) for k in range(TOP_K)]
    o_ref[...] = jnp.concatenate(rows, axis=0).astype(I32) + pos_ref[...]


def _dest(top_idx, pos, pad_off):
    k, n = top_idx.shape
    e = pad_off.shape[0]
    tn = _tile(n, 512)
    spec = pl.BlockSpec((k, tn), lambda i: (0, i))
    return pl.pallas_call(
        _dest_kernel,
        grid=(n // tn,),
        in_specs=[spec, spec, pl.BlockSpec((e, 1), lambda i: (0, 0))],
        out_specs=spec,
        out_shape=jax.ShapeDtypeStruct((k, n), I32),
        compiler_params=_cparams(("arbitrary",)),
    )(top_idx, pos, pad_off.astype(F32).reshape(e, 1))


def _dispatch_kernel(cnt_ref, off_ref, nu_ref, dest_ref, f_hbm, xs_hbm, zbuf, sem, zsem):
    i = pl.program_id(0)
    tt = dest_ref.shape[1]
    st = SUBLANES
    bm = zbuf.shape[0] // st
    n_exp = cnt_ref.shape[0]
    nb = xs_hbm.shape[0] // (bm * st)

    def zero_fill(wait):
        def fill(start, rows):
            cp = pltpu.make_async_copy(zbuf.at[pl.ds(0, rows * st), :],
                                       xs_hbm.at[pl.ds(pl.multiple_of(start * st, st), rows * st), :], zsem)
            if wait:
                cp.wait()
            else:
                cp.start()

        def per_expert(e, c):
            cnt = cnt_ref[e]
            npad = (-cnt) & (bm - 1)
            start = off_ref[e] + cnt
            p = bm // 2
            while p >= 1:
                @pl.when((npad & p) != 0)
                def _(start=start, p=p):
                    fill(start, p)
                start = start + (npad & p)
                p //= 2
            return c
        lax.fori_loop(0, n_exp, per_expert, 0)

        def per_block(b, c):
            fill(b * bm, bm)
            return c
        lax.fori_loop(nu_ref[0], nb, per_block, 0)

    def row_copy(tok, dst):
        return pltpu.make_async_copy(f_hbm.at[pl.ds(pl.multiple_of(tok * st, st), st), :],
                                     xs_hbm.at[pl.ds(pl.multiple_of(dst * st, st), st), :], sem)

    @pl.when(i == 0)
    def _():
        zbuf[...] = jnp.zeros_like(zbuf)
        zero_fill(False)

    def issue(j, c):
        base = j * DMA_GROUP
        dsts = [[dest_ref[k, base + u] for k in range(TOP_K)] for u in range(DMA_GROUP)]
        for u in range(DMA_GROUP):
            for k in range(TOP_K):
                row_copy(i * tt + base + u, dsts[u][k]).start()
        return c
    lax.fori_loop(0, tt // DMA_GROUP, issue, 0)

    def drain(j, c):
        for k in range(TOP_K):
            row_copy(0, 0).wait()
        return c
    lax.fori_loop(0, tt, drain, 0, unroll=2)

    @pl.when(i == pl.num_programs(0) - 1)
    def _():
        zero_fill(True)


def _dispatch(counts, pad_off, n_used, dest, f_packed, n_slots):
    st = SUBLANES
    n = f_packed.shape[0] // st
    tt = _tile(n, 512)
    grid_spec = pltpu.PrefetchScalarGridSpec(
        num_scalar_prefetch=3,
        grid=(n // tt,),
        in_specs=[pl.BlockSpec((TOP_K, tt), lambda i, c, o, u: (0, i), memory_space=pltpu.SMEM),
                  pl.BlockSpec(memory_space=pl.ANY)],
        out_specs=pl.BlockSpec(memory_space=pl.ANY),
        scratch_shapes=[pltpu.VMEM((MOE_BLOCK * st, LANES), U32), pltpu.SemaphoreType.DMA(()),
                        pltpu.SemaphoreType.DMA(())],
    )
    return pl.pallas_call(
        _dispatch_kernel,
        grid_spec=grid_spec,
        out_shape=jax.ShapeDtypeStruct((n_slots * st, LANES), U32),
        compiler_params=_cparams(("arbitrary",)),
    )(counts, pad_off, n_used, dest, f_packed)


def _expert_kernel(be_ref, nu_ref, x_ref, w1_ref, w3_ref, w2_ref, y_ref, w1b, w3b, w2b):
    b = pl.program_id(0)
    n_used = nu_ref[0]
    bm = MOE_BLOCK
    hw = w1b.shape[0] // 2

    @pl.when(b < n_used)
    def _():
        first = jnp.logical_or(b == 0, be_ref[b] != be_ref[jnp.maximum(b - 1, 0)])

        @pl.when(first)
        def _():
            w1b[...] = w1_ref[0].astype(BF16)
            w3b[...] = w3_ref[0].astype(BF16)
            w2b[...] = w2_ref[0].astype(BF16)

        lo, hi = _unpack_pairs(_load_token_tiles(x_ref, bm))
        lo = lo.astype(BF16)
        hi = hi.astype(BF16)
        h1 = (jnp.dot(lo, w1b[0:hw, :], preferred_element_type=F32)
              + jnp.dot(hi, w1b[hw:, :], preferred_element_type=F32))
        h3 = (jnp.dot(lo, w3b[0:hw, :], preferred_element_type=F32)
              + jnp.dot(hi, w3b[hw:, :], preferred_element_type=F32))
        hid = (h1 * jax.nn.sigmoid(h1) * h3).astype(BF16)
        _store_token_tiles(y_ref, _pack_pairs(jnp.dot(hid, w2b[...], preferred_element_type=F32)))

    @pl.when(b >= n_used)
    def _():
        y_ref[...] = jnp.zeros_like(y_ref)


def _experts(xs, blk_expert, n_used, w1, w3, w2):
    nb = blk_expert.shape[0]
    e, d, fdim = w1.shape
    rows = MOE_BLOCK * (d // 2 // LANES)

    def used(b, nu):
        return jnp.minimum(b, nu[0] - 1)

    grid_spec = pltpu.PrefetchScalarGridSpec(
        num_scalar_prefetch=2,
        grid=(nb,),
        in_specs=[pl.BlockSpec((rows, LANES), lambda b, be, nu: (used(b, nu), 0)),
                  pl.BlockSpec((1, d, fdim), lambda b, be, nu: (be[used(b, nu)], 0, 0)),
                  pl.BlockSpec((1, d, fdim), lambda b, be, nu: (be[used(b, nu)], 0, 0)),
                  pl.BlockSpec((1, fdim, d), lambda b, be, nu: (be[used(b, nu)], 0, 0))],
        out_specs=pl.BlockSpec((rows, LANES), lambda b, be, nu: (b, 0)),
        scratch_shapes=[pltpu.VMEM((d, fdim), BF16), pltpu.VMEM((d, fdim), BF16), pltpu.VMEM((fdim, d), BF16)],
    )
    return pl.pallas_call(
        _expert_kernel,
        grid_spec=grid_spec,
        out_shape=jax.ShapeDtypeStruct((nb * rows, LANES), U32),
        compiler_params=_cparams(("arbitrary",)),
    )(blk_expert, n_used, xs, w1, w3, w2)


def _combine_kernel(dcur_ref, dnxt_ref, wt_ref, f_ref, x1_ref, g2_ref, pw_ref, s1_ref, s3_ref, s2_ref, ys_hbm,
                    o_ref, ybuf, sem):
    i = pl.program_id(0)
    tm = x1_ref.shape[0]
    hw = x1_ref.shape[1] // 2
    st = f_ref.shape[0] // tm
    slot = i % 2

    def row_copy(src, k, r, s):
        return pltpu.make_async_copy(ys_hbm.at[pl.ds(pl.multiple_of(src * st, st), st), :],
                                     ybuf.at[s, k, pl.ds(pl.multiple_of(r * st, st), st), :], sem.at[s])

    def start_gather(d_ref, s):
        def body(j, c):
            base = j * DMA_GROUP
            srcs = [[d_ref[k, base + u] for k in range(TOP_K)] for u in range(DMA_GROUP)]
            for u in range(DMA_GROUP):
                for k in range(TOP_K):
                    row_copy(srcs[u][k], k, base + u, s).start()
            return c
        lax.fori_loop(0, tm // DMA_GROUP, body, 0)

    def wait_gather(s):
        def body(r, c):
            for k in range(TOP_K):
                row_copy(0, k, r, s).wait()
            return c
        lax.fori_loop(0, tm, body, 0, unroll=2)

    @pl.when(i == 0)
    def _():
        start_gather(dcur_ref, 0)

    @pl.when(i + 1 < pl.num_programs(0))
    def _():
        start_gather(dnxt_ref, 1 - slot)

    lo, hi = _unpack_pairs(_load_token_tiles(f_ref, tm))
    lo = lo.astype(BF16)
    hi = hi.astype(BF16)
    h1 = (jnp.dot(lo, s1_ref[0:hw, :], preferred_element_type=F32)
          + jnp.dot(hi, s1_ref[hw:, :], preferred_element_type=F32))
    h3 = (jnp.dot(lo, s3_ref[0:hw, :], preferred_element_type=F32)
          + jnp.dot(hi, s3_ref[hw:, :], preferred_element_type=F32))
    shared = jnp.dot((h1 * jax.nn.sigmoid(h1) * h3).astype(BF16), s2_ref[...], preferred_element_type=F32)

    wait_gather(slot)
    wt = wt_ref[...]
    r_lo = shared[:, :hw]
    r_hi = shared[:, hw:]
    for k in range(TOP_K):
        a, c = _unpack_pairs(_load_token_tiles(ybuf.at[slot, k], tm))
        w = wt[:, k:k + 1]
        r_lo = r_lo + a * w
        r_hi = r_hi + c * w
    moe = jnp.concatenate([r_lo, r_hi], axis=1)
    o_ref[...] = x1_ref[...] + g2_ref[0] * (_rms(moe) * pw_ref[...])


def _combine(ys, dest, top_wt, f_packed, x1, g2, post_w, s1, s3, s2, rows_per_mod):
    m, d = x1.shape
    st = d // 2 // LANES
    fdim = s1.shape[1]
    tm = _tile(rows_per_mod, 256)
    per = rows_per_mod // tm
    g = g2.shape[0]
    nblk = m // tm
    smem = functools.partial(pl.BlockSpec, memory_space=pltpu.SMEM)
    return pl.pallas_call(
        _combine_kernel,
        grid=(nblk,),
        in_specs=[smem((TOP_K, tm), lambda i: (0, i)),
                  smem((TOP_K, tm), lambda i: (0, jnp.minimum(i + 1, nblk - 1))),
                  pl.BlockSpec((tm, TOP_K), lambda i: (i, 0)),
                  pl.BlockSpec((tm * st, LANES), lambda i: (i, 0)),
                  pl.BlockSpec((tm, d), lambda i: (i, 0)),
                  pl.BlockSpec((1, 1, d), lambda i: (i // per, 0, 0)),
                  pl.BlockSpec((1, d), lambda i: (0, 0)),
                  pl.BlockSpec((d, fdim), lambda i: (0, 0)),
                  pl.BlockSpec((d, fdim), lambda i: (0, 0)),
                  pl.BlockSpec((fdim, d), lambda i: (0, 0)),
                  pl.BlockSpec(memory_space=pl.ANY)],
        out_specs=pl.BlockSpec((tm, d), lambda i: (i, 0)),
        out_shape=jax.ShapeDtypeStruct((m, d), F32),
        scratch_shapes=[pltpu.VMEM((2, TOP_K, tm * st, LANES), U32), pltpu.SemaphoreType.DMA((2,))],
        compiler_params=_cparams(("arbitrary",)),
    )(dest, dest, top_wt, f_packed, x1, g2.reshape(g, 1, d), post_w.reshape(1, d), s1, s3, s2, ys)


def _layer(x, c, ctx, c_ctx, ada_w, ada_b, mix_pre_norm, mix_post_norm, ffn_pre_norm, ffn_post_norm,
           w_in, ret_decay_fwd, ret_decay_bwd, attn_q_norm, attn_k_norm, w_ret_o, w_attn_o, w_out,
           router_w, router_b, exp_w1, exp_w3, exp_w2, shared_w1, shared_w3, shared_w2):
    b, t, d = x.shape
    lc = ctx.shape[1]
    n = b * t
    rq_w, rv_w = RET_HEADS * RET_QK_DIM, RET_HEADS * RET_V_DIM
    aq_w, akv_w = ATTN_HEADS * ATTN_HEAD_DIM, ATTN_KV_HEADS * ATTN_HEAD_DIM
    names = ("rq", "rk", "rv", "rg", "aq", "ak", "av", "gr", "ga")
    widths = (rq_w, rq_w, rv_w, rv_w, aq_w, akv_w, akv_w, d, d)
    offs, o = {}, 0
    for nm, wd in zip(names, widths):
        offs[nm] = o
        o += wd

    rows = -(-(b + 1) // 16) * 16
    cond = jnp.zeros((rows, d), F32).at[:b].set(c).at[b].set(c_ctx)
    mod = _adaln(cond, ada_w, ada_b)
    sh1, sc1, g1, sh2, sc2, g2 = [mod[:b, i * d:(i + 1) * d] for i in range(6)]
    ch1, cs1 = mod[b:b + 1, 0:d], mod[b:b + 1, d:2 * d]

    x2 = x.reshape(n, d)
    h_x = _norm_mod(x2, mix_pre_norm, sc1, sh1, t)
    h_c = _norm_mod(ctx.reshape(b * lc, d), mix_pre_norm, cs1, ch1, b * lc)

    w_in_b = w_in.astype(BF16)
    p = _matmul(h_x, w_in_b, BF16)
    cnames = ("rv", "rk", "ak", "av")
    coffs, o = {}, 0
    for nm in cnames:
        coffs[nm] = o
        o += widths[names.index(nm)]
    w_ctx = jnp.concatenate([w_in_b[:, offs[nm]:offs[nm] + widths[names.index(nm)]] for nm in cnames], axis=1)
    pc = _matmul(h_c, w_ctx, BF16)

    lg = jnp.stack([jax.nn.log_sigmoid(ret_decay_fwd.astype(F32)),
                    jax.nn.log_sigmoid(ret_decay_bwd.astype(F32))])
    rc, rs_lo, rs_hi = _rope_tables(t, RET_QK_DIM)
    retg = _retention(p, pc, lg, rc, rs_lo + rs_hi, b, t, lc, offs, coffs)
    att = _attention(p, pc, attn_q_norm, attn_k_norm, _rope_tables(t, ATTN_HEAD_DIM), b, t, lc, offs, coffs)

    z = _merge(retg, att, p, w_ret_o.astype(BF16), w_attn_o.astype(BF16), offs)
    x1, f_packed, logits_t = _post(z, w_out.astype(BF16), x2, g1, mix_post_norm, ffn_pre_norm, sc2, sh2,
                                   router_w.T, t)

    top_idx, top_w, pos, cnt = _route(logits_t, router_b)

    bm = MOE_BLOCK
    e = router_w.shape[1]
    nb = (n * TOP_K) // bm + e
    counts = cnt[:, 0]
    padded = (counts + bm - 1) // bm * bm
    pad_end = jnp.cumsum(padded)
    pad_off = pad_end - padded
    blk_start = jnp.arange(nb, dtype=I32) * bm
    blk_expert = jnp.minimum(jnp.sum((pad_end[None, :] <= blk_start[:, None]).astype(I32), axis=1), e - 1)
    n_used = (pad_end[-1] // bm).astype(I32).reshape(1)

    dest = _dest(top_idx, pos, pad_off)
    xs = _dispatch(counts, pad_off, n_used, dest, f_packed, nb * bm)
    ys = _experts(xs, blk_expert, n_used, exp_w1, exp_w3, exp_w2)
    out = _combine(ys, dest, top_w.T, f_packed, x1, g2, ffn_post_norm, shared_w1.astype(BF16),
                   shared_w3.astype(BF16), shared_w2.astype(BF16), t)
    return out.reshape(b, t, d)


def kernel(x, c, ctx, c_ctx, ada_w, ada_b, mix_pre_norm, mix_post_norm, ffn_pre_norm, ffn_post_norm,
           w_in, ret_decay_fwd, ret_decay_bwd, attn_q_norm, attn_k_norm, w_ret_o, w_attn_o, w_out,
           router_w, router_b, exp_w1, exp_w3, exp_w2, shared_w1, shared_w3, shared_w2):
    assert ada_w.shape[0] == 1, "single-layer operation"
    return _layer(x, c, ctx, c_ctx, ada_w[0], ada_b[0], mix_pre_norm[0], mix_post_norm[0], ffn_pre_norm[0],
                  ffn_post_norm[0], w_in[0], ret_decay_fwd[0], ret_decay_bwd[0], attn_q_norm[0],
                  attn_k_norm[0], w_ret_o[0], w_attn_o[0], w_out[0], router_w[0], router_b[0],
                  exp_w1[0], exp_w3[0], exp_w2[0], shared_w1[0], shared_w3[0], shared_w2[0])
```

```python
import functools

import jax
import jax.numpy as jnp
import numpy as np
from jax import lax
from jax.experimental import pallas as pl
from jax.experimental.pallas import tpu as pltpu

F32 = jnp.float32
BF16 = jnp.bfloat16
U32 = jnp.uint32
I32 = jnp.int32

GRID_W = 64
NORM_EPS = 1e-6
ROPE_THETA = 10000.0
RET_HEADS = 4
RET_QK_DIM = 256
RET_V_DIM = 512
ATTN_HEADS = 16
ATTN_KV_HEADS = 4
ATTN_HEAD_DIM = 128
N_EXPERTS = 256
TOP_K = 8
N_GROUPS = 8
TOPK_GROUPS = 4
EXPERT_DIM = 512
ROUTED_SCALE = 2.5
LOG2E = 1.4426950408889634

V7X_VMEM_BYTES = 64 * 1024 * 1024
VMEM_LIMIT = V7X_VMEM_BYTES - 12 * 1024 * 1024
LANES = 128
SUBLANES = 8

RET_CHUNK = 256
MOE_BLOCK = 256
DMA_GROUP = 2


def _cparams(sem):
    return pltpu.CompilerParams(dimension_semantics=sem, vmem_limit_bytes=VMEM_LIMIT)


def _tile(dim, pref):
    if dim <= pref:
        return dim
    for t in range(pref, 0, -LANES):
        if dim % t == 0:
            return t
    raise ValueError((dim, pref))


def _blk(off, width):
    assert off % width == 0, (off, width)
    return off // width


def _ada_kernel(s_ref, w_ref, b_ref, o_ref):
    s = s_ref[...]
    s = (s * jax.nn.sigmoid(s)).astype(BF16)
    o_ref[...] = jnp.dot(s, w_ref[...].astype(BF16), preferred_element_type=F32) + b_ref[...]


def _adaln(cond, w, b):
    r, d = cond.shape
    n = w.shape[1]
    tn = _tile(n, 1024)
    return pl.pallas_call(
        _ada_kernel,
        grid=(n // tn,),
        in_specs=[pl.BlockSpec((r, d), lambda j: (0, 0)),
                  pl.BlockSpec((d, tn), lambda j: (0, j)),
                  pl.BlockSpec((1, tn), lambda j: (0, j))],
        out_specs=pl.BlockSpec((r, tn), lambda j: (0, j)),
        out_shape=jax.ShapeDtypeStruct((r, n), F32),
        compiler_params=_cparams(("arbitrary",)),
    )(cond, w, b.reshape(1, n))


def _rms(x):
    return x * lax.rsqrt(jnp.mean(x * x, axis=-1, keepdims=True) + NORM_EPS)


def _norm_mod_kernel(x_ref, w_ref, sc_ref, sh_ref, o_ref):
    y = _rms(x_ref[...]) * w_ref[...]
    o_ref[...] = (y * (1.0 + sc_ref[0]) + sh_ref[0]).astype(o_ref.dtype)


def _norm_mod(x2, w, sc, sh, rows_per_mod):
    m, d = x2.shape
    tm = _tile(rows_per_mod, 512)
    per = rows_per_mod // tm
    g = sc.shape[0]
    return pl.pallas_call(
        _norm_mod_kernel,
        grid=(m // tm,),
        in_specs=[pl.BlockSpec((tm, d), lambda i: (i, 0)),
                  pl.BlockSpec((1, d), lambda i: (0, 0)),
                  pl.BlockSpec((1, 1, d), lambda i: (i // per, 0, 0)),
                  pl.BlockSpec((1, 1, d), lambda i: (i // per, 0, 0))],
        out_specs=pl.BlockSpec((tm, d), lambda i: (i, 0)),
        out_shape=jax.ShapeDtypeStruct((m, d), BF16),
        compiler_params=_cparams(("arbitrary",)),
    )(x2, w.reshape(1, d), sc.reshape(g, 1, d), sh.reshape(g, 1, d))


def _mm_kernel(a_ref, b_ref, o_ref):
    o_ref[...] = jnp.dot(a_ref[...], b_ref[...], preferred_element_type=F32).astype(o_ref.dtype)


def _matmul(a, b, out_dtype, tm_pref=1024, tn_pref=1024):
    m, k = a.shape
    n = b.shape[1]
    tm, tn = _tile(m, tm_pref), _tile(n, tn_pref)
    return pl.pallas_call(
        _mm_kernel,
        grid=(m // tm, n // tn),
        in_specs=[pl.BlockSpec((tm, k), lambda i, j: (i, 0)),
                  pl.BlockSpec((k, tn), lambda i, j: (0, j))],
        out_specs=pl.BlockSpec((tm, tn), lambda i, j: (i, j)),
        out_shape=jax.ShapeDtypeStruct((m, n), out_dtype),
        compiler_params=_cparams(("arbitrary", "arbitrary")),
    )(a, b)


def _rope_tables(t, dim):
    half = dim // 2
    pos = jnp.arange(t, dtype=I32)
    row = (pos // GRID_W).astype(F32)
    col = (pos % GRID_W).astype(F32)
    inv = ROPE_THETA ** (-jnp.arange(0, half, 2, dtype=F32) / half)
    ar = row[:, None] * inv[None, :]
    ac = col[:, None] * inv[None, :]
    cr, sr, cc, sc = jnp.cos(ar), jnp.sin(ar), jnp.cos(ac), jnp.sin(ac)
    z = jnp.zeros_like(sr)
    c = jnp.concatenate([cr, cr, cc, cc], axis=1)
    s_lo = jnp.concatenate([-sr, z, -sc, z], axis=1)
    s_hi = jnp.concatenate([z, sr, z, sc], axis=1)
    return c, s_lo, s_hi


def _rope128(x, c, s_lo, s_hi):
    q4 = ATTN_HEAD_DIM // 4
    return (x * c + pltpu.roll(x, LANES - q4, axis=1) * s_lo + pltpu.roll(x, q4, axis=1) * s_hi)


def _rope256(x, c, s):
    parts = [pltpu.roll(x[:, i * LANES:(i + 1) * LANES], LANES // 2, axis=1)
             for i in range(RET_QK_DIM // LANES)]
    return x * c + jnp.concatenate(parts, axis=1) * s


def _ret_kernel(lg_ref, q_ref, k_ref, v_ref, g_ref, kc_ref, vc_ref, c_ref, s_ref, o_ref,
                qs, ks, acc, st):
    h = pl.program_id(1)
    lgf = lg_ref[0, h]
    lgb = lg_ref[1, h]
    t = q_ref.shape[0]
    ch = RET_CHUNK if t % RET_CHUNK == 0 else t
    nch = t // ch
    lc = kc_ref.shape[0]
    k_scale = RET_QK_DIM ** -0.5
    tn_dims = (((0,), (0,)), ((), ()))

    ii = lax.broadcasted_iota(I32, (ch, ch), 0)
    jj = lax.broadcasted_iota(I32, (ch, ch), 1)
    dif = (ii - jj).astype(F32)
    dmat = (jnp.where(dif >= 0, jnp.exp(lgf * jnp.maximum(dif, 0.0)), 0.0)
            + jnp.where(dif <= 0, jnp.exp(lgb * jnp.maximum(-dif, 0.0)), 0.0))
    pos = lax.broadcasted_iota(I32, (ch, 1), 0).astype(F32)
    qf_dec = jnp.exp(lgf * (pos + 1.0))
    kf_dec = jnp.exp(lgf * (ch - 1.0 - pos))
    qb_dec = jnp.exp(lgb * (ch - pos))
    kb_dec = jnp.exp(lgb * pos)
    chv = jnp.full((1, 1), float(ch), F32)
    cf_dec = jnp.exp(lgf * chv)
    cb_dec = jnp.exp(lgb * chv)

    cpos = lax.broadcasted_iota(I32, (lc, 1), 0).astype(F32)
    kc = kc_ref[...].astype(F32) * k_scale
    vc = vc_ref[...]
    st[0] = lax.dot_general((kc * jnp.exp(lgf * (lc - 1.0 - cpos))).astype(BF16), vc, tn_dims,
                            preferred_element_type=F32)
    st[1] = lax.dot_general((kc * jnp.exp(lgb * cpos)).astype(BF16), vc, tn_dims,
                            preferred_element_type=F32)

    def fwd(c, carry):
        r0 = pl.multiple_of(c * ch, ch)
        cs = c_ref[pl.ds(r0, ch), :]
        sn = s_ref[pl.ds(r0, ch), :]
        q = _rope256(q_ref[pl.ds(r0, ch), :].astype(F32), cs, sn)
        k = _rope256(k_ref[pl.ds(r0, ch), :].astype(F32), cs, sn) * k_scale
        v = v_ref[pl.ds(r0, ch), :]
        qb = q.astype(BF16)
        kb = k.astype(BF16)
        qs[pl.ds(r0, ch), :] = q
        ks[pl.ds(r0, ch), :] = k
        sc = lax.dot_general(qb, kb, (((1,), (1,)), ((), ())), preferred_element_type=F32) * dmat
        o = jnp.dot(sc.astype(BF16), v, preferred_element_type=F32)
        o += jnp.dot((q * qf_dec).astype(BF16), st[0].astype(BF16), preferred_element_type=F32)
        acc[pl.ds(r0, ch), :] = o
        st[0] = st[0] * cf_dec + lax.dot_general((k * kf_dec).astype(BF16), v, tn_dims,
                                                 preferred_element_type=F32)
        return carry

    lax.fori_loop(0, nch, fwd, 0)

    def bwd(i, carry):
        c = nch - 1 - i
        r0 = pl.multiple_of(c * ch, ch)
        q = qs[pl.ds(r0, ch), :]
        k = ks[pl.ds(r0, ch), :]
        v = v_ref[pl.ds(r0, ch), :]
        o = jnp.dot((q * qb_dec).astype(BF16), st[1].astype(BF16), preferred_element_type=F32)
        tot = acc[pl.ds(r0, ch), :] + o
        gate = g_ref[pl.ds(r0, ch), :].astype(F32)
        o_ref[pl.ds(r0, ch), :] = (gate * jax.nn.sigmoid(gate) * _rms(tot)).astype(o_ref.dtype)
        st[1] = st[1] * cb_dec + lax.dot_general((k * kb_dec).astype(BF16), v, tn_dims,
                                                 preferred_element_type=F32)
        return carry

    lax.fori_loop(0, nch, bwd, 0)


def _retention(p, pc, lg, tab_c, tab_s, b, t, lc, offs, coffs):
    h, dk, dv = RET_HEADS, RET_QK_DIM, RET_V_DIM
    qo, ko, vo, go = (_blk(offs["rq"], dk), _blk(offs["rk"], dk), _blk(offs["rv"], dv), _blk(offs["rg"], dv))
    kco, vco = _blk(coffs["rk"], dk), _blk(coffs["rv"], dv)
    grid_spec = pltpu.PrefetchScalarGridSpec(
        num_scalar_prefetch=1,
        grid=(b, h),
        in_specs=[pl.BlockSpec((t, dk), lambda bi, hi, lg: (bi, qo + hi)),
                  pl.BlockSpec((t, dk), lambda bi, hi, lg: (bi, ko + hi)),
                  pl.BlockSpec((t, dv), lambda bi, hi, lg: (bi, vo + hi)),
                  pl.BlockSpec((t, dv), lambda bi, hi, lg: (bi, go + hi)),
                  pl.BlockSpec((lc, dk), lambda bi, hi, lg: (bi, kco + hi)),
                  pl.BlockSpec((lc, dv), lambda bi, hi, lg: (bi, vco + hi)),
                  pl.BlockSpec((t, dk), lambda bi, hi, lg: (0, 0)),
                  pl.BlockSpec((t, dk), lambda bi, hi, lg: (0, 0))],
        out_specs=pl.BlockSpec((t, dv), lambda bi, hi, lg: (bi, hi)),
        scratch_shapes=[pltpu.VMEM((t, dk), F32), pltpu.VMEM((t, dk), F32),
                        pltpu.VMEM((t, dv), F32), pltpu.VMEM((2, dk, dv), F32)],
    )
    return pl.pallas_call(
        _ret_kernel,
        grid_spec=grid_spec,
        out_shape=jax.ShapeDtypeStruct((b * t, h * dv), BF16),
        compiler_params=_cparams(("arbitrary", "arbitrary")),
    )(lg, p, p, p, p, pc, pc, tab_c, tab_s)


def _attn_kernel(q_ref, kx_ref, vx_ref, kc_ref, vc_ref, qn_ref, kn_ref, c_ref, sl_ref, sh_ref, o_ref,
                 k_s, v_s):
    qt = pl.program_id(2)
    tq = q_ref.shape[0]
    lc = kc_ref.shape[0]
    hd = ATTN_HEAD_DIM
    scale = hd ** -0.5 * LOG2E

    @pl.when(qt == 0)
    def _():
        kn = kn_ref[...]
        k_s[0:lc, :] = (_rms(kc_ref[...].astype(F32)) * kn).astype(BF16)
        kx = _rms(kx_ref[...].astype(F32)) * kn
        k_s[lc:, :] = _rope128(kx, c_ref[...], sl_ref[...], sh_ref[...]).astype(BF16)
        v_s[0:lc, 0:hd] = vc_ref[...]
        v_s[lc:, 0:hd] = vx_ref[...]
        col = lax.broadcasted_iota(I32, (v_s.shape[0], hd), 1)
        v_s[:, hd:] = jnp.where(col == 0, 1.0, 0.0).astype(BF16)

    r0 = pl.multiple_of(qt * tq, tq)
    c = c_ref[pl.ds(r0, tq), :]
    sl = sl_ref[pl.ds(r0, tq), :]
    sh = sh_ref[pl.ds(r0, tq), :]
    qn = qn_ref[...]
    kk = k_s[...]
    vv = v_s[...]
    for g in range(ATTN_HEADS // ATTN_KV_HEADS):
        q = _rms(q_ref[:, g * hd:(g + 1) * hd].astype(F32)) * qn
        q = (_rope128(q, c, sl, sh) * scale).astype(BF16)
        s = lax.dot_general(q, kk, (((1,), (1,)), ((), ())), preferred_element_type=F32)
        m = jnp.max(s, axis=-1, keepdims=True)
        p = jnp.exp2(s - m).astype(BF16)
        ov = jnp.dot(p, vv, preferred_element_type=F32)
        o = ov[:, 0:hd] / ov[:, hd:hd + 1]
        o_ref[:, g * hd:(g + 1) * hd] = o.astype(o_ref.dtype)


def _attention(p, pc, qn, kn, tabs, b, t, lc, offs, coffs):
    hd, kvh = ATTN_HEAD_DIM, ATTN_KV_HEADS
    gw = (ATTN_HEADS // kvh) * hd
    tq = _tile(t, 256)
    qo, ko, vo = _blk(offs["aq"], gw), _blk(offs["ak"], hd), _blk(offs["av"], hd)
    kco, vco = _blk(coffs["ak"], hd), _blk(coffs["av"], hd)
    tab = pl.BlockSpec((t, hd), lambda bi, ki, qi: (0, 0))
    return pl.pallas_call(
        _attn_kernel,
        grid=(b, kvh, t // tq),
        in_specs=[pl.BlockSpec((tq, gw), lambda bi, ki, qi: (bi * (t // tq) + qi, qo + ki)),
                  pl.BlockSpec((t, hd), lambda bi, ki, qi: (bi, ko + ki)),
                  pl.BlockSpec((t, hd), lambda bi, ki, qi: (bi, vo + ki)),
                  pl.BlockSpec((lc, hd), lambda bi, ki, qi: (bi, kco + ki)),
                  pl.BlockSpec((lc, hd), lambda bi, ki, qi: (bi, vco + ki)),
                  pl.BlockSpec((1, hd), lambda bi, ki, qi: (0, 0)),
                  pl.BlockSpec((1, hd), lambda bi, ki, qi: (0, 0)),
                  tab, tab, tab],
        out_specs=pl.BlockSpec((tq, gw), lambda bi, ki, qi: (bi * (t // tq) + qi, ki)),
        out_shape=jax.ShapeDtypeStruct((b * t, ATTN_HEADS * hd), BF16),
        scratch_shapes=[pltpu.VMEM((lc + t, hd), BF16), pltpu.VMEM((lc + t, 2 * hd), BF16)],
        compiler_params=_cparams(("arbitrary", "arbitrary", "arbitrary")),
    )(p, p, p, pc, pc, qn.reshape(1, hd), kn.reshape(1, hd), *tabs)


def _merge_kernel(r_ref, a_ref, gr_ref, ga_ref, wr_ref, wa_ref, o_ref):
    ret = jnp.dot(r_ref[...], wr_ref[...], preferred_element_type=F32)
    att = jnp.dot(a_ref[...], wa_ref[...], preferred_element_type=F32)
    z = (jax.nn.sigmoid(gr_ref[...].astype(F32)) * ret + jax.nn.sigmoid(ga_ref[...].astype(F32)) * att)
    o_ref[...] = z.astype(o_ref.dtype)


def _merge(retg, att, p, w_ret_o, w_attn_o, offs):
    m, kr = retg.shape
    ka = att.shape[1]
    d = w_ret_o.shape[1]
    tm, tn = _tile(m, 1024), _tile(d, 256 if d < 2048 else 512)
    gro, gao = _blk(offs["gr"], tn), _blk(offs["ga"], tn)
    return pl.pallas_call(
        _merge_kernel,
        grid=(m // tm, d // tn),
        in_specs=[pl.BlockSpec((tm, kr), lambda i, j: (i, 0)),
                  pl.BlockSpec((tm, ka), lambda i, j: (i, 0)),
                  pl.BlockSpec((tm, tn), lambda i, j: (i, gro + j)),
                  pl.BlockSpec((tm, tn), lambda i, j: (i, gao + j)),
                  pl.BlockSpec((kr, tn), lambda i, j: (0, j)),
                  pl.BlockSpec((ka, tn), lambda i, j: (0, j))],
        out_specs=pl.BlockSpec((tm, tn), lambda i, j: (i, j)),
        out_shape=jax.ShapeDtypeStruct((m, d), BF16),
        compiler_params=_cparams(("arbitrary", "arbitrary")),
    )(retg, att, p, p, w_ret_o, w_attn_o)


def _pack_pairs(x):
    hw = x.shape[1] // 2
    lo = pltpu.bitcast(x[:, :hw].astype(BF16).astype(F32), U32)
    hi = pltpu.bitcast(x[:, hw:].astype(BF16).astype(F32), U32)
    return (hi & jnp.uint32(0xFFFF0000)) | (lo >> 16)


def _store_token_tiles(ref, packed):
    m = packed.shape[0]
    st = packed.shape[1] // LANES
    for s in range(st):
        ref[pl.ds(s, m, stride=st), :] = packed[:, s * LANES:(s + 1) * LANES]


def _load_token_tiles(ref, m):
    st = ref.shape[0] // m
    return jnp.concatenate([ref[pl.ds(s, m, stride=st), :] for s in range(st)], axis=1)


def _unpack_pairs(w):
    lo = pltpu.bitcast(w << 16, F32)
    hi = pltpu.bitcast(w & jnp.uint32(0xFFFF0000), F32)
    return lo, hi


def _post_kernel(z_ref, wo_ref, x_ref, g1_ref, pw_ref, fw_ref, sc_ref, sh_ref, rwh_ref, rwl_ref,
                 x1_ref, f_ref, lg_ref):
    y = jnp.dot(z_ref[...], wo_ref[...], preferred_element_type=F32)
    x1 = x_ref[...] + g1_ref[0] * (_rms(y) * pw_ref[...])
    x1_ref[...] = x1
    f = (_rms(x1) * fw_ref[...]) * (1.0 + sc_ref[0]) + sh_ref[0]
    _store_token_tiles(f_ref, _pack_pairs(f))
    fh = f.astype(BF16)
    fl = (f - fh.astype(F32)).astype(BF16)
    nt = (((1,), (1,)), ((), ()))
    lg_ref[...] = (lax.dot_general(rwh_ref[...], fh, nt, preferred_element_type=F32)
                   + lax.dot_general(rwh_ref[...], fl, nt, preferred_element_type=F32)
                   + lax.dot_general(rwl_ref[...], fh, nt, preferred_element_type=F32))


def _post(z, w_out, x2, g1, post_w, pre_w, sc2, sh2, router_wt, rows_per_mod):
    m, d = x2.shape
    e = router_wt.shape[0]
    st = d // 2 // LANES
    assert st == SUBLANES, "token rows must be whole (8, 128) tiles"
    rw_hi = router_wt.astype(BF16)
    rw_lo = (router_wt - rw_hi.astype(F32)).astype(BF16)
    tm = _tile(rows_per_mod, 256)
    per = rows_per_mod // tm
    g = g1.shape[0]
    vec = pl.BlockSpec((1, d), lambda i: (0, 0))
    mod = pl.BlockSpec((1, 1, d), lambda i: (i // per, 0, 0))
    return pl.pallas_call(
        _post_kernel,
        grid=(m // tm,),
        in_specs=[pl.BlockSpec((tm, d), lambda i: (i, 0)),
                  pl.BlockSpec((d, d), lambda i: (0, 0)),
                  pl.BlockSpec((tm, d), lambda i: (i, 0)),
                  mod, vec, vec, mod, mod,
                  pl.BlockSpec((e, d), lambda i: (0, 0)),
                  pl.BlockSpec((e, d), lambda i: (0, 0))],
        out_specs=[pl.BlockSpec((tm, d), lambda i: (i, 0)),
                   pl.BlockSpec((tm * st, LANES), lambda i: (i, 0)),
                   pl.BlockSpec((e, tm), lambda i: (0, i))],
        out_shape=[jax.ShapeDtypeStruct((m, d), F32),
                   jax.ShapeDtypeStruct((m * st, LANES), U32),
                   jax.ShapeDtypeStruct((e, m), F32)],
        compiler_params=_cparams(("arbitrary",)),
    )(z, w_out, x2, g1.reshape(g, 1, d), post_w.reshape(1, d), pre_w.reshape(1, d),
      sc2.reshape(g, 1, d), sh2.reshape(g, 1, d), rw_hi, rw_lo)


def _route_kernel(lg_ref, b_ref, tri_ref, idx_ref, w_ref, pos_ref, cnt_ref, carry):
    @pl.when(pl.program_id(0) == 0)
    def _():
        carry[...] = jnp.zeros_like(carry)

    e, tn = lg_ref.shape
    gs = e // N_GROUPS
    neg = -jnp.inf
    big = float(e)
    s = jax.nn.sigmoid(lg_ref[...])
    choice = s + b_ref[...]

    row_g = lax.broadcasted_iota(I32, (gs, tn), 0).astype(F32)
    rows = []
    for g in range(N_GROUPS):
        blk = choice[g * gs:(g + 1) * gs]
        m1 = jnp.max(blk, axis=0, keepdims=True)
        i1 = jnp.min(jnp.where(blk == m1, row_g, big), axis=0, keepdims=True)
        m2 = jnp.max(jnp.where(row_g == i1, neg, blk), axis=0, keepdims=True)
        rows.append(m1 + m2)
    gscore = jnp.concatenate(rows, axis=0)

    row_n = lax.broadcasted_iota(I32, (N_GROUPS, tn), 0).astype(F32)
    sel = jnp.zeros((N_GROUPS, tn), F32)
    cur = gscore
    for _ in range(TOPK_GROUPS):
        m = jnp.max(cur, axis=0, keepdims=True)
        gi = jnp.min(jnp.where(cur == m, row_n, big), axis=0, keepdims=True)
        hit = row_n == gi
        sel = jnp.where(hit, 1.0, sel)
        cur = jnp.where(hit, neg, cur)

    masked = jnp.concatenate(
        [jnp.where(sel[g:g + 1] > 0.0, choice[g * gs:(g + 1) * gs], neg) for g in range(N_GROUPS)], axis=0)

    row_e = lax.broadcasted_iota(I32, (e, tn), 0).astype(F32)
    onehot = jnp.zeros((e, tn), F32)
    cur = masked
    idxs, ws = [], []
    for _ in range(TOP_K):
        m = jnp.max(cur, axis=0, keepdims=True)
        ik = jnp.min(jnp.where(cur == m, row_e, big), axis=0, keepdims=True)
        hit = row_e == ik
        ws.append(jnp.sum(jnp.where(hit, s, 0.0), axis=0, keepdims=True))
        cur = jnp.where(hit, neg, cur)
        onehot = jnp.where(hit, 1.0, onehot)
        idxs.append(ik)
    w = jnp.concatenate(ws, axis=0)
    w = w / jnp.sum(w, axis=0, keepdims=True) * ROUTED_SCALE

    prefix = jnp.dot(onehot.astype(BF16), tri_ref[...], preferred_element_type=F32)
    base = prefix + carry[...]
    pos = jnp.concatenate(
        [jnp.sum(jnp.where(row_e == ik, base, 0.0), axis=0, keepdims=True) for ik in idxs], axis=0)
    carry[...] = carry[...] + jnp.sum(onehot, axis=1, keepdims=True)

    idx_ref[...] = jnp.concatenate(idxs, axis=0).astype(I32)
    w_ref[...] = w
    pos_ref[...] = pos.astype(I32)
    cnt_ref[...] = jnp.broadcast_to(carry[...], cnt_ref.shape).astype(I32)


def _route(logits_t, bias):
    e, n = logits_t.shape
    tn = _tile(n, 512)
    tri = (jnp.arange(tn)[:, None] < jnp.arange(tn)[None, :]).astype(BF16)
    kspec = pl.BlockSpec((TOP_K, tn), lambda i: (0, i))
    return pl.pallas_call(
        _route_kernel,
        grid=(n // tn,),
        in_specs=[pl.BlockSpec((e, tn), lambda i: (0, i)),
                  pl.BlockSpec((e, 1), lambda i: (0, 0)),
                  pl.BlockSpec((tn, tn), lambda i: (0, 0))],
        out_specs=[kspec, kspec, kspec, pl.BlockSpec((e, LANES), lambda i: (0, 0))],
        out_shape=[jax.ShapeDtypeStruct((TOP_K, n), I32), jax.ShapeDtypeStruct((TOP_K, n), F32),
                   jax.ShapeDtypeStruct((TOP_K, n), I32), jax.ShapeDtypeStruct((e, LANES), I32)],
        scratch_shapes=[pltpu.VMEM((e, 1), F32)],
        compiler_params=_cparams(("arbitrary",)),
    )(logits_t, bias.reshape(e, 1), tri)


def _dest_kernel(idx_ref, pos_ref, off_ref, o_ref):
    e = off_ref.shape[0]
    tn = idx_ref.shape[1]
    row_e = lax.broadcasted_iota(I32, (e, tn), 0)
    off = off_ref[...]
    idx = idx_ref[...]
    rows = [jnp.sum(jnp.where(row_e == idx[k:k + 1], off, 0.0), axis=0, keepdims=True) for k in range(TOP_K)]
    o_ref[...] = jnp.concatenate(rows, axis=0).astype(I32) + pos_ref[...]


def _dest(top_idx, pos, pad_off):
    k, n = top_idx.shape
    e = pad_off.shape[0]
    tn = _tile(n, 512)
    spec = pl.BlockSpec((k, tn), lambda i: (0, i))
    return pl.pallas_call(
        _dest_kernel,
        grid=(n // tn,),
        in_specs=[spec, spec, pl.BlockSpec((e, 1), lambda i: (0, 0))],
        out_specs=spec,
        out_shape=jax.ShapeDtypeStruct((k, n), I32),
        compiler_params=_cparams(("arbitrary",)),
    )(top_idx, pos, pad_off.astype(F32).reshape(e, 1))


def _dispatch_kernel(cnt_ref, off_ref, nu_ref, dest_ref, f_ref, xs_hbm, zbuf, sem, zsem):
    i = pl.program_id(0)
    tt = dest_ref.shape[1]
    st = SUBLANES
    bm = zbuf.shape[0] // st
    n_exp = cnt_ref.shape[0]
    nb = xs_hbm.shape[0] // (bm * st)

    def zero_fill(wait):
        def fill(start, rows):
            cp = pltpu.make_async_copy(zbuf.at[pl.ds(0, rows * st), :],
                                       xs_hbm.at[pl.ds(pl.multiple_of(start * st, st), rows * st), :], zsem)
            if wait:
                cp.wait()
            else:
                cp.start()

        def per_expert(e, c):
            cnt = cnt_ref[e]
            npad = (-cnt) & (bm - 1)
            start = off_ref[e] + cnt
            p = bm // 2
            while p >= 1:
                @pl.when((npad & p) != 0)
                def _(start=start, p=p):
                    fill(start, p)
                start = start + (npad & p)
                p //= 2
            return c
        lax.fori_loop(0, n_exp, per_expert, 0)

        def per_block(b, c):
            fill(b * bm, bm)
            return c
        lax.fori_loop(nu_ref[0], nb, per_block, 0)

    def row_copy(j, dst):
        return pltpu.make_async_copy(f_ref.at[pl.ds(pl.multiple_of(j * st, st), st), :],
                                     xs_hbm.at[pl.ds(pl.multiple_of(dst * st, st), st), :], sem)

    @pl.when(i == 0)
    def _():
        zbuf[...] = jnp.zeros_like(zbuf)
        zero_fill(False)

    def issue(j, c):
        base = j * DMA_GROUP
        dsts = [[dest_ref[k, base + u] for k in range(TOP_K)] for u in range(DMA_GROUP)]
        for u in range(DMA_GROUP):
            for k in range(TOP_K):
                row_copy(base + u, dsts[u][k]).start(priority=k % 2)
        return c
    lax.fori_loop(0, tt // DMA_GROUP, issue, 0)

    def drain(j, c):
        for k in range(TOP_K):
            row_copy(0, 0).wait()
        return c
    lax.fori_loop(0, tt, drain, 0, unroll=2)

    @pl.when(i == pl.num_programs(0) - 1)
    def _():
        zero_fill(True)


def _dispatch(counts, pad_off, n_used, dest, f_packed, n_slots):
    st = SUBLANES
    n = f_packed.shape[0] // st
    tt = _tile(n, 512)
    grid_spec = pltpu.PrefetchScalarGridSpec(
        num_scalar_prefetch=3,
        grid=(n // tt,),
        in_specs=[pl.BlockSpec((TOP_K, tt), lambda i, c, o, u: (0, i), memory_space=pltpu.SMEM),
                  pl.BlockSpec((tt * st, LANES), lambda i, c, o, u: (i, 0))],
        out_specs=pl.BlockSpec(memory_space=pl.ANY),
        scratch_shapes=[pltpu.VMEM((MOE_BLOCK * st, LANES), U32), pltpu.SemaphoreType.DMA(()),
                        pltpu.SemaphoreType.DMA(())],
    )
    return pl.pallas_call(
        _dispatch_kernel,
        grid_spec=grid_spec,
        out_shape=jax.ShapeDtypeStruct((n_slots * st, LANES), U32),
        compiler_params=_cparams(("arbitrary",)),
    )(counts, pad_off, n_used, dest, f_packed)


def _expert_kernel(ob_ref, act_ref, na_ref, nu_ref, x_ref, w1_hbm, w3_hbm, w2_hbm, y_ref,
                   w1f, w3f, w2f, w1b, w3b, w2b, wsem):
    b = pl.program_id(0)
    n_used = nu_ref[0]
    n_act = na_ref[0]
    bm = MOE_BLOCK
    hw = w1b.shape[0] // 2

    def fetch(i, wait):
        e = act_ref[i]
        slot = i % 2
        for src, dst in ((w1_hbm, w1f), (w3_hbm, w3f), (w2_hbm, w2f)):
            cp = pltpu.make_async_copy(src.at[e], dst.at[slot], wsem.at[slot])
            if wait:
                cp.wait()
            else:
                cp.start()

    @pl.when(b < n_used)
    def _():
        i = ob_ref[b]
        first = jnp.logical_or(b == 0, i != ob_ref[jnp.maximum(b - 1, 0)])

        @pl.when(b == 0)
        def _():
            fetch(0, False)

            @pl.when(n_act > 1)
            def _():
                fetch(1, False)

        @pl.when(first)
        def _():
            fetch(i, True)
            slot = i % 2
            w1b[...] = w1f[slot].astype(BF16)
            w3b[...] = w3f[slot].astype(BF16)
            w2b[...] = w2f[slot].astype(BF16)

            @pl.when(i + 2 < n_act)
            def _():
                fetch(i + 2, False)

        lo, hi = _unpack_pairs(_load_token_tiles(x_ref, bm))
        lo = lo.astype(BF16)
        hi = hi.astype(BF16)
        h1 = (jnp.dot(lo, w1b[0:hw, :], preferred_element_type=F32)
              + jnp.dot(hi, w1b[hw:, :], preferred_element_type=F32))
        h3 = (jnp.dot(lo, w3b[0:hw, :], preferred_element_type=F32)
              + jnp.dot(hi, w3b[hw:, :], preferred_element_type=F32))
        hid = (h1 * jax.nn.sigmoid(h1) * h3).astype(BF16)
        _store_token_tiles(y_ref, _pack_pairs(jnp.dot(hid, w2b[...], preferred_element_type=F32)))

    @pl.when(b >= n_used)
    def _():
        y_ref[...] = jnp.zeros_like(y_ref)


def _experts(xs, blk_ord, active, n_active, n_used, w1, w3, w2):
    nb = blk_ord.shape[0]
    e, d, fdim = w1.shape
    rows = MOE_BLOCK * (d // 2 // LANES)
    grid_spec = pltpu.PrefetchScalarGridSpec(
        num_scalar_prefetch=4,
        grid=(nb,),
        in_specs=[pl.BlockSpec((rows, LANES), lambda b, ob, ac, na, nu: (jnp.minimum(b, nu[0] - 1), 0)),
                  pl.BlockSpec(memory_space=pl.ANY),
                  pl.BlockSpec(memory_space=pl.ANY),
                  pl.BlockSpec(memory_space=pl.ANY)],
        out_specs=pl.BlockSpec((rows, LANES), lambda b, ob, ac, na, nu: (b, 0)),
        scratch_shapes=[pltpu.VMEM((2, d, fdim), F32), pltpu.VMEM((2, d, fdim), F32), pltpu.VMEM((2, fdim, d), F32),
                        pltpu.VMEM((d, fdim), BF16), pltpu.VMEM((d, fdim), BF16), pltpu.VMEM((fdim, d), BF16),
                        pltpu.SemaphoreType.DMA((2,))],
    )
    return pl.pallas_call(
        _expert_kernel,
        grid_spec=grid_spec,
        out_shape=jax.ShapeDtypeStruct((nb * rows, LANES), U32),
        compiler_params=_cparams(("arbitrary",)),
    )(blk_ord, active, n_active, n_used, xs, w1, w3, w2)


def _combine_kernel(dcur_ref, dnxt_ref, wt_ref, f_ref, x1_ref, g2_ref, pw_ref, s1_ref, s3_ref, s2_ref, ys_hbm,
                    o_ref, ybuf, sem):
    i = pl.program_id(0)
    tm = x1_ref.shape[0]
    hw = x1_ref.shape[1] // 2
    st = f_ref.shape[0] // tm
    slot = i % 2

    def row_copy(src, k, r, s):
        return pltpu.make_async_copy(ys_hbm.at[pl.ds(pl.multiple_of(src * st, st), st), :],
                                     ybuf.at[s, k, pl.ds(pl.multiple_of(r * st, st), st), :], sem.at[s])

    def start_gather(d_ref, s):
        def body(j, c):
            base = j * DMA_GROUP
            srcs = [[d_ref[k, base + u] for k in range(TOP_K)] for u in range(DMA_GROUP)]
            for u in range(DMA_GROUP):
                for k in range(TOP_K):
                    row_copy(srcs[u][k], k, base + u, s).start(priority=k % 2)
            return c
        lax.fori_loop(0, tm // DMA_GROUP, body, 0)

    def wait_gather(s):
        def body(r, c):
            for k in range(TOP_K):
                row_copy(0, k, r, s).wait()
            return c
        lax.fori_loop(0, tm, body, 0, unroll=2)

    @pl.when(i == 0)
    def _():
        start_gather(dcur_ref, 0)

    @pl.when(i + 1 < pl.num_programs(0))
    def _():
        start_gather(dnxt_ref, 1 - slot)

    lo, hi = _unpack_pairs(_load_token_tiles(f_ref, tm))
    lo = lo.astype(BF16)
    hi = hi.astype(BF16)
    h1 = (jnp.dot(lo, s1_ref[0:hw, :], preferred_element_type=F32)
          + jnp.dot(hi, s1_ref[hw:, :], preferred_element_type=F32))
    h3 = (jnp.dot(lo, s3_ref[0:hw, :], preferred_element_type=F32)
          + jnp.dot(hi, s3_ref[hw:, :], preferred_element_type=F32))
    shared = jnp.dot((h1 * jax.nn.sigmoid(h1) * h3).astype(BF16), s2_ref[...], preferred_element_type=F32)

    wait_gather(slot)
    wt = wt_ref[...]
    r_lo = shared[:, :hw]
    r_hi = shared[:, hw:]
    for k in range(TOP_K):
        a, c = _unpack_pairs(_load_token_tiles(ybuf.at[slot, k], tm))
        w = wt[:, k:k + 1]
        r_lo = r_lo + a * w
        r_hi = r_hi + c * w
    moe = jnp.concatenate([r_lo, r_hi], axis=1)
    o_ref[...] = x1_ref[...] + g2_ref[0] * (_rms(moe) * pw_ref[...])


def _combine(ys, dest, top_wt, f_packed, x1, g2, post_w, s1, s3, s2, rows_per_mod):
    m, d = x1.shape
    st = d // 2 // LANES
    fdim = s1.shape[1]
    tm = _tile(rows_per_mod, 256)
    per = rows_per_mod // tm
    g = g2.shape[0]
    nblk = m // tm
    smem = functools.partial(pl.BlockSpec, memory_space=pltpu.SMEM)
    return pl.pallas_call(
        _combine_kernel,
        grid=(nblk,),
        in_specs=[smem((TOP_K, tm), lambda i: (0, i)),
                  smem((TOP_K, tm), lambda i: (0, jnp.minimum(i + 1, nblk - 1))),
                  pl.BlockSpec((tm, TOP_K), lambda i: (i, 0)),
                  pl.BlockSpec((tm * st, LANES), lambda i: (i, 0)),
                  pl.BlockSpec((tm, d), lambda i: (i, 0)),
                  pl.BlockSpec((1, 1, d), lambda i: (i // per, 0, 0)),
                  pl.BlockSpec((1, d), lambda i: (0, 0)),
                  pl.BlockSpec((d, fdim), lambda i: (0, 0)),
                  pl.BlockSpec((d, fdim), lambda i: (0, 0)),
                  pl.BlockSpec((fdim, d), lambda i: (0, 0)),
                  pl.BlockSpec(memory_space=pl.ANY)],
        out_specs=pl.BlockSpec((tm, d), lambda i: (i, 0)),
        out_shape=jax.ShapeDtypeStruct((m, d), F32),
        scratch_shapes=[pltpu.VMEM((2, TOP_K, tm * st, LANES), U32), pltpu.SemaphoreType.DMA((2,))],
        compiler_params=_cparams(("arbitrary",)),
    )(dest, dest, top_wt, f_packed, x1, g2.reshape(g, 1, d), post_w.reshape(1, d), s1, s3, s2, ys)


def _layer(x, c, ctx, c_ctx, ada_w, ada_b, mix_pre_norm, mix_post_norm, ffn_pre_norm, ffn_post_norm,
           w_in, ret_decay_fwd, ret_decay_bwd, attn_q_norm, attn_k_norm, w_ret_o, w_attn_o, w_out,
           router_w, router_b, exp_w1, exp_w3, exp_w2, shared_w1, shared_w3, shared_w2):
    b, t, d = x.shape
    lc = ctx.shape[1]
    n = b * t
    rq_w, rv_w = RET_HEADS * RET_QK_DIM, RET_HEADS * RET_V_DIM
    aq_w, akv_w = ATTN_HEADS * ATTN_HEAD_DIM, ATTN_KV_HEADS * ATTN_HEAD_DIM
    names = ("rq", "rk", "rv", "rg", "aq", "ak", "av", "gr", "ga")
    widths = (rq_w, rq_w, rv_w, rv_w, aq_w, akv_w, akv_w, d, d)
    offs, o = {}, 0
    for nm, wd in zip(names, widths):
        offs[nm] = o
        o += wd

    rows = -(-(b + 1) // 16) * 16
    cond = jnp.zeros((rows, d), F32).at[:b].set(c).at[b].set(c_ctx)
    mod = _adaln(cond, ada_w, ada_b)
    sh1, sc1, g1, sh2, sc2, g2 = [mod[:b, i * d:(i + 1) * d] for i in range(6)]
    ch1, cs1 = mod[b:b + 1, 0:d], mod[b:b + 1, d:2 * d]

    x2 = x.reshape(n, d)
    h_x = _norm_mod(x2, mix_pre_norm, sc1, sh1, t)
    h_c = _norm_mod(ctx.reshape(b * lc, d), mix_pre_norm, cs1, ch1, b * lc)

    w_in_b = w_in.astype(BF16)
    p = _matmul(h_x, w_in_b, BF16)
    cnames = ("rv", "rk", "ak", "av")
    coffs, o = {}, 0
    for nm in cnames:
        coffs[nm] = o
        o += widths[names.index(nm)]
    w_ctx = jnp.concatenate([w_in_b[:, offs[nm]:offs[nm] + widths[names.index(nm)]] for nm in cnames], axis=1)
    pc = _matmul(h_c, w_ctx, BF16)

    lg = jnp.stack([jax.nn.log_sigmoid(ret_decay_fwd.astype(F32)),
                    jax.nn.log_sigmoid(ret_decay_bwd.astype(F32))])
    rc, rs_lo, rs_hi = _rope_tables(t, RET_QK_DIM)
    retg = _retention(p, pc, lg, rc, rs_lo + rs_hi, b, t, lc, offs, coffs)
    att = _attention(p, pc, attn_q_norm, attn_k_norm, _rope_tables(t, ATTN_HEAD_DIM), b, t, lc, offs, coffs)

    z = _merge(retg, att, p, w_ret_o.astype(BF16), w_attn_o.astype(BF16), offs)
    x1, f_packed, logits_t = _post(z, w_out.astype(BF16), x2, g1, mix_post_norm, ffn_pre_norm, sc2, sh2,
                                   router_w.T, t)

    top_idx, top_w, pos, cnt = _route(logits_t, router_b)

    bm = MOE_BLOCK
    e = router_w.shape[1]
    nb = (n * TOP_K) // bm + e
    counts = cnt[:, 0]
    padded = (counts + bm - 1) // bm * bm
    pad_end = jnp.cumsum(padded)
    pad_off = pad_end - padded
    blk_start = jnp.arange(nb, dtype=I32) * bm
    blk_expert = jnp.minimum(jnp.sum((pad_end[None, :] <= blk_start[:, None]).astype(I32), axis=1), e - 1)
    n_used = (pad_end[-1] // bm).astype(I32).reshape(1)
    is_act = (counts > 0).astype(I32)
    exp_ord = jnp.cumsum(is_act) - is_act
    n_active = jnp.sum(is_act).astype(I32).reshape(1)
    eids = jnp.arange(e, dtype=I32)
    active = jnp.sum(jnp.where((exp_ord[None, :] == eids[:, None]) & (is_act[None, :] > 0), eids[None, :], 0),
                     axis=1).astype(I32)
    blk_ord = exp_ord[blk_expert].astype(I32)

    dest = _dest(top_idx, pos, pad_off)
    xs = _dispatch(counts, pad_off, n_used, dest, f_packed, nb * bm)
    ys = _experts(xs, blk_ord, active, n_active, n_used, exp_w1, exp_w3, exp_w2)
    out = _combine(ys, dest, top_w.T, f_packed, x1, g2, ffn_post_norm, shared_w1.astype(BF16),
                   shared_w3.astype(BF16), shared_w2.astype(BF16), t)
    return out.reshape(b, t, d)


def kernel(x, c, ctx, c_ctx, ada_w, ada_b, mix_pre_norm, mix_post_norm, ffn_pre_norm, ffn_post_norm,
           w_in, ret_decay_fwd, ret_decay_bwd, attn_q_norm, attn_k_norm, w_ret_o, w_attn_o, w_out,
           router_w, router_b, exp_w1, exp_w3, exp_w2, shared_w1, shared_w3, shared_w2):
    assert ada_w.shape[0] == 1, "single-layer operation"
    return _layer(x, c, ctx, c_ctx, ada_w[0], ada_b[0], mix_pre_norm[0], mix_post_norm[0], ffn_pre_norm[0],
                  ffn_post_norm[0], w_in[0], ret_decay_fwd[0], ret_decay_bwd[0], attn_q_norm[0],
                  attn_k_norm[0], w_ret_o[0], w_attn_o[0], w_out[0], router_w[0], router_b[0],
                  exp_w1[0], exp_w3[0], exp_w2[0], shared_w1[0], shared_w3[0], shared_w2[0])
```

```python
import functools

import jax
import jax.numpy as jnp
import numpy as np
from jax import lax
from jax.experimental import pallas as pl
from jax.experimental.pallas import tpu as pltpu

F32 = jnp.float32
BF16 = jnp.bfloat16
U32 = jnp.uint32
I32 = jnp.int32

GRID_W = 64
NORM_EPS = 1e-6
ROPE_THETA = 10000.0
RET_HEADS = 4
RET_QK_DIM = 256
RET_V_DIM = 512
ATTN_HEADS = 16
ATTN_KV_HEADS = 4
ATTN_HEAD_DIM = 128
N_EXPERTS = 256
TOP_K = 8
N_GROUPS = 8
TOPK_GROUPS = 4
EXPERT_DIM = 512
ROUTED_SCALE = 2.5
LOG2E = 1.4426950408889634

V7X_VMEM_BYTES = 64 * 1024 * 1024
VMEM_LIMIT = V7X_VMEM_BYTES - 12 * 1024 * 1024
LANES = 128
SUBLANES = 8

RET_CHUNK = 256
MOE_BLOCK = 256
DMA_GROUP = 2
DMA_LAG = 32
CAST_ROWS = 256
POST_SUB = 256


def _cparams(sem):
    return pltpu.CompilerParams(dimension_semantics=sem, vmem_limit_bytes=VMEM_LIMIT)


def _tile(dim, pref):
    if dim <= pref:
        return dim
    for t in range(pref, 0, -LANES):
        if dim % t == 0:
            return t
    raise ValueError((dim, pref))


def _blk(off, width):
    assert off % width == 0, (off, width)
    return off // width


def _ada_kernel(s_ref, w_ref, b_ref, o_ref):
    s = s_ref[...]
    s = (s * jax.nn.sigmoid(s)).astype(BF16)
    o_ref[...] = jnp.dot(s, w_ref[...].astype(BF16), preferred_element_type=F32) + b_ref[...]


def _adaln(cond, w, b):
    r, d = cond.shape
    n = w.shape[1]
    tn = _tile(n, 1024)
    return pl.pallas_call(
        _ada_kernel,
        grid=(n // tn,),
        in_specs=[pl.BlockSpec((r, d), lambda j: (0, 0)),
                  pl.BlockSpec((d, tn), lambda j: (0, j)),
                  pl.BlockSpec((1, tn), lambda j: (0, j))],
        out_specs=pl.BlockSpec((r, tn), lambda j: (0, j)),
        out_shape=jax.ShapeDtypeStruct((r, n), F32),
        compiler_params=_cparams(("arbitrary",)),
    )(cond, w, b.reshape(1, n))


def _rms(x):
    return x * lax.rsqrt(jnp.mean(x * x, axis=-1, keepdims=True) + NORM_EPS)


def _norm_mod_kernel(x_ref, w_ref, sc_ref, sh_ref, o_ref):
    y = _rms(x_ref[...]) * w_ref[...]
    o_ref[...] = (y * (1.0 + sc_ref[0]) + sh_ref[0]).astype(o_ref.dtype)


def _norm_mod(x2, w, sc, sh, rows_per_mod):
    m, d = x2.shape
    tm = _tile(rows_per_mod, 512)
    per = rows_per_mod // tm
    g = sc.shape[0]
    return pl.pallas_call(
        _norm_mod_kernel,
        grid=(m // tm,),
        in_specs=[pl.BlockSpec((tm, d), lambda i: (i, 0)),
                  pl.BlockSpec((1, d), lambda i: (0, 0)),
                  pl.BlockSpec((1, 1, d), lambda i: (i // per, 0, 0)),
                  pl.BlockSpec((1, 1, d), lambda i: (i // per, 0, 0))],
        out_specs=pl.BlockSpec((tm, d), lambda i: (i, 0)),
        out_shape=jax.ShapeDtypeStruct((m, d), BF16),
        compiler_params=_cparams(("arbitrary",)),
    )(x2, w.reshape(1, d), sc.reshape(g, 1, d), sh.reshape(g, 1, d))


def _mm_kernel(a_ref, b_ref, o_ref):
    o_ref[...] = jnp.dot(a_ref[...], b_ref[...], preferred_element_type=F32).astype(o_ref.dtype)


def _matmul(a, b, out_dtype, tm_pref=1024, tn_pref=1024):
    m, k = a.shape
    n = b.shape[1]
    tm, tn = _tile(m, tm_pref), _tile(n, tn_pref)
    return pl.pallas_call(
        _mm_kernel,
        grid=(m // tm, n // tn),
        in_specs=[pl.BlockSpec((tm, k), lambda i, j: (i, 0)),
                  pl.BlockSpec((k, tn), lambda i, j: (0, j))],
        out_specs=pl.BlockSpec((tm, tn), lambda i, j: (i, j)),
        out_shape=jax.ShapeDtypeStruct((m, n), out_dtype),
        compiler_params=_cparams(("arbitrary", "arbitrary")),
    )(a, b)


def _rope_tables(t, dim):
    half = dim // 2
    pos = jnp.arange(t, dtype=I32)
    row = (pos // GRID_W).astype(F32)
    col = (pos % GRID_W).astype(F32)
    inv = ROPE_THETA ** (-jnp.arange(0, half, 2, dtype=F32) / half)
    ar = row[:, None] * inv[None, :]
    ac = col[:, None] * inv[None, :]
    cr, sr, cc, sc = jnp.cos(ar), jnp.sin(ar), jnp.cos(ac), jnp.sin(ac)
    z = jnp.zeros_like(sr)
    c = jnp.concatenate([cr, cr, cc, cc], axis=1)
    s_lo = jnp.concatenate([-sr, z, -sc, z], axis=1)
    s_hi = jnp.concatenate([z, sr, z, sc], axis=1)
    return c, s_lo, s_hi


def _rope128(x, c, s_lo, s_hi):
    q4 = ATTN_HEAD_DIM // 4
    return (x * c + pltpu.roll(x, LANES - q4, axis=1) * s_lo + pltpu.roll(x, q4, axis=1) * s_hi)


def _rope256(x, c, s):
    parts = [pltpu.roll(x[:, i * LANES:(i + 1) * LANES], LANES // 2, axis=1)
             for i in range(RET_QK_DIM // LANES)]
    return x * c + jnp.concatenate(parts, axis=1) * s


def _ret_kernel(lg_ref, q_ref, k_ref, v_ref, g_ref, kc_ref, vc_ref, c_ref, s_ref, o_ref,
                qs, ks, acc, st):
    h = pl.program_id(1)
    lgf = lg_ref[0, h]
    lgb = lg_ref[1, h]
    t = q_ref.shape[0]
    ch = RET_CHUNK if t % RET_CHUNK == 0 else t
    nch = t // ch
    lc = kc_ref.shape[0]
    k_scale = RET_QK_DIM ** -0.5
    tn_dims = (((0,), (0,)), ((), ()))

    ii = lax.broadcasted_iota(I32, (ch, ch), 0)
    jj = lax.broadcasted_iota(I32, (ch, ch), 1)
    dif = (ii - jj).astype(F32)
    dmat = (jnp.where(dif >= 0, jnp.exp(lgf * jnp.maximum(dif, 0.0)), 0.0)
            + jnp.where(dif <= 0, jnp.exp(lgb * jnp.maximum(-dif, 0.0)), 0.0))
    pos = lax.broadcasted_iota(I32, (ch, 1), 0).astype(F32)
    qf_dec = jnp.exp(lgf * (pos + 1.0))
    kf_dec = jnp.exp(lgf * (ch - 1.0 - pos))
    qb_dec = jnp.exp(lgb * (ch - pos))
    kb_dec = jnp.exp(lgb * pos)
    chv = jnp.full((1, 1), float(ch), F32)
    cf_dec = jnp.exp(lgf * chv)
    cb_dec = jnp.exp(lgb * chv)

    cpos = lax.broadcasted_iota(I32, (lc, 1), 0).astype(F32)
    kc = kc_ref[...].astype(F32) * k_scale
    vc = vc_ref[...]
    st[0] = lax.dot_general((kc * jnp.exp(lgf * (lc - 1.0 - cpos))).astype(BF16), vc, tn_dims,
                            preferred_element_type=F32)
    st[1] = lax.dot_general((kc * jnp.exp(lgb * cpos)).astype(BF16), vc, tn_dims,
                            preferred_element_type=F32)

    def fwd(c, carry):
        r0 = pl.multiple_of(c * ch, ch)
        cs = c_ref[pl.ds(r0, ch), :]
        sn = s_ref[pl.ds(r0, ch), :]
        q = _rope256(q_ref[pl.ds(r0, ch), :].astype(F32), cs, sn)
        k = _rope256(k_ref[pl.ds(r0, ch), :].astype(F32), cs, sn) * k_scale
        v = v_ref[pl.ds(r0, ch), :]
        qb = q.astype(BF16)
        kb = k.astype(BF16)
        qs[pl.ds(r0, ch), :] = q
        ks[pl.ds(r0, ch), :] = k
        sc = lax.dot_general(qb, kb, (((1,), (1,)), ((), ())), preferred_element_type=F32) * dmat
        o = jnp.dot(sc.astype(BF16), v, preferred_element_type=F32)
        o += jnp.dot((q * qf_dec).astype(BF16), st[0].astype(BF16), preferred_element_type=F32)
        acc[pl.ds(r0, ch), :] = o
        st[0] = st[0] * cf_dec + lax.dot_general((k * kf_dec).astype(BF16), v, tn_dims,
                                                 preferred_element_type=F32)
        return carry

    lax.fori_loop(0, nch, fwd, 0, unroll=2)

    def bwd(i, carry):
        c = nch - 1 - i
        r0 = pl.multiple_of(c * ch, ch)
        q = qs[pl.ds(r0, ch), :]
        k = ks[pl.ds(r0, ch), :]
        v = v_ref[pl.ds(r0, ch), :]
        o = jnp.dot((q * qb_dec).astype(BF16), st[1].astype(BF16), preferred_element_type=F32)
        tot = acc[pl.ds(r0, ch), :] + o
        gate = g_ref[pl.ds(r0, ch), :].astype(F32)
        o_ref[pl.ds(r0, ch), :] = (gate * jax.nn.sigmoid(gate) * _rms(tot)).astype(o_ref.dtype)
        st[1] = st[1] * cb_dec + lax.dot_general((k * kb_dec).astype(BF16), v, tn_dims,
                                                 preferred_element_type=F32)
        return carry

    lax.fori_loop(0, nch, bwd, 0, unroll=2)


def _retention(p, pc, lg, tab_c, tab_s, b, t, lc, offs, coffs):
    h, dk, dv = RET_HEADS, RET_QK_DIM, RET_V_DIM
    qo, ko, vo, go = (_blk(offs["rq"], dk), _blk(offs["rk"], dk), _blk(offs["rv"], dv), _blk(offs["rg"], dv))
    kco, vco = _blk(coffs["rk"], dk), _blk(coffs["rv"], dv)
    grid_spec = pltpu.PrefetchScalarGridSpec(
        num_scalar_prefetch=1,
        grid=(b, h),
        in_specs=[pl.BlockSpec((t, dk), lambda bi, hi, lg: (bi, qo + hi)),
                  pl.BlockSpec((t, dk), lambda bi, hi, lg: (bi, ko + hi)),
                  pl.BlockSpec((t, dv), lambda bi, hi, lg: (bi, vo + hi)),
                  pl.BlockSpec((t, dv), lambda bi, hi, lg: (bi, go + hi)),
                  pl.BlockSpec((lc, dk), lambda bi, hi, lg: (bi, kco + hi)),
                  pl.BlockSpec((lc, dv), lambda bi, hi, lg: (bi, vco + hi)),
                  pl.BlockSpec((t, dk), lambda bi, hi, lg: (0, 0)),
                  pl.BlockSpec((t, dk), lambda bi, hi, lg: (0, 0))],
        out_specs=pl.BlockSpec((t, dv), lambda bi, hi, lg: (bi, hi)),
        scratch_shapes=[pltpu.VMEM((t, dk), F32), pltpu.VMEM((t, dk), F32),
                        pltpu.VMEM((t, dv), F32), pltpu.VMEM((2, dk, dv), F32)],
    )
    return pl.pallas_call(
        _ret_kernel,
        grid_spec=grid_spec,
        out_shape=jax.ShapeDtypeStruct((b * t, h * dv), BF16),
        compiler_params=_cparams(("arbitrary", "arbitrary")),
    )(lg, p, p, p, p, pc, pc, tab_c, tab_s)


def _attn_kernel(q_ref, kx_ref, vx_ref, kc_ref, vc_ref, qn_ref, kn_ref, c_ref, sl_ref, sh_ref, o_ref,
                 k_s, v_s):
    qt = pl.program_id(2)
    tq = q_ref.shape[0]
    lc = kc_ref.shape[0]
    hd = ATTN_HEAD_DIM
    scale = hd ** -0.5 * LOG2E

    @pl.when(qt == 0)
    def _():
        kn = kn_ref[...]
        k_s[0:lc, :] = (_rms(kc_ref[...].astype(F32)) * kn).astype(BF16)
        kx = _rms(kx_ref[...].astype(F32)) * kn
        k_s[lc:, :] = _rope128(kx, c_ref[...], sl_ref[...], sh_ref[...]).astype(BF16)
        v_s[0:lc, 0:hd] = vc_ref[...]
        v_s[lc:, 0:hd] = vx_ref[...]
        col = lax.broadcasted_iota(I32, (v_s.shape[0], hd), 1)
        v_s[:, hd:] = jnp.where(col == 0, 1.0, 0.0).astype(BF16)

    r0 = pl.multiple_of(qt * tq, tq)
    c = c_ref[pl.ds(r0, tq), :]
    sl = sl_ref[pl.ds(r0, tq), :]
    sh = sh_ref[pl.ds(r0, tq), :]
    qn = qn_ref[...]
    kk = k_s[...]
    vv = v_s[...]
    for g in range(ATTN_HEADS // ATTN_KV_HEADS):
        q = _rms(q_ref[:, g * hd:(g + 1) * hd].astype(F32)) * qn
        q = (_rope128(q, c, sl, sh) * scale).astype(BF16)
        s = lax.dot_general(q, kk, (((1,), (1,)), ((), ())), preferred_element_type=F32)
        m = jnp.max(s, axis=-1, keepdims=True)
        p = jnp.exp2(s - m).astype(BF16)
        ov = jnp.dot(p, vv, preferred_element_type=F32)
        o = ov[:, 0:hd] / ov[:, hd:hd + 1]
        o_ref[:, g * hd:(g + 1) * hd] = o.astype(o_ref.dtype)


def _attention(p, pc, qn, kn, tabs, b, t, lc, offs, coffs):
    hd, kvh = ATTN_HEAD_DIM, ATTN_KV_HEADS
    gw = (ATTN_HEADS // kvh) * hd
    tq = _tile(t, 256)
    qo, ko, vo = _blk(offs["aq"], gw), _blk(offs["ak"], hd), _blk(offs["av"], hd)
    kco, vco = _blk(coffs["ak"], hd), _blk(coffs["av"], hd)
    tab = pl.BlockSpec((t, hd), lambda bi, ki, qi: (0, 0))
    return pl.pallas_call(
        _attn_kernel,
        grid=(b, kvh, t // tq),
        in_specs=[pl.BlockSpec((tq, gw), lambda bi, ki, qi: (bi * (t // tq) + qi, qo + ki)),
                  pl.BlockSpec((t, hd), lambda bi, ki, qi: (bi, ko + ki)),
                  pl.BlockSpec((t, hd), lambda bi, ki, qi: (bi, vo + ki)),
                  pl.BlockSpec((lc, hd), lambda bi, ki, qi: (bi, kco + ki)),
                  pl.BlockSpec((lc, hd), lambda bi, ki, qi: (bi, vco + ki)),
                  pl.BlockSpec((1, hd), lambda bi, ki, qi: (0, 0)),
                  pl.BlockSpec((1, hd), lambda bi, ki, qi: (0, 0)),
                  tab, tab, tab],
        out_specs=pl.BlockSpec((tq, gw), lambda bi, ki, qi: (bi * (t // tq) + qi, ki)),
        out_shape=jax.ShapeDtypeStruct((b * t, ATTN_HEADS * hd), BF16),
        scratch_shapes=[pltpu.VMEM((lc + t, hd), BF16), pltpu.VMEM((lc + t, 2 * hd), BF16)],
        compiler_params=_cparams(("arbitrary", "arbitrary", "arbitrary")),
    )(p, p, p, pc, pc, qn.reshape(1, hd), kn.reshape(1, hd), *tabs)


def _merge_kernel(r_ref, a_ref, gr_ref, ga_ref, wr_ref, wa_ref, o_ref):
    ret = jnp.dot(r_ref[...], wr_ref[...], preferred_element_type=F32)
    att = jnp.dot(a_ref[...], wa_ref[...], preferred_element_type=F32)
    z = (jax.nn.sigmoid(gr_ref[...].astype(F32)) * ret + jax.nn.sigmoid(ga_ref[...].astype(F32)) * att)
    o_ref[...] = z.astype(o_ref.dtype)


def _merge(retg, att, p, w_ret_o, w_attn_o, offs):
    m, kr = retg.shape
    ka = att.shape[1]
    d = w_ret_o.shape[1]
    tm, tn = _tile(m, 1024), _tile(d, 256 if d < 2048 else 512)
    gro, gao = _blk(offs["gr"], tn), _blk(offs["ga"], tn)
    return pl.pallas_call(
        _merge_kernel,
        grid=(m // tm, d // tn),
        in_specs=[pl.BlockSpec((tm, kr), lambda i, j: (i, 0)),
                  pl.BlockSpec((tm, ka), lambda i, j: (i, 0)),
                  pl.BlockSpec((tm, tn), lambda i, j: (i, gro + j)),
                  pl.BlockSpec((tm, tn), lambda i, j: (i, gao + j)),
                  pl.BlockSpec((kr, tn), lambda i, j: (0, j)),
                  pl.BlockSpec((ka, tn), lambda i, j: (0, j))],
        out_specs=pl.BlockSpec((tm, tn), lambda i, j: (i, j)),
        out_shape=jax.ShapeDtypeStruct((m, d), BF16),
        compiler_params=_cparams(("arbitrary", "arbitrary")),
    )(retg, att, p, p, w_ret_o, w_attn_o)


def _pack_pairs(x):
    hw = x.shape[1] // 2
    lo = pltpu.bitcast(x[:, :hw].astype(BF16).astype(F32), U32)
    hi = pltpu.bitcast(x[:, hw:].astype(BF16).astype(F32), U32)
    return (hi & jnp.uint32(0xFFFF0000)) | (lo >> 16)


def _store_token_tiles(ref, packed):
    m = packed.shape[0]
    st = packed.shape[1] // LANES
    for s in range(st):
        ref[pl.ds(s, m, stride=st), :] = packed[:, s * LANES:(s + 1) * LANES]


def _load_token_tiles(ref, m):
    st = ref.shape[0] // m
    return jnp.concatenate([ref[pl.ds(s, m, stride=st), :] for s in range(st)], axis=1)


def _unpack_pairs(w):
    lo = pltpu.bitcast(w << 16, F32)
    hi = pltpu.bitcast(w & jnp.uint32(0xFFFF0000), F32)
    return lo, hi


def _post_kernel(z_ref, wo_ref, x_ref, g1_ref, pw_ref, fw_ref, sc_ref, sh_ref, rwh_ref, rwl_ref,
                 x1_ref, f_ref, lg_ref):
    tm = x_ref.shape[0]
    sub = min(tm, POST_SUB)
    st = f_ref.shape[0] // tm
    nt = (((1,), (1,)), ((), ()))
    for h in range(tm // sub):
        r0 = h * sub
        y = jnp.dot(z_ref[r0:r0 + sub, :], wo_ref[...], preferred_element_type=F32)
        x1 = x_ref[r0:r0 + sub, :] + g1_ref[0] * (_rms(y) * pw_ref[...])
        x1_ref[r0:r0 + sub, :] = x1
        f = (_rms(x1) * fw_ref[...]) * (1.0 + sc_ref[0]) + sh_ref[0]
        _store_token_tiles(f_ref.at[r0 * st:(r0 + sub) * st, :], _pack_pairs(f))
        fh = f.astype(BF16)
        fl = (f - fh.astype(F32)).astype(BF16)
        lg_ref[:, r0:r0 + sub] = (lax.dot_general(rwh_ref[...], fh, nt, preferred_element_type=F32)
                                  + lax.dot_general(rwh_ref[...], fl, nt, preferred_element_type=F32)
                                  + lax.dot_general(rwl_ref[...], fh, nt, preferred_element_type=F32))


def _post(z, w_out, x2, g1, post_w, pre_w, sc2, sh2, router_wt, rows_per_mod):
    m, d = x2.shape
    e = router_wt.shape[0]
    st = d // 2 // LANES
    assert st == SUBLANES, "token rows must be whole (8, 128) tiles"
    rw_hi = router_wt.astype(BF16)
    rw_lo = (router_wt - rw_hi.astype(F32)).astype(BF16)
    tm = _tile(rows_per_mod, 2 * POST_SUB)
    per = rows_per_mod // tm
    g = g1.shape[0]
    vec = pl.BlockSpec((1, d), lambda i: (0, 0))
    mod = pl.BlockSpec((1, 1, d), lambda i: (i // per, 0, 0))
    once = pl.Buffered(1)
    return pl.pallas_call(
        _post_kernel,
        grid=(m // tm,),
        in_specs=[pl.BlockSpec((tm, d), lambda i: (i, 0)),
                  pl.BlockSpec((d, d), lambda i: (0, 0), pipeline_mode=once),
                  pl.BlockSpec((tm, d), lambda i: (i, 0)),
                  mod, vec, vec, mod, mod,
                  pl.BlockSpec((e, d), lambda i: (0, 0), pipeline_mode=once),
                  pl.BlockSpec((e, d), lambda i: (0, 0), pipeline_mode=once)],
        out_specs=[pl.BlockSpec((tm, d), lambda i: (i, 0)),
                   pl.BlockSpec((tm * st, LANES), lambda i: (i, 0)),
                   pl.BlockSpec((e, tm), lambda i: (0, i))],
        out_shape=[jax.ShapeDtypeStruct((m, d), F32),
                   jax.ShapeDtypeStruct((m * st, LANES), U32),
                   jax.ShapeDtypeStruct((e, m), F32)],
        compiler_params=_cparams(("arbitrary",)),
    )(z, w_out, x2, g1.reshape(g, 1, d), post_w.reshape(1, d), pre_w.reshape(1, d),
      sc2.reshape(g, 1, d), sh2.reshape(g, 1, d), rw_hi, rw_lo)


def _route_kernel(lg_ref, b_ref, tri_ref, idx_ref, w_ref, pos_ref, cnt_ref, carry):
    @pl.when(pl.program_id(0) == 0)
    def _():
        carry[...] = jnp.zeros_like(carry)

    e, tn = lg_ref.shape
    gs = e // N_GROUPS
    neg = -jnp.inf
    big = float(e)
    s = jax.nn.sigmoid(lg_ref[...])
    choice = s + b_ref[...]

    row_g = lax.broadcasted_iota(I32, (gs, tn), 0).astype(F32)
    rows = []
    for g in range(N_GROUPS):
        blk = choice[g * gs:(g + 1) * gs]
        m1 = jnp.max(blk, axis=0, keepdims=True)
        i1 = jnp.min(jnp.where(blk == m1, row_g, big), axis=0, keepdims=True)
        m2 = jnp.max(jnp.where(row_g == i1, neg, blk), axis=0, keepdims=True)
        rows.append(m1 + m2)
    gscore = jnp.concatenate(rows, axis=0)

    row_n = lax.broadcasted_iota(I32, (N_GROUPS, tn), 0).astype(F32)
    sel = jnp.zeros((N_GROUPS, tn), F32)
    cur = gscore
    for _ in range(TOPK_GROUPS):
        m = jnp.max(cur, axis=0, keepdims=True)
        gi = jnp.min(jnp.where(cur == m, row_n, big), axis=0, keepdims=True)
        hit = row_n == gi
        sel = jnp.where(hit, 1.0, sel)
        cur = jnp.where(hit, neg, cur)

    masked = jnp.concatenate(
        [jnp.where(sel[g:g + 1] > 0.0, choice[g * gs:(g + 1) * gs], neg) for g in range(N_GROUPS)], axis=0)

    row_e = lax.broadcasted_iota(I32, (e, tn), 0).astype(F32)
    onehot = jnp.zeros((e, tn), F32)
    cur = masked
    idxs, ws = [], []
    for _ in range(TOP_K):
        m = jnp.max(cur, axis=0, keepdims=True)
        ik = jnp.min(jnp.where(cur == m, row_e, big), axis=0, keepdims=True)
        hit = row_e == ik
        ws.append(jnp.sum(jnp.where(hit, s, 0.0), axis=0, keepdims=True))
        cur = jnp.where(hit, neg, cur)
        onehot = jnp.where(hit, 1.0, onehot)
        idxs.append(ik)
    w = jnp.concatenate(ws, axis=0)
    w = w / jnp.sum(w, axis=0, keepdims=True) * ROUTED_SCALE

    prefix = jnp.dot(onehot.astype(BF16), tri_ref[...], preferred_element_type=F32)
    base = prefix + carry[...]
    pos = jnp.concatenate(
        [jnp.sum(jnp.where(row_e == ik, base, 0.0), axis=0, keepdims=True) for ik in idxs], axis=0)
    carry[...] = carry[...] + jnp.sum(onehot, axis=1, keepdims=True)

    idx_ref[...] = jnp.concatenate(idxs, axis=0).astype(I32)
    w_ref[...] = w
    pos_ref[...] = pos.astype(I32)
    cnt_ref[...] = jnp.broadcast_to(carry[...], cnt_ref.shape).astype(I32)


def _route(logits_t, bias):
    e, n = logits_t.shape
    tn = _tile(n, 512)
    tri = (jnp.arange(tn)[:, None] < jnp.arange(tn)[None, :]).astype(BF16)
    kspec = pl.BlockSpec((TOP_K, tn), lambda i: (0, i))
    return pl.pallas_call(
        _route_kernel,
        grid=(n // tn,),
        in_specs=[pl.BlockSpec((e, tn), lambda i: (0, i)),
                  pl.BlockSpec((e, 1), lambda i: (0, 0)),
                  pl.BlockSpec((tn, tn), lambda i: (0, 0))],
        out_specs=[kspec, kspec, kspec, pl.BlockSpec((e, LANES), lambda i: (0, 0))],
        out_shape=[jax.ShapeDtypeStruct((TOP_K, n), I32), jax.ShapeDtypeStruct((TOP_K, n), F32),
                   jax.ShapeDtypeStruct((TOP_K, n), I32), jax.ShapeDtypeStruct((e, LANES), I32)],
        scratch_shapes=[pltpu.VMEM((e, 1), F32)],
        compiler_params=_cparams(("arbitrary",)),
    )(logits_t, bias.reshape(e, 1), tri)


def _dest_kernel(idx_ref, pos_ref, off_ref, o_ref):
    e = off_ref.shape[0]
    tn = idx_ref.shape[1]
    row_e = lax.broadcasted_iota(I32, (e, tn), 0)
    off = off_ref[...]
    idx = idx_ref[...]
    rows = [jnp.sum(jnp.where(row_e == idx[k:k + 1], off, 0.0), axis=0, keepdims=True) for k in range(TOP_K)]
    o_ref[...] = jnp.concatenate(rows, axis=0).astype(I32) + pos_ref[...]


def _dest(top_idx, pos, pad_off):
    k, n = top_idx.shape
    e = pad_off.shape[0]
    tn = _tile(n, 512)
    spec = pl.BlockSpec((k, tn), lambda i: (0, i))
    return pl.pallas_call(
        _dest_kernel,
        grid=(n // tn,),
        in_specs=[spec, spec, pl.BlockSpec((e, 1), lambda i: (0, 0))],
        out_specs=spec,
        out_shape=jax.ShapeDtypeStruct((k, n), I32),
        compiler_params=_cparams(("arbitrary",)),
    )(top_idx, pos, pad_off.astype(F32).reshape(e, 1))


def _dispatch_kernel(cnt_ref, off_ref, nu_ref, dest_ref, f_ref, xs_hbm, zbuf, sem, zsem):
    i = pl.program_id(0)
    tt = dest_ref.shape[1]
    st = SUBLANES
    bm = zbuf.shape[0] // st
    n_exp = cnt_ref.shape[0]
    nb = xs_hbm.shape[0] // (bm * st)

    def zero_fill(wait):
        def fill(start, rows):
            cp = pltpu.make_async_copy(zbuf.at[pl.ds(0, rows * st), :],
                                       xs_hbm.at[pl.ds(pl.multiple_of(start * st, st), rows * st), :], zsem)
            if wait:
                cp.wait()
            else:
                cp.start()

        def per_expert(e, c):
            cnt = cnt_ref[e]
            npad = (-cnt) & (bm - 1)
            start = off_ref[e] + cnt
            p = bm // 2
            while p >= 1:
                @pl.when((npad & p) != 0)
                def _(start=start, p=p):
                    fill(start, p)
                start = start + (npad & p)
                p //= 2
            return c
        lax.fori_loop(0, n_exp, per_expert, 0)

        def per_block(b, c):
            fill(b * bm, bm)
            return c
        lax.fori_loop(nu_ref[0], nb, per_block, 0)

    def row_copy(j, dst):
        return pltpu.make_async_copy(f_ref.at[pl.ds(pl.multiple_of(j * st, st), st), :],
                                     xs_hbm.at[pl.ds(pl.multiple_of(dst * st, st), st), :], sem)

    @pl.when(i == 0)
    def _():
        zbuf[...] = jnp.zeros_like(zbuf)
        zero_fill(False)

    def issue(j, c):
        base = j * DMA_GROUP
        dsts = [[dest_ref[k, base + u] for k in range(TOP_K)] for u in range(DMA_GROUP)]
        for u in range(DMA_GROUP):
            for k in range(TOP_K):
                row_copy(base + u, dsts[u][k]).start(priority=k % 2)

        @pl.when(j >= DMA_LAG)
        def _():
            wait_group()
        return c

    def wait_group():
        for _ in range(DMA_GROUP * TOP_K):
            row_copy(0, 0).wait()

    n_groups = tt // DMA_GROUP
    lax.fori_loop(0, n_groups, issue, 0)

    def drain(j, c):
        wait_group()
        return c
    lax.fori_loop(0, min(DMA_LAG, n_groups), drain, 0)

    @pl.when(i == pl.num_programs(0) - 1)
    def _():
        zero_fill(True)


def _dispatch(counts, pad_off, n_used, dest, f_packed, n_slots):
    st = SUBLANES
    n = f_packed.shape[0] // st
    tt = _tile(n, 512)
    grid_spec = pltpu.PrefetchScalarGridSpec(
        num_scalar_prefetch=3,
        grid=(n // tt,),
        in_specs=[pl.BlockSpec((TOP_K, tt), lambda i, c, o, u: (0, i), memory_space=pltpu.SMEM),
                  pl.BlockSpec((tt * st, LANES), lambda i, c, o, u: (i, 0))],
        out_specs=pl.BlockSpec(memory_space=pl.ANY),
        scratch_shapes=[pltpu.VMEM((MOE_BLOCK * st, LANES), U32), pltpu.SemaphoreType.DMA(()),
                        pltpu.SemaphoreType.DMA(())],
    )
    return pl.pallas_call(
        _dispatch_kernel,
        grid_spec=grid_spec,
        out_shape=jax.ShapeDtypeStruct((n_slots * st, LANES), U32),
        compiler_params=_cparams(("arbitrary",)),
    )(counts, pad_off, n_used, dest, f_packed)


def _cast_rows(src, dst):
    rows = src.shape[0]
    ch = CAST_ROWS * LANES // src.shape[1]

    def body(j, c):
        r0 = pl.multiple_of(j * ch, ch)
        dst[pl.ds(r0, ch), :] = src[pl.ds(r0, ch), :].astype(dst.dtype)
        return c
    lax.fori_loop(0, rows // ch, body, 0, unroll=4)


def _expert_kernel(ob_ref, act_ref, na_ref, nu_ref, x_ref, w1_hbm, w3_hbm, w2_hbm, y_ref,
                   w1f, w3f, w2f, w1b, w3b, w2b, wsem):
    b = pl.program_id(0)
    n_used = nu_ref[0]
    n_act = na_ref[0]
    bm = MOE_BLOCK
    hw = w1b.shape[0] // 2

    def fetch(i, wait):
        e = act_ref[i]
        slot = i % 2
        for src, dst in ((w1_hbm, w1f), (w3_hbm, w3f), (w2_hbm, w2f)):
            cp = pltpu.make_async_copy(src.at[e], dst.at[slot], wsem.at[slot])
            if wait:
                cp.wait()
            else:
                cp.start()

    @pl.when(b < n_used)
    def _():
        i = ob_ref[b]
        first = jnp.logical_or(b == 0, i != ob_ref[jnp.maximum(b - 1, 0)])

        @pl.when(b == 0)
        def _():
            fetch(0, False)

            @pl.when(n_act > 1)
            def _():
                fetch(1, False)

        @pl.when(first)
        def _():
            fetch(i, True)
            slot = i % 2
            for src, dst in ((w1f, w1b), (w3f, w3b), (w2f, w2b)):
                _cast_rows(src.at[slot], dst)

            @pl.when(i + 2 < n_act)
            def _():
                fetch(i + 2, False)

        lo, hi = _unpack_pairs(_load_token_tiles(x_ref, bm))
        lo = lo.astype(BF16)
        hi = hi.astype(BF16)
        h1 = (jnp.dot(lo, w1b[0:hw, :], preferred_element_type=F32)
              + jnp.dot(hi, w1b[hw:, :], preferred_element_type=F32))
        h3 = (jnp.dot(lo, w3b[0:hw, :], preferred_element_type=F32)
              + jnp.dot(hi, w3b[hw:, :], preferred_element_type=F32))
        hid = (h1 * jax.nn.sigmoid(h1) * h3).astype(BF16)
        _store_token_tiles(y_ref, _pack_pairs(jnp.dot(hid, w2b[...], preferred_element_type=F32)))

    @pl.when(b >= n_used)
    def _():
        y_ref[...] = jnp.zeros_like(y_ref)


def _experts(xs, blk_ord, active, n_active, n_used, w1, w3, w2):
    nb = blk_ord.shape[0]
    e, d, fdim = w1.shape
    rows = MOE_BLOCK * (d // 2 // LANES)
    grid_spec = pltpu.PrefetchScalarGridSpec(
        num_scalar_prefetch=4,
        grid=(nb,),
        in_specs=[pl.BlockSpec((rows, LANES), lambda b, ob, ac, na, nu: (jnp.minimum(b, nu[0] - 1), 0)),
                  pl.BlockSpec(memory_space=pl.ANY),
                  pl.BlockSpec(memory_space=pl.ANY),
                  pl.BlockSpec(memory_space=pl.ANY)],
        out_specs=pl.BlockSpec((rows, LANES), lambda b, ob, ac, na, nu: (b, 0)),
        scratch_shapes=[pltpu.VMEM((2, d, fdim), F32), pltpu.VMEM((2, d, fdim), F32), pltpu.VMEM((2, fdim, d), F32),
                        pltpu.VMEM((d, fdim), BF16), pltpu.VMEM((d, fdim), BF16), pltpu.VMEM((fdim, d), BF16),
                        pltpu.SemaphoreType.DMA((2,))],
    )
    return pl.pallas_call(
        _expert_kernel,
        grid_spec=grid_spec,
        out_shape=jax.ShapeDtypeStruct((nb * rows, LANES), U32),
        compiler_params=_cparams(("arbitrary",)),
    )(blk_ord, active, n_active, n_used, xs, w1, w3, w2)


def _combine_kernel(dcur_ref, dnxt_ref, wt_ref, f_ref, x1_ref, g2_ref, pw_ref, s1_ref, s3_ref, s2_ref, ys_hbm,
                    o_ref, ybuf, sem):
    i = pl.program_id(0)
    tm = x1_ref.shape[0]
    hw = x1_ref.shape[1] // 2
    st = f_ref.shape[0] // tm
    slot = i % 2

    def row_copy(src, k, r, s):
        return pltpu.make_async_copy(ys_hbm.at[pl.ds(pl.multiple_of(src * st, st), st), :],
                                     ybuf.at[s, k, pl.ds(pl.multiple_of(r * st, st), st), :], sem.at[s])

    def start_gather(d_ref, s):
        def body(j, c):
            base = j * DMA_GROUP
            srcs = [[d_ref[k, base + u] for k in range(TOP_K)] for u in range(DMA_GROUP)]
            for u in range(DMA_GROUP):
                for k in range(TOP_K):
                    row_copy(srcs[u][k], k, base + u, s).start(priority=k % 2)
            return c
        lax.fori_loop(0, tm // DMA_GROUP, body, 0)

    def wait_gather(s):
        def body(r, c):
            for k in range(TOP_K):
                row_copy(0, k, r, s).wait()
            return c
        lax.fori_loop(0, tm, body, 0, unroll=2)

    @pl.when(i == 0)
    def _():
        start_gather(dcur_ref, 0)

    @pl.when(i + 1 < pl.num_programs(0))
    def _():
        start_gather(dnxt_ref, 1 - slot)

    lo, hi = _unpack_pairs(_load_token_tiles(f_ref, tm))
    lo = lo.astype(BF16)
    hi = hi.astype(BF16)
    h1 = (jnp.dot(lo, s1_ref[0:hw, :], preferred_element_type=F32)
          + jnp.dot(hi, s1_ref[hw:, :], preferred_element_type=F32))
    h3 = (jnp.dot(lo, s3_ref[0:hw, :], preferred_element_type=F32)
          + jnp.dot(hi, s3_ref[hw:, :], preferred_element_type=F32))
    shared = jnp.dot((h1 * jax.nn.sigmoid(h1) * h3).astype(BF16), s2_ref[...], preferred_element_type=F32)

    wait_gather(slot)
    wt = wt_ref[...]
    r_lo = shared[:, :hw]
    r_hi = shared[:, hw:]
    for k in range(TOP_K):
        a, c = _unpack_pairs(_load_token_tiles(ybuf.at[slot, k], tm))
        w = wt[:, k:k + 1]
        r_lo = r_lo + a * w
        r_hi = r_hi + c * w
    moe = jnp.concatenate([r_lo, r_hi], axis=1)
    o_ref[...] = x1_ref[...] + g2_ref[0] * (_rms(moe) * pw_ref[...])


def _combine(ys, dest, top_wt, f_packed, x1, g2, post_w, s1, s3, s2, rows_per_mod):
    m, d = x1.shape
    st = d // 2 // LANES
    fdim = s1.shape[1]
    tm = _tile(rows_per_mod, 256)
    per = rows_per_mod // tm
    g = g2.shape[0]
    nblk = m // tm
    smem = functools.partial(pl.BlockSpec, memory_space=pltpu.SMEM)
    return pl.pallas_call(
        _combine_kernel,
        grid=(nblk,),
        in_specs=[smem((TOP_K, tm), lambda i: (0, i)),
                  smem((TOP_K, tm), lambda i: (0, jnp.minimum(i + 1, nblk - 1))),
                  pl.BlockSpec((tm, TOP_K), lambda i: (i, 0)),
                  pl.BlockSpec((tm * st, LANES), lambda i: (i, 0)),
                  pl.BlockSpec((tm, d), lambda i: (i, 0)),
                  pl.BlockSpec((1, 1, d), lambda i: (i // per, 0, 0)),
                  pl.BlockSpec((1, d), lambda i: (0, 0)),
                  pl.BlockSpec((d, fdim), lambda i: (0, 0)),
                  pl.BlockSpec((d, fdim), lambda i: (0, 0)),
                  pl.BlockSpec((fdim, d), lambda i: (0, 0)),
                  pl.BlockSpec(memory_space=pl.ANY)],
        out_specs=pl.BlockSpec((tm, d), lambda i: (i, 0)),
        out_shape=jax.ShapeDtypeStruct((m, d), F32),
        scratch_shapes=[pltpu.VMEM((2, TOP_K, tm * st, LANES), U32), pltpu.SemaphoreType.DMA((2,))],
        compiler_params=_cparams(("arbitrary",)),
    )(dest, dest, top_wt, f_packed, x1, g2.reshape(g, 1, d), post_w.reshape(1, d), s1, s3, s2, ys)


def _layer(x, c, ctx, c_ctx, ada_w, ada_b, mix_pre_norm, mix_post_norm, ffn_pre_norm, ffn_post_norm,
           w_in, ret_decay_fwd, ret_decay_bwd, attn_q_norm, attn_k_norm, w_ret_o, w_attn_o, w_out,
           router_w, router_b, exp_w1, exp_w3, exp_w2, shared_w1, shared_w3, shared_w2):
    b, t, d = x.shape
    lc = ctx.shape[1]
    n = b * t
    rq_w, rv_w = RET_HEADS * RET_QK_DIM, RET_HEADS * RET_V_DIM
    aq_w, akv_w = ATTN_HEADS * ATTN_HEAD_DIM, ATTN_KV_HEADS * ATTN_HEAD_DIM
    names = ("rq", "rk", "rv", "rg", "aq", "ak", "av", "gr", "ga")
    widths = (rq_w, rq_w, rv_w, rv_w, aq_w, akv_w, akv_w, d, d)
    offs, o = {}, 0
    for nm, wd in zip(names, widths):
        offs[nm] = o
        o += wd

    rows = -(-(b + 1) // 16) * 16
    cond = jnp.zeros((rows, d), F32).at[:b].set(c).at[b].set(c_ctx)
    mod = _adaln(cond, ada_w, ada_b)
    sh1, sc1, g1, sh2, sc2, g2 = [mod[:b, i * d:(i + 1) * d] for i in range(6)]
    ch1, cs1 = mod[b:b + 1, 0:d], mod[b:b + 1, d:2 * d]

    x2 = x.reshape(n, d)
    h_x = _norm_mod(x2, mix_pre_norm, sc1, sh1, t)
    h_c = _norm_mod(ctx.reshape(b * lc, d), mix_pre_norm, cs1, ch1, b * lc)

    w_in_b = w_in.astype(BF16)
    p = _matmul(h_x, w_in_b, BF16)
    cnames = ("rv", "rk", "ak", "av")
    coffs, o = {}, 0
    for nm in cnames:
        coffs[nm] = o
        o += widths[names.index(nm)]
    w_ctx = jnp.concatenate([w_in_b[:, offs[nm]:offs[nm] + widths[names.index(nm)]] for nm in cnames], axis=1)
    pc = _matmul(h_c, w_ctx, BF16)

    lg = jnp.stack([jax.nn.log_sigmoid(ret_decay_fwd.astype(F32)),
                    jax.nn.log_sigmoid(ret_decay_bwd.astype(F32))])
    rc, rs_lo, rs_hi = _rope_tables(t, RET_QK_DIM)
    retg = _retention(p, pc, lg, rc, rs_lo + rs_hi, b, t, lc, offs, coffs)
    att = _attention(p, pc, attn_q_norm, attn_k_norm, _rope_tables(t, ATTN_HEAD_DIM), b, t, lc, offs, coffs)

    z = _merge(retg, att, p, w_ret_o.astype(BF16), w_attn_o.astype(BF16), offs)
    x1, f_packed, logits_t = _post(z, w_out.astype(BF16), x2, g1, mix_post_norm, ffn_pre_norm, sc2, sh2,
                                   router_w.T, t)

    top_idx, top_w, pos, cnt = _route(logits_t, router_b)

    bm = MOE_BLOCK
    e = router_w.shape[1]
    nb = (n * TOP_K) // bm + e
    counts = cnt[:, 0]
    padded = (counts + bm - 1) // bm * bm
    pad_end = jnp.cumsum(padded)
    pad_off = pad_end - padded
    blk_start = jnp.arange(nb, dtype=I32) * bm
    blk_expert = jnp.minimum(jnp.sum((pad_end[None, :] <= blk_start[:, None]).astype(I32), axis=1), e - 1)
    n_used = (pad_end[-1] // bm).astype(I32).reshape(1)
    is_act = (counts > 0).astype(I32)
    exp_ord = jnp.cumsum(is_act) - is_act
    n_active = jnp.sum(is_act).astype(I32).reshape(1)
    eids = jnp.arange(e, dtype=I32)
    active = jnp.sum(jnp.where((exp_ord[None, :] == eids[:, None]) & (is_act[None, :] > 0), eids[None, :], 0),
                     axis=1).astype(I32)
    blk_ord = exp_ord[blk_expert].astype(I32)

    dest = _dest(top_idx, pos, pad_off)
    xs = _dispatch(counts, pad_off, n_used, dest, f_packed, nb * bm)
    ys = _experts(xs, blk_ord, active, n_active, n_used, exp_w1, exp_w3, exp_w2)
    out = _combine(ys, dest, top_w.T, f_packed, x1, g2, ffn_post_norm, shared_w1.astype(BF16),
                   shared_w3.astype(BF16), shared_w2.astype(BF16), t)
    return out.reshape(b, t, d)


def kernel(x, c, ctx, c_ctx, ada_w, ada_b, mix_pre_norm, mix_post_norm, ffn_pre_norm, ffn_post_norm,
           w_in, ret_decay_fwd, ret_decay_bwd, attn_q_norm, attn_k_norm, w_ret_o, w_attn_o, w_out,
           router_w, router_b, exp_w1, exp_w3, exp_w2, shared_w1, shared_w3, shared_w2):
    assert ada_w.shape[0] == 1, "single-layer operation"
    return _layer(x, c, ctx, c_ctx, ada_w[0], ada_b[0], mix_pre_norm[0], mix_post_norm[0], ffn_pre_norm[0],
                  ffn_post_norm[0], w_in[0], ret_decay_fwd[0], ret_decay_bwd[0], attn_q_norm[0],
                  attn_k_norm[0], w_ret_o[0], w_attn_o[0], w_out[0], router_w[0], router_b[0],
                  exp_w1[0], exp_w3[0], exp_w2[0], shared_w1[0], shared_w3[0], shared_w2[0])
```

```python
import functools
import math

import jax
import jax.numpy as jnp
import numpy as np
from jax import lax
from jax.experimental import pallas as pl
from jax.experimental.pallas import tpu as pltpu

F32 = jnp.float32
BF16 = jnp.bfloat16
U32 = jnp.uint32
I32 = jnp.int32

GRID_W = 64
NORM_EPS = 1e-6
ROPE_THETA = 10000.0
RET_HEADS = 4
RET_QK_DIM = 256
RET_V_DIM = 512
ATTN_HEADS = 16
ATTN_KV_HEADS = 4
ATTN_HEAD_DIM = 128
N_EXPERTS = 256
TOP_K = 8
N_GROUPS = 8
TOPK_GROUPS = 4
EXPERT_DIM = 512
ROUTED_SCALE = 2.5
LOG2E = 1.4426950408889634

V7X_VMEM_BYTES = 64 * 1024 * 1024
VMEM_LIMIT = V7X_VMEM_BYTES - 12 * 1024 * 1024
LANES = 128
SUBLANES = 8

RET_CHUNK = 256
MOE_BLOCK = 256
DMA_GROUP = 2
DMA_LAG = 32
CAST_ROWS = 256
POST_SUB = 256


def _cparams(sem):
    return pltpu.CompilerParams(dimension_semantics=sem, vmem_limit_bytes=VMEM_LIMIT)


def _tile(dim, pref):
    if dim <= pref:
        return dim
    for t in range(pref, 0, -LANES):
        if dim % t == 0:
            return t
    raise ValueError((dim, pref))


def _blk(off, width):
    assert off % width == 0, (off, width)
    return off // width


def _ada_kernel(s_ref, w_ref, b_ref, o_ref):
    s = s_ref[...]
    s = (s * jax.nn.sigmoid(s)).astype(BF16)
    o_ref[...] = jnp.dot(s, w_ref[...].astype(BF16), preferred_element_type=F32) + b_ref[...]


def _adaln(cond, w, b):
    r, d = cond.shape
    n = w.shape[1]
    tn = _tile(n, 1024)
    return pl.pallas_call(
        _ada_kernel,
        grid=(n // tn,),
        in_specs=[pl.BlockSpec((r, d), lambda j: (0, 0)),
                  pl.BlockSpec((d, tn), lambda j: (0, j)),
                  pl.BlockSpec((1, tn), lambda j: (0, j))],
        out_specs=pl.BlockSpec((r, tn), lambda j: (0, j)),
        out_shape=jax.ShapeDtypeStruct((r, n), F32),
        compiler_params=_cparams(("arbitrary",)),
    )(cond, w, b.reshape(1, n))


def _rms(x):
    return x * lax.rsqrt(jnp.mean(x * x, axis=-1, keepdims=True) + NORM_EPS)


def _norm_mod_kernel(x_ref, w_ref, sc_ref, sh_ref, o_ref):
    y = _rms(x_ref[...]) * w_ref[...]
    o_ref[...] = (y * (1.0 + sc_ref[0]) + sh_ref[0]).astype(o_ref.dtype)


def _norm_mod(x2, w, sc, sh, rows_per_mod):
    m, d = x2.shape
    tm = _tile(rows_per_mod, 512)
    per = rows_per_mod // tm
    g = sc.shape[0]
    return pl.pallas_call(
        _norm_mod_kernel,
        grid=(m // tm,),
        in_specs=[pl.BlockSpec((tm, d), lambda i: (i, 0)),
                  pl.BlockSpec((1, d), lambda i: (0, 0)),
                  pl.BlockSpec((1, 1, d), lambda i: (i // per, 0, 0)),
                  pl.BlockSpec((1, 1, d), lambda i: (i // per, 0, 0))],
        out_specs=pl.BlockSpec((tm, d), lambda i: (i, 0)),
        out_shape=jax.ShapeDtypeStruct((m, d), BF16),
        compiler_params=_cparams(("arbitrary",)),
    )(x2, w.reshape(1, d), sc.reshape(g, 1, d), sh.reshape(g, 1, d))


def _proj_kernel(cols_ref, a_ref, w_ref, o_ref, wb):
    @pl.when(pl.program_id(1) == 0)
    def _():
        _cast_rows(w_ref, wb)

    o_ref[...] = jnp.dot(a_ref[...], wb[...], preferred_element_type=F32).astype(o_ref.dtype)


def _project(a, w, col_tiles, tn, out_dtype):
    m, k = a.shape
    tm = _tile(m, 1024)
    nj = len(col_tiles)
    grid_spec = pltpu.PrefetchScalarGridSpec(
        num_scalar_prefetch=1,
        grid=(nj, m // tm),
        in_specs=[pl.BlockSpec((tm, k), lambda j, i, c: (i, 0)),
                  pl.BlockSpec((k, tn), lambda j, i, c: (0, c[j]))],
        out_specs=pl.BlockSpec((tm, tn), lambda j, i, c: (i, j)),
        scratch_shapes=[pltpu.VMEM((k, tn), BF16)],
    )
    return pl.pallas_call(
        _proj_kernel,
        grid_spec=grid_spec,
        out_shape=jax.ShapeDtypeStruct((m, nj * tn), out_dtype),
        compiler_params=_cparams(("arbitrary", "arbitrary")),
    )(jnp.asarray(col_tiles, I32), a, w)


def _rope_tables(t, dim):
    half = dim // 2
    pos = jnp.arange(t, dtype=I32)
    row = (pos // GRID_W).astype(F32)
    col = (pos % GRID_W).astype(F32)
    inv = ROPE_THETA ** (-jnp.arange(0, half, 2, dtype=F32) / half)
    ar = row[:, None] * inv[None, :]
    ac = col[:, None] * inv[None, :]
    cr, sr, cc, sc = jnp.cos(ar), jnp.sin(ar), jnp.cos(ac), jnp.sin(ac)
    z = jnp.zeros_like(sr)
    c = jnp.concatenate([cr, cr, cc, cc], axis=1)
    s_lo = jnp.concatenate([-sr, z, -sc, z], axis=1)
    s_hi = jnp.concatenate([z, sr, z, sc], axis=1)
    return c, s_lo, s_hi


def _rope128(x, c, s_lo, s_hi):
    q4 = ATTN_HEAD_DIM // 4
    return (x * c + pltpu.roll(x, LANES - q4, axis=1) * s_lo + pltpu.roll(x, q4, axis=1) * s_hi)


def _rope256(x, c, s):
    parts = [pltpu.roll(x[:, i * LANES:(i + 1) * LANES], LANES // 2, axis=1)
             for i in range(RET_QK_DIM // LANES)]
    return x * c + jnp.concatenate(parts, axis=1) * s


def _ret_kernel(lg_ref, q_ref, k_ref, v_ref, g_ref, kc_ref, vc_ref, c_ref, s_ref, o_ref,
                qs, ks, acc, st):
    h = pl.program_id(1)
    lgf = lg_ref[0, h]
    lgb = lg_ref[1, h]
    t = q_ref.shape[0]
    ch = RET_CHUNK if t % RET_CHUNK == 0 else t
    nch = t // ch
    lc = kc_ref.shape[0]
    k_scale = RET_QK_DIM ** -0.5
    tn_dims = (((0,), (0,)), ((), ()))

    ii = lax.broadcasted_iota(I32, (ch, ch), 0)
    jj = lax.broadcasted_iota(I32, (ch, ch), 1)
    dif = (ii - jj).astype(F32)
    dmat = (jnp.where(dif >= 0, jnp.exp(lgf * jnp.maximum(dif, 0.0)), 0.0)
            + jnp.where(dif <= 0, jnp.exp(lgb * jnp.maximum(-dif, 0.0)), 0.0))
    pos = lax.broadcasted_iota(I32, (ch, 1), 0).astype(F32)
    qf_dec = jnp.exp(lgf * (pos + 1.0))
    kf_dec = jnp.exp(lgf * (ch - 1.0 - pos))
    qb_dec = jnp.exp(lgb * (ch - pos))
    kb_dec = jnp.exp(lgb * pos)
    chv = jnp.full((1, 1), float(ch), F32)
    cf_dec = jnp.exp(lgf * chv)
    cb_dec = jnp.exp(lgb * chv)

    cpos = lax.broadcasted_iota(I32, (lc, 1), 0).astype(F32)
    kc = kc_ref[...].astype(F32) * k_scale
    vc = vc_ref[...]
    st[0] = lax.dot_general((kc * jnp.exp(lgf * (lc - 1.0 - cpos))).astype(BF16), vc, tn_dims,
                            preferred_element_type=F32)
    st[1] = lax.dot_general((kc * jnp.exp(lgb * cpos)).astype(BF16), vc, tn_dims,
                            preferred_element_type=F32)

    def fwd(c, carry):
        r0 = pl.multiple_of(c * ch, ch)
        cs = c_ref[pl.ds(r0, ch), :]
        sn = s_ref[pl.ds(r0, ch), :]
        q = _rope256(q_ref[pl.ds(r0, ch), :].astype(F32), cs, sn)
        k = _rope256(k_ref[pl.ds(r0, ch), :].astype(F32), cs, sn) * k_scale
        v = v_ref[pl.ds(r0, ch), :]
        qb = q.astype(BF16)
        kb = k.astype(BF16)
        qs[pl.ds(r0, ch), :] = q
        ks[pl.ds(r0, ch), :] = k
        sc = lax.dot_general(qb, kb, (((1,), (1,)), ((), ())), preferred_element_type=F32) * dmat
        o = jnp.dot(sc.astype(BF16), v, preferred_element_type=F32)
        o += jnp.dot((q * qf_dec).astype(BF16), st[0].astype(BF16), preferred_element_type=F32)
        acc[pl.ds(r0, ch), :] = o
        st[0] = st[0] * cf_dec + lax.dot_general((k * kf_dec).astype(BF16), v, tn_dims,
                                                 preferred_element_type=F32)
        return carry

    lax.fori_loop(0, nch, fwd, 0, unroll=2)

    def bwd(i, carry):
        c = nch - 1 - i
        r0 = pl.multiple_of(c * ch, ch)
        q = qs[pl.ds(r0, ch), :]
        k = ks[pl.ds(r0, ch), :]
        v = v_ref[pl.ds(r0, ch), :]
        o = jnp.dot((q * qb_dec).astype(BF16), st[1].astype(BF16), preferred_element_type=F32)
        tot = acc[pl.ds(r0, ch), :] + o
        gate = g_ref[pl.ds(r0, ch), :].astype(F32)
        o_ref[pl.ds(r0, ch), :] = (gate * jax.nn.sigmoid(gate) * _rms(tot)).astype(o_ref.dtype)
        st[1] = st[1] * cb_dec + lax.dot_general((k * kb_dec).astype(BF16), v, tn_dims,
                                                 preferred_element_type=F32)
        return carry

    lax.fori_loop(0, nch, bwd, 0, unroll=2)


def _retention(p, pc, lg, tab_c, tab_s, b, t, lc, offs, coffs):
    h, dk, dv = RET_HEADS, RET_QK_DIM, RET_V_DIM
    qo, ko, vo, go = (_blk(offs["rq"], dk), _blk(offs["rk"], dk), _blk(offs["rv"], dv), _blk(offs["rg"], dv))
    kco, vco = _blk(coffs["rk"], dk), _blk(coffs["rv"], dv)
    grid_spec = pltpu.PrefetchScalarGridSpec(
        num_scalar_prefetch=1,
        grid=(b, h),
        in_specs=[pl.BlockSpec((t, dk), lambda bi, hi, lg: (bi, qo + hi)),
                  pl.BlockSpec((t, dk), lambda bi, hi, lg: (bi, ko + hi)),
                  pl.BlockSpec((t, dv), lambda bi, hi, lg: (bi, vo + hi)),
                  pl.BlockSpec((t, dv), lambda bi, hi, lg: (bi, go + hi)),
                  pl.BlockSpec((lc, dk), lambda bi, hi, lg: (bi, kco + hi)),
                  pl.BlockSpec((lc, dv), lambda bi, hi, lg: (bi, vco + hi)),
                  pl.BlockSpec((t, dk), lambda bi, hi, lg: (0, 0)),
                  pl.BlockSpec((t, dk), lambda bi, hi, lg: (0, 0))],
        out_specs=pl.BlockSpec((t, dv), lambda bi, hi, lg: (bi, hi)),
        scratch_shapes=[pltpu.VMEM((t, dk), F32), pltpu.VMEM((t, dk), F32),
                        pltpu.VMEM((t, dv), F32), pltpu.VMEM((2, dk, dv), F32)],
    )
    return pl.pallas_call(
        _ret_kernel,
        grid_spec=grid_spec,
        out_shape=jax.ShapeDtypeStruct((b * t, h * dv), BF16),
        compiler_params=_cparams(("arbitrary", "arbitrary")),
    )(lg, p, p, p, p, pc, pc, tab_c, tab_s)


def _attn_kernel(q_ref, kx_ref, vx_ref, kc_ref, vc_ref, qn_ref, kn_ref, c_ref, sl_ref, sh_ref, o_ref,
                 k_s, v_s):
    qt = pl.program_id(2)
    tq = q_ref.shape[0]
    lc = kc_ref.shape[0]
    hd = ATTN_HEAD_DIM
    scale = hd ** -0.5 * LOG2E

    @pl.when(qt == 0)
    def _():
        kn = kn_ref[...]
        k_s[0:lc, :] = (_rms(kc_ref[...].astype(F32)) * kn).astype(BF16)
        kx = _rms(kx_ref[...].astype(F32)) * kn
        k_s[lc:, :] = _rope128(kx, c_ref[...], sl_ref[...], sh_ref[...]).astype(BF16)
        v_s[0:lc, 0:hd] = vc_ref[...]
        v_s[lc:, 0:hd] = vx_ref[...]
        col = lax.broadcasted_iota(I32, (v_s.shape[0], hd), 1)
        v_s[:, hd:] = jnp.where(col == 0, 1.0, 0.0).astype(BF16)

    r0 = pl.multiple_of(qt * tq, tq)
    c = c_ref[pl.ds(r0, tq), :]
    sl = sl_ref[pl.ds(r0, tq), :]
    sh = sh_ref[pl.ds(r0, tq), :]
    qn = qn_ref[...]
    kk = k_s[...]
    vv = v_s[...]
    for g in range(ATTN_HEADS // ATTN_KV_HEADS):
        q = _rms(q_ref[:, g * hd:(g + 1) * hd].astype(F32)) * qn
        q = (_rope128(q, c, sl, sh) * scale).astype(BF16)
        s = lax.dot_general(q, kk, (((1,), (1,)), ((), ())), preferred_element_type=F32)
        m = jnp.max(s, axis=-1, keepdims=True)
        p = jnp.exp2(s - m).astype(BF16)
        ov = jnp.dot(p, vv, preferred_element_type=F32)
        o = ov[:, 0:hd] / ov[:, hd:hd + 1]
        o_ref[:, g * hd:(g + 1) * hd] = o.astype(o_ref.dtype)


def _attention(p, pc, qn, kn, tabs, b, t, lc, offs, coffs):
    hd, kvh = ATTN_HEAD_DIM, ATTN_KV_HEADS
    gw = (ATTN_HEADS // kvh) * hd
    tq = _tile(t, 256)
    qo, ko, vo = _blk(offs["aq"], gw), _blk(offs["ak"], hd), _blk(offs["av"], hd)
    kco, vco = _blk(coffs["ak"], hd), _blk(coffs["av"], hd)
    tab = pl.BlockSpec((t, hd), lambda bi, ki, qi: (0, 0))
    return pl.pallas_call(
        _attn_kernel,
        grid=(b, kvh, t // tq),
        in_specs=[pl.BlockSpec((tq, gw), lambda bi, ki, qi: (bi * (t // tq) + qi, qo + ki)),
                  pl.BlockSpec((t, hd), lambda bi, ki, qi: (bi, ko + ki)),
                  pl.BlockSpec((t, hd), lambda bi, ki, qi: (bi, vo + ki)),
                  pl.BlockSpec((lc, hd), lambda bi, ki, qi: (bi, kco + ki)),
                  pl.BlockSpec((lc, hd), lambda bi, ki, qi: (bi, vco + ki)),
                  pl.BlockSpec((1, hd), lambda bi, ki, qi: (0, 0)),
                  pl.BlockSpec((1, hd), lambda bi, ki, qi: (0, 0)),
                  tab, tab, tab],
        out_specs=pl.BlockSpec((tq, gw), lambda bi, ki, qi: (bi * (t // tq) + qi, ki)),
        out_shape=jax.ShapeDtypeStruct((b * t, ATTN_HEADS * hd), BF16),
        scratch_shapes=[pltpu.VMEM((lc + t, hd), BF16), pltpu.VMEM((lc + t, 2 * hd), BF16)],
        compiler_params=_cparams(("arbitrary", "arbitrary", "arbitrary")),
    )(p, p, p, pc, pc, qn.reshape(1, hd), kn.reshape(1, hd), *tabs)


def _merge_kernel(r_ref, a_ref, gr_ref, ga_ref, wr_ref, wa_ref, o_ref):
    ret = jnp.dot(r_ref[...], wr_ref[...], preferred_element_type=F32)
    att = jnp.dot(a_ref[...], wa_ref[...], preferred_element_type=F32)
    z = (jax.nn.sigmoid(gr_ref[...].astype(F32)) * ret + jax.nn.sigmoid(ga_ref[...].astype(F32)) * att)
    o_ref[...] = z.astype(o_ref.dtype)


def _merge(retg, att, p, w_ret_o, w_attn_o, offs):
    m, kr = retg.shape
    ka = att.shape[1]
    d = w_ret_o.shape[1]
    tm, tn = _tile(m, 1024), _tile(d, 256 if d < 2048 else 512)
    gro, gao = _blk(offs["gr"], tn), _blk(offs["ga"], tn)
    return pl.pallas_call(
        _merge_kernel,
        grid=(m // tm, d // tn),
        in_specs=[pl.BlockSpec((tm, kr), lambda i, j: (i, 0)),
                  pl.BlockSpec((tm, ka), lambda i, j: (i, 0)),
                  pl.BlockSpec((tm, tn), lambda i, j: (i, gro + j)),
                  pl.BlockSpec((tm, tn), lambda i, j: (i, gao + j)),
                  pl.BlockSpec((kr, tn), lambda i, j: (0, j)),
                  pl.BlockSpec((ka, tn), lambda i, j: (0, j))],
        out_specs=pl.BlockSpec((tm, tn), lambda i, j: (i, j)),
        out_shape=jax.ShapeDtypeStruct((m, d), BF16),
        compiler_params=_cparams(("arbitrary", "arbitrary")),
    )(retg, att, p, p, w_ret_o, w_attn_o)


def _pack_pairs(x):
    hw = x.shape[1] // 2
    lo = pltpu.bitcast(x[:, :hw].astype(BF16).astype(F32), U32)
    hi = pltpu.bitcast(x[:, hw:].astype(BF16).astype(F32), U32)
    return (hi & jnp.uint32(0xFFFF0000)) | (lo >> 16)


def _store_token_tiles(ref, packed):
    m = packed.shape[0]
    st = packed.shape[1] // LANES
    for s in range(st):
        ref[pl.ds(s, m, stride=st), :] = packed[:, s * LANES:(s + 1) * LANES]


def _load_token_tiles(ref, m):
    st = ref.shape[0] // m
    return jnp.concatenate([ref[pl.ds(s, m, stride=st), :] for s in range(st)], axis=1)


def _unpack_pairs(w):
    lo = pltpu.bitcast(w << 16, F32)
    hi = pltpu.bitcast(w & jnp.uint32(0xFFFF0000), F32)
    return lo, hi


def _post_kernel(z_ref, wo_ref, x_ref, g1_ref, pw_ref, fw_ref, sc_ref, sh_ref, rwh_ref, rwl_ref,
                 x1_ref, f_ref, lg_ref):
    tm = x_ref.shape[0]
    sub = min(tm, POST_SUB)
    st = f_ref.shape[0] // tm
    nt = (((1,), (1,)), ((), ()))
    for h in range(tm // sub):
        r0 = h * sub
        y = jnp.dot(z_ref[r0:r0 + sub, :], wo_ref[...], preferred_element_type=F32)
        x1 = x_ref[r0:r0 + sub, :] + g1_ref[0] * (_rms(y) * pw_ref[...])
        x1_ref[r0:r0 + sub, :] = x1
        f = (_rms(x1) * fw_ref[...]) * (1.0 + sc_ref[0]) + sh_ref[0]
        _store_token_tiles(f_ref.at[r0 * st:(r0 + sub) * st, :], _pack_pairs(f))
        fh = f.astype(BF16)
        fl = (f - fh.astype(F32)).astype(BF16)
        lg_ref[:, r0:r0 + sub] = (lax.dot_general(rwh_ref[...], fh, nt, preferred_element_type=F32)
                                  + lax.dot_general(rwh_ref[...], fl, nt, preferred_element_type=F32)
                                  + lax.dot_general(rwl_ref[...], fh, nt, preferred_element_type=F32))


def _post(z, w_out, x2, g1, post_w, pre_w, sc2, sh2, router_wt, rows_per_mod):
    m, d = x2.shape
    e = router_wt.shape[0]
    st = d // 2 // LANES
    assert st == SUBLANES, "token rows must be whole (8, 128) tiles"
    rw_hi = router_wt.astype(BF16)
    rw_lo = (router_wt - rw_hi.astype(F32)).astype(BF16)
    tm = _tile(rows_per_mod, 2 * POST_SUB)
    per = rows_per_mod // tm
    g = g1.shape[0]
    vec = pl.BlockSpec((1, d), lambda i: (0, 0))
    mod = pl.BlockSpec((1, 1, d), lambda i: (i // per, 0, 0))
    once = pl.Buffered(1)
    return pl.pallas_call(
        _post_kernel,
        grid=(m // tm,),
        in_specs=[pl.BlockSpec((tm, d), lambda i: (i, 0)),
                  pl.BlockSpec((d, d), lambda i: (0, 0), pipeline_mode=once),
                  pl.BlockSpec((tm, d), lambda i: (i, 0)),
                  mod, vec, vec, mod, mod,
                  pl.BlockSpec((e, d), lambda i: (0, 0), pipeline_mode=once),
                  pl.BlockSpec((e, d), lambda i: (0, 0), pipeline_mode=once)],
        out_specs=[pl.BlockSpec((tm, d), lambda i: (i, 0)),
                   pl.BlockSpec((tm * st, LANES), lambda i: (i, 0)),
                   pl.BlockSpec((e, tm), lambda i: (0, i))],
        out_shape=[jax.ShapeDtypeStruct((m, d), F32),
                   jax.ShapeDtypeStruct((m * st, LANES), U32),
                   jax.ShapeDtypeStruct((e, m), F32)],
        compiler_params=_cparams(("arbitrary",)),
    )(z, w_out, x2, g1.reshape(g, 1, d), post_w.reshape(1, d), pre_w.reshape(1, d),
      sc2.reshape(g, 1, d), sh2.reshape(g, 1, d), rw_hi, rw_lo)


def _route_kernel(lg_ref, b_ref, tri_ref, idx_ref, w_ref, pos_ref, cnt_ref, carry):
    @pl.when(pl.program_id(0) == 0)
    def _():
        carry[...] = jnp.zeros_like(carry)

    e, tn = lg_ref.shape
    gs = e // N_GROUPS
    neg = -jnp.inf
    big = float(e)
    s = jax.nn.sigmoid(lg_ref[...])
    choice = s + b_ref[...]

    row_g = lax.broadcasted_iota(I32, (gs, tn), 0).astype(F32)
    rows = []
    for g in range(N_GROUPS):
        blk = choice[g * gs:(g + 1) * gs]
        m1 = jnp.max(blk, axis=0, keepdims=True)
        i1 = jnp.min(jnp.where(blk == m1, row_g, big), axis=0, keepdims=True)
        m2 = jnp.max(jnp.where(row_g == i1, neg, blk), axis=0, keepdims=True)
        rows.append(m1 + m2)
    gscore = jnp.concatenate(rows, axis=0)

    row_n = lax.broadcasted_iota(I32, (N_GROUPS, tn), 0).astype(F32)
    sel = jnp.zeros((N_GROUPS, tn), F32)
    cur = gscore
    for _ in range(TOPK_GROUPS):
        m = jnp.max(cur, axis=0, keepdims=True)
        gi = jnp.min(jnp.where(cur == m, row_n, big), axis=0, keepdims=True)
        hit = row_n == gi
        sel = jnp.where(hit, 1.0, sel)
        cur = jnp.where(hit, neg, cur)

    masked = jnp.concatenate(
        [jnp.where(sel[g:g + 1] > 0.0, choice[g * gs:(g + 1) * gs], neg) for g in range(N_GROUPS)], axis=0)

    row_e = lax.broadcasted_iota(I32, (e, tn), 0).astype(F32)
    onehot = jnp.zeros((e, tn), F32)
    cur = masked
    idxs, ws = [], []
    for _ in range(TOP_K):
        m = jnp.max(cur, axis=0, keepdims=True)
        ik = jnp.min(jnp.where(cur == m, row_e, big), axis=0, keepdims=True)
        hit = row_e == ik
        ws.append(jnp.sum(jnp.where(hit, s, 0.0), axis=0, keepdims=True))
        cur = jnp.where(hit, neg, cur)
        onehot = jnp.where(hit, 1.0, onehot)
        idxs.append(ik)
    w = jnp.concatenate(ws, axis=0)
    w = w / jnp.sum(w, axis=0, keepdims=True) * ROUTED_SCALE

    prefix = jnp.dot(onehot.astype(BF16), tri_ref[...], preferred_element_type=F32)
    base = prefix + carry[...]
    pos = jnp.concatenate(
        [jnp.sum(jnp.where(row_e == ik, base, 0.0), axis=0, keepdims=True) for ik in idxs], axis=0)
    carry[...] = carry[...] + jnp.sum(onehot, axis=1, keepdims=True)

    idx_ref[...] = jnp.concatenate(idxs, axis=0).astype(I32)
    w_ref[...] = w
    pos_ref[...] = pos.astype(I32)
    cnt_ref[...] = jnp.broadcast_to(carry[...], cnt_ref.shape).astype(I32)


def _route(logits_t, bias):
    e, n = logits_t.shape
    tn = _tile(n, 512)
    tri = (jnp.arange(tn)[:, None] < jnp.arange(tn)[None, :]).astype(BF16)
    kspec = pl.BlockSpec((TOP_K, tn), lambda i: (0, i))
    return pl.pallas_call(
        _route_kernel,
        grid=(n // tn,),
        in_specs=[pl.BlockSpec((e, tn), lambda i: (0, i)),
                  pl.BlockSpec((e, 1), lambda i: (0, 0)),
                  pl.BlockSpec((tn, tn), lambda i: (0, 0))],
        out_specs=[kspec, kspec, kspec, pl.BlockSpec((e, LANES), lambda i: (0, 0))],
        out_shape=[jax.ShapeDtypeStruct((TOP_K, n), I32), jax.ShapeDtypeStruct((TOP_K, n), F32),
                   jax.ShapeDtypeStruct((TOP_K, n), I32), jax.ShapeDtypeStruct((e, LANES), I32)],
        scratch_shapes=[pltpu.VMEM((e, 1), F32)],
        compiler_params=_cparams(("arbitrary",)),
    )(logits_t, bias.reshape(e, 1), tri)


def _dest_kernel(idx_ref, pos_ref, off_ref, o_ref):
    e = off_ref.shape[0]
    tn = idx_ref.shape[1]
    row_e = lax.broadcasted_iota(I32, (e, tn), 0)
    off = off_ref[...]
    idx = idx_ref[...]
    rows = [jnp.sum(jnp.where(row_e == idx[k:k + 1], off, 0.0), axis=0, keepdims=True) for k in range(TOP_K)]
    o_ref[...] = jnp.concatenate(rows, axis=0).astype(I32) + pos_ref[...]


def _dest(top_idx, pos, pad_off):
    k, n = top_idx.shape
    e = pad_off.shape[0]
    tn = _tile(n, 512)
    spec = pl.BlockSpec((k, tn), lambda i: (0, i))
    return pl.pallas_call(
        _dest_kernel,
        grid=(n // tn,),
        in_specs=[spec, spec, pl.BlockSpec((e, 1), lambda i: (0, 0))],
        out_specs=spec,
        out_shape=jax.ShapeDtypeStruct((k, n), I32),
        compiler_params=_cparams(("arbitrary",)),
    )(top_idx, pos, pad_off.astype(F32).reshape(e, 1))


def _dispatch_kernel(cnt_ref, off_ref, nu_ref, dest_ref, f_ref, xs_hbm, zbuf, sem, zsem):
    i = pl.program_id(0)
    tt = dest_ref.shape[1]
    st = SUBLANES
    bm = zbuf.shape[0] // st
    n_exp = cnt_ref.shape[0]
    nb = xs_hbm.shape[0] // (bm * st)

    def zero_fill(wait):
        def fill(start, rows):
            cp = pltpu.make_async_copy(zbuf.at[pl.ds(0, rows * st), :],
                                       xs_hbm.at[pl.ds(pl.multiple_of(start * st, st), rows * st), :], zsem)
            if wait:
                cp.wait()
            else:
                cp.start()

        def per_expert(e, c):
            cnt = cnt_ref[e]
            npad = (-cnt) & (bm - 1)
            start = off_ref[e] + cnt
            p = bm // 2
            while p >= 1:
                @pl.when((npad & p) != 0)
                def _(start=start, p=p):
                    fill(start, p)
                start = start + (npad & p)
                p //= 2
            return c
        lax.fori_loop(0, n_exp, per_expert, 0)

        def per_block(b, c):
            fill(b * bm, bm)
            return c
        lax.fori_loop(nu_ref[0], nb, per_block, 0)

    def row_copy(j, dst):
        return pltpu.make_async_copy(f_ref.at[pl.ds(pl.multiple_of(j * st, st), st), :],
                                     xs_hbm.at[pl.ds(pl.multiple_of(dst * st, st), st), :], sem)

    @pl.when(i == 0)
    def _():
        zbuf[...] = jnp.zeros_like(zbuf)
        zero_fill(False)

    def issue(j, c):
        base = j * DMA_GROUP
        dsts = [[dest_ref[k, base + u] for k in range(TOP_K)] for u in range(DMA_GROUP)]
        for u in range(DMA_GROUP):
            for k in range(TOP_K):
                row_copy(base + u, dsts[u][k]).start(priority=k % 2)

        @pl.when(j >= DMA_LAG)
        def _():
            wait_group()
        return c

    def wait_group():
        for _ in range(DMA_GROUP * TOP_K):
            row_copy(0, 0).wait()

    n_groups = tt // DMA_GROUP
    lax.fori_loop(0, n_groups, issue, 0)

    def drain(j, c):
        wait_group()
        return c
    lax.fori_loop(0, min(DMA_LAG, n_groups), drain, 0)

    @pl.when(i == pl.num_programs(0) - 1)
    def _():
        zero_fill(True)


def _dispatch(counts, pad_off, n_used, dest, f_packed, n_slots):
    st = SUBLANES
    n = f_packed.shape[0] // st
    tt = _tile(n, 512)
    grid_spec = pltpu.PrefetchScalarGridSpec(
        num_scalar_prefetch=3,
        grid=(n // tt,),
        in_specs=[pl.BlockSpec((TOP_K, tt), lambda i, c, o, u: (0, i), memory_space=pltpu.SMEM),
                  pl.BlockSpec((tt * st, LANES), lambda i, c, o, u: (i, 0))],
        out_specs=pl.BlockSpec(memory_space=pl.ANY),
        scratch_shapes=[pltpu.VMEM((MOE_BLOCK * st, LANES), U32), pltpu.SemaphoreType.DMA(()),
                        pltpu.SemaphoreType.DMA(())],
    )
    return pl.pallas_call(
        _dispatch_kernel,
        grid_spec=grid_spec,
        out_shape=jax.ShapeDtypeStruct((n_slots * st, LANES), U32),
        compiler_params=_cparams(("arbitrary",)),
    )(counts, pad_off, n_used, dest, f_packed)


def _cast_rows(src, dst):
    rows = src.shape[0]
    ch = CAST_ROWS * LANES // src.shape[1]

    def body(j, c):
        r0 = pl.multiple_of(j * ch, ch)
        dst[pl.ds(r0, ch), :] = src[pl.ds(r0, ch), :].astype(dst.dtype)
        return c
    lax.fori_loop(0, rows // ch, body, 0, unroll=4)


def _expert_kernel(ob_ref, act_ref, na_ref, nu_ref, x_ref, w1_hbm, w3_hbm, w2_hbm, y_ref,
                   w1f, w3f, w2f, w1b, w3b, w2b, wsem):
    b = pl.program_id(0)
    n_used = nu_ref[0]
    n_act = na_ref[0]
    bm = MOE_BLOCK
    hw = w1b.shape[0] // 2

    def fetch(i, wait):
        e = act_ref[i]
        slot = i % 2
        for src, dst in ((w1_hbm, w1f), (w3_hbm, w3f), (w2_hbm, w2f)):
            cp = pltpu.make_async_copy(src.at[e], dst.at[slot], wsem.at[slot])
            if wait:
                cp.wait()
            else:
                cp.start(priority=1)

    @pl.when(b < n_used)
    def _():
        i = ob_ref[b]
        first = jnp.logical_or(b == 0, i != ob_ref[jnp.maximum(b - 1, 0)])

        @pl.when(b == 0)
        def _():
            fetch(0, False)

            @pl.when(n_act > 1)
            def _():
                fetch(1, False)

        @pl.when(first)
        def _():
            fetch(i, True)
            slot = i % 2
            for src, dst in ((w1f, w1b), (w3f, w3b), (w2f, w2b)):
                _cast_rows(src.at[slot], dst)

            @pl.when(i + 2 < n_act)
            def _():
                fetch(i + 2, False)

        lo, hi = _unpack_pairs(_load_token_tiles(x_ref, bm))
        lo = lo.astype(BF16)
        hi = hi.astype(BF16)
        h1 = (jnp.dot(lo, w1b[0:hw, :], preferred_element_type=F32)
              + jnp.dot(hi, w1b[hw:, :], preferred_element_type=F32))
        h3 = (jnp.dot(lo, w3b[0:hw, :], preferred_element_type=F32)
              + jnp.dot(hi, w3b[hw:, :], preferred_element_type=F32))
        hid = (h1 * jax.nn.sigmoid(h1) * h3).astype(BF16)
        _store_token_tiles(y_ref, _pack_pairs(jnp.dot(hid, w2b[...], preferred_element_type=F32)))

    @pl.when(b >= n_used)
    def _():
        y_ref[...] = jnp.zeros_like(y_ref)


def _experts(xs, blk_ord, active, n_active, n_used, w1, w3, w2):
    nb = blk_ord.shape[0]
    e, d, fdim = w1.shape
    rows = MOE_BLOCK * (d // 2 // LANES)
    grid_spec = pltpu.PrefetchScalarGridSpec(
        num_scalar_prefetch=4,
        grid=(nb,),
        in_specs=[pl.BlockSpec((rows, LANES), lambda b, ob, ac, na, nu: (jnp.minimum(b, nu[0] - 1), 0)),
                  pl.BlockSpec(memory_space=pl.ANY),
                  pl.BlockSpec(memory_space=pl.ANY),
                  pl.BlockSpec(memory_space=pl.ANY)],
        out_specs=pl.BlockSpec((rows, LANES), lambda b, ob, ac, na, nu: (b, 0)),
        scratch_shapes=[pltpu.VMEM((2, d, fdim), F32), pltpu.VMEM((2, d, fdim), F32), pltpu.VMEM((2, fdim, d), F32),
                        pltpu.VMEM((d, fdim), BF16), pltpu.VMEM((d, fdim), BF16), pltpu.VMEM((fdim, d), BF16),
                        pltpu.SemaphoreType.DMA((2,))],
    )
    return pl.pallas_call(
        _expert_kernel,
        grid_spec=grid_spec,
        out_shape=jax.ShapeDtypeStruct((nb * rows, LANES), U32),
        compiler_params=_cparams(("arbitrary",)),
    )(blk_ord, active, n_active, n_used, xs, w1, w3, w2)


def _combine_kernel(dcur_ref, dnxt_ref, wt_ref, f_ref, x1_ref, g2_ref, pw_ref, s1_ref, s3_ref, s2_ref, ys_hbm,
                    o_ref, ybuf, sem):
    i = pl.program_id(0)
    tm = x1_ref.shape[0]
    hw = x1_ref.shape[1] // 2
    st = f_ref.shape[0] // tm
    slot = i % 2

    def row_copy(src, k, r, s):
        return pltpu.make_async_copy(ys_hbm.at[pl.ds(pl.multiple_of(src * st, st), st), :],
                                     ybuf.at[s, k, pl.ds(pl.multiple_of(r * st, st), st), :], sem.at[s])

    def start_gather(d_ref, s):
        def body(j, c):
            base = j * DMA_GROUP
            srcs = [[d_ref[k, base + u] for k in range(TOP_K)] for u in range(DMA_GROUP)]
            for u in range(DMA_GROUP):
                for k in range(TOP_K):
                    row_copy(srcs[u][k], k, base + u, s).start(priority=k % 2)
            return c
        lax.fori_loop(0, tm // DMA_GROUP, body, 0)

    def wait_gather(s):
        def body(r, c):
            for k in range(TOP_K):
                row_copy(0, k, r, s).wait()
            return c
        lax.fori_loop(0, tm, body, 0, unroll=2)

    @pl.when(i == 0)
    def _():
        start_gather(dcur_ref, 0)

    @pl.when(i + 1 < pl.num_programs(0))
    def _():
        start_gather(dnxt_ref, 1 - slot)

    lo, hi = _unpack_pairs(_load_token_tiles(f_ref, tm))
    lo = lo.astype(BF16)
    hi = hi.astype(BF16)
    h1 = (jnp.dot(lo, s1_ref[0:hw, :], preferred_element_type=F32)
          + jnp.dot(hi, s1_ref[hw:, :], preferred_element_type=F32))
    h3 = (jnp.dot(lo, s3_ref[0:hw, :], preferred_element_type=F32)
          + jnp.dot(hi, s3_ref[hw:, :], preferred_element_type=F32))
    shared = jnp.dot((h1 * jax.nn.sigmoid(h1) * h3).astype(BF16), s2_ref[...], preferred_element_type=F32)

    wait_gather(slot)
    wt = wt_ref[...]
    r_lo = shared[:, :hw]
    r_hi = shared[:, hw:]
    for k in range(TOP_K):
        a, c = _unpack_pairs(_load_token_tiles(ybuf.at[slot, k], tm))
        w = wt[:, k:k + 1]
        r_lo = r_lo + a * w
        r_hi = r_hi + c * w
    moe = jnp.concatenate([r_lo, r_hi], axis=1)
    o_ref[...] = x1_ref[...] + g2_ref[0] * (_rms(moe) * pw_ref[...])


def _combine(ys, dest, top_wt, f_packed, x1, g2, post_w, s1, s3, s2, rows_per_mod):
    m, d = x1.shape
    st = d // 2 // LANES
    fdim = s1.shape[1]
    tm = _tile(rows_per_mod, 256)
    per = rows_per_mod // tm
    g = g2.shape[0]
    nblk = m // tm
    smem = functools.partial(pl.BlockSpec, memory_space=pltpu.SMEM)
    return pl.pallas_call(
        _combine_kernel,
        grid=(nblk,),
        in_specs=[smem((TOP_K, tm), lambda i: (0, i)),
                  smem((TOP_K, tm), lambda i: (0, jnp.minimum(i + 1, nblk - 1))),
                  pl.BlockSpec((tm, TOP_K), lambda i: (i, 0)),
                  pl.BlockSpec((tm * st, LANES), lambda i: (i, 0)),
                  pl.BlockSpec((tm, d), lambda i: (i, 0)),
                  pl.BlockSpec((1, 1, d), lambda i: (i // per, 0, 0)),
                  pl.BlockSpec((1, d), lambda i: (0, 0)),
                  pl.BlockSpec((d, fdim), lambda i: (0, 0)),
                  pl.BlockSpec((d, fdim), lambda i: (0, 0)),
                  pl.BlockSpec((fdim, d), lambda i: (0, 0)),
                  pl.BlockSpec(memory_space=pl.ANY)],
        out_specs=pl.BlockSpec((tm, d), lambda i: (i, 0)),
        out_shape=jax.ShapeDtypeStruct((m, d), F32),
        scratch_shapes=[pltpu.VMEM((2, TOP_K, tm * st, LANES), U32), pltpu.SemaphoreType.DMA((2,))],
        compiler_params=_cparams(("arbitrary",)),
    )(dest, dest, top_wt, f_packed, x1, g2.reshape(g, 1, d), post_w.reshape(1, d), s1, s3, s2, ys)


def _layer(x, c, ctx, c_ctx, ada_w, ada_b, mix_pre_norm, mix_post_norm, ffn_pre_norm, ffn_post_norm,
           w_in, ret_decay_fwd, ret_decay_bwd, attn_q_norm, attn_k_norm, w_ret_o, w_attn_o, w_out,
           router_w, router_b, exp_w1, exp_w3, exp_w2, shared_w1, shared_w3, shared_w2):
    b, t, d = x.shape
    lc = ctx.shape[1]
    n = b * t
    rq_w, rv_w = RET_HEADS * RET_QK_DIM, RET_HEADS * RET_V_DIM
    aq_w, akv_w = ATTN_HEADS * ATTN_HEAD_DIM, ATTN_KV_HEADS * ATTN_HEAD_DIM
    names = ("rq", "rk", "rv", "rg", "aq", "ak", "av", "gr", "ga")
    widths = (rq_w, rq_w, rv_w, rv_w, aq_w, akv_w, akv_w, d, d)
    offs, o = {}, 0
    for nm, wd in zip(names, widths):
        offs[nm] = o
        o += wd

    rows = -(-(b + 1) // 16) * 16
    cond = jnp.zeros((rows, d), F32).at[:b].set(c).at[b].set(c_ctx)
    mod = _adaln(cond, ada_w, ada_b)
    sh1, sc1, g1, sh2, sc2, g2 = [mod[:b, i * d:(i + 1) * d] for i in range(6)]
    ch1, cs1 = mod[b:b + 1, 0:d], mod[b:b + 1, d:2 * d]

    x2 = x.reshape(n, d)
    h_x = _norm_mod(x2, mix_pre_norm, sc1, sh1, t)
    h_c = _norm_mod(ctx.reshape(b * lc, d), mix_pre_norm, cs1, ch1, b * lc)

    tn = _tile(w_in.shape[1], 1024)
    p = _project(h_x, w_in, list(range(w_in.shape[1] // tn)), tn, BF16)
    cnames = ("rv", "rk", "ak", "av")
    tnc = functools.reduce(math.gcd, [offs[nm] for nm in cnames] + [widths[names.index(nm)] for nm in cnames], tn)
    coffs, ctiles = {}, []
    for nm in cnames:
        coffs[nm] = len(ctiles) * tnc
        ctiles += list(range(offs[nm] // tnc, (offs[nm] + widths[names.index(nm)]) // tnc))
    pc = _project(h_c, w_in, ctiles, tnc, BF16)

    lg = jnp.stack([jax.nn.log_sigmoid(ret_decay_fwd.astype(F32)),
                    jax.nn.log_sigmoid(ret_decay_bwd.astype(F32))])
    rc, rs_lo, rs_hi = _rope_tables(t, RET_QK_DIM)
    retg = _retention(p, pc, lg, rc, rs_lo + rs_hi, b, t, lc, offs, coffs)
    att = _attention(p, pc, attn_q_norm, attn_k_norm, _rope_tables(t, ATTN_HEAD_DIM), b, t, lc, offs, coffs)

    z = _merge(retg, att, p, w_ret_o.astype(BF16), w_attn_o.astype(BF16), offs)
    x1, f_packed, logits_t = _post(z, w_out.astype(BF16), x2, g1, mix_post_norm, ffn_pre_norm, sc2, sh2,
                                   router_w.T, t)

    top_idx, top_w, pos, cnt = _route(logits_t, router_b)

    bm = MOE_BLOCK
    e = router_w.shape[1]
    nb = (n * TOP_K) // bm + e
    counts = cnt[:, 0]
    padded = (counts + bm - 1) // bm * bm
    pad_end = jnp.cumsum(padded)
    pad_off = pad_end - padded
    blk_start = jnp.arange(nb, dtype=I32) * bm
    blk_expert = jnp.minimum(jnp.sum((pad_end[None, :] <= blk_start[:, None]).astype(I32), axis=1), e - 1)
    n_used = (pad_end[-1] // bm).astype(I32).reshape(1)
    is_act = (counts > 0).astype(I32)
    exp_ord = jnp.cumsum(is_act) - is_act
    n_active = jnp.sum(is_act).astype(I32).reshape(1)
    eids = jnp.arange(e, dtype=I32)
    active = jnp.sum(jnp.where((exp_ord[None, :] == eids[:, None]) & (is_act[None, :] > 0), eids[None, :], 0),
                     axis=1).astype(I32)
    blk_ord = exp_ord[blk_expert].astype(I32)

    dest = _dest(top_idx, pos, pad_off)
    xs = _dispatch(counts, pad_off, n_used, dest, f_packed, nb * bm)
    ys = _experts(xs, blk_ord, active, n_active, n_used, exp_w1, exp_w3, exp_w2)
    out = _combine(ys, dest, top_w.T, f_packed, x1, g2, ffn_post_norm, shared_w1.astype(BF16),
                   shared_w3.astype(BF16), shared_w2.astype(BF16), t)
    return out.reshape(b, t, d)


def kernel(x, c, ctx, c_ctx, ada_w, ada_b, mix_pre_norm, mix_post_norm, ffn_pre_norm, ffn_post_norm,
           w_in, ret_decay_fwd, ret_decay_bwd, attn_q_norm, attn_k_norm, w_ret_o, w_attn_o, w_out,
           router_w, router_b, exp_w1, exp_w3, exp_w2, shared_w1, shared_w3, shared_w2):
    assert ada_w.shape[0] == 1, "single-layer operation"
    return _layer(x, c, ctx, c_ctx, ada_w[0], ada_b[0], mix_pre_norm[0], mix_post_norm[0], ffn_pre_norm[0],
                  ffn_post_norm[0], w_in[0], ret_decay_fwd[0], ret_decay_bwd[0], attn_q_norm[0],
                  attn_k_norm[0], w_ret_o[0], w_attn_o[0], w_out[0], router_w[0], router_b[0],
                  exp_w1[0], exp_w3[0], exp_w2[0], shared_w1[0], shared_w3[0], shared_w2[0])
```

```python
import functools
import math

import jax
import jax.numpy as jnp
import numpy as np
from jax import lax
from jax.experimental import pallas as pl
from jax.experimental.pallas import tpu as pltpu

F32 = jnp.float32
BF16 = jnp.bfloat16
U32 = jnp.uint32
I32 = jnp.int32

GRID_W = 64
NORM_EPS = 1e-6
ROPE_THETA = 10000.0
RET_HEADS = 4
RET_QK_DIM = 256
RET_V_DIM = 512
ATTN_HEADS = 16
ATTN_KV_HEADS = 4
ATTN_HEAD_DIM = 128
N_EXPERTS = 256
TOP_K = 8
N_GROUPS = 8
TOPK_GROUPS = 4
EXPERT_DIM = 512
ROUTED_SCALE = 2.5
LOG2E = 1.4426950408889634

V7X_VMEM_BYTES = 64 * 1024 * 1024
VMEM_LIMIT = V7X_VMEM_BYTES - 12 * 1024 * 1024
VMEM_SLACK = 4 * 1024 * 1024
LANES = 128
SUBLANES = 8

RET_CHUNK = 256
MOE_BLOCK = 256
DMA_GROUP = 2
DMA_LAG = 32
CAST_ROWS = 256
POST_SUB = 256
W_BUFS = 3


def _cparams(sem, vmem_limit=VMEM_LIMIT):
    return pltpu.CompilerParams(dimension_semantics=sem, vmem_limit_bytes=vmem_limit)


def _tile(dim, pref):
    if dim <= pref:
        return dim
    for t in range(pref, 0, -LANES):
        if dim % t == 0:
            return t
    raise ValueError((dim, pref))


def _blk(off, width):
    assert off % width == 0, (off, width)
    return off // width


def _ada_kernel(s_ref, w_ref, b_ref, o_ref):
    s = s_ref[...]
    s = (s * jax.nn.sigmoid(s)).astype(BF16)
    o_ref[...] = jnp.dot(s, w_ref[...].astype(BF16), preferred_element_type=F32) + b_ref[...]


def _adaln(cond, w, b):
    r, d = cond.shape
    n = w.shape[1]
    tn = _tile(n, 1024)
    return pl.pallas_call(
        _ada_kernel,
        grid=(n // tn,),
        in_specs=[pl.BlockSpec((r, d), lambda j: (0, 0)),
                  pl.BlockSpec((d, tn), lambda j: (0, j)),
                  pl.BlockSpec((1, tn), lambda j: (0, j))],
        out_specs=pl.BlockSpec((r, tn), lambda j: (0, j)),
        out_shape=jax.ShapeDtypeStruct((r, n), F32),
        compiler_params=_cparams(("arbitrary",)),
    )(cond, w, b.reshape(1, n))


def _rms(x):
    return x * lax.rsqrt(jnp.mean(x * x, axis=-1, keepdims=True) + NORM_EPS)


def _norm_mod_kernel(x_ref, w_ref, sc_ref, sh_ref, o_ref):
    y = _rms(x_ref[...]) * w_ref[...]
    o_ref[...] = (y * (1.0 + sc_ref[0]) + sh_ref[0]).astype(o_ref.dtype)


def _norm_mod(x2, w, sc, sh, rows_per_mod):
    m, d = x2.shape
    tm = _tile(rows_per_mod, 512)
    per = rows_per_mod // tm
    g = sc.shape[0]
    return pl.pallas_call(
        _norm_mod_kernel,
        grid=(m // tm,),
        in_specs=[pl.BlockSpec((tm, d), lambda i: (i, 0)),
                  pl.BlockSpec((1, d), lambda i: (0, 0)),
                  pl.BlockSpec((1, 1, d), lambda i: (i // per, 0, 0)),
                  pl.BlockSpec((1, 1, d), lambda i: (i // per, 0, 0))],
        out_specs=pl.BlockSpec((tm, d), lambda i: (i, 0)),
        out_shape=jax.ShapeDtypeStruct((m, d), BF16),
        compiler_params=_cparams(("arbitrary",)),
    )(x2, w.reshape(1, d), sc.reshape(g, 1, d), sh.reshape(g, 1, d))


def _proj_kernel(cols_ref, a_ref, w_ref, o_ref, wb):
    @pl.when(pl.program_id(1) == 0)
    def _():
        _cast_rows(w_ref, wb)

    o_ref[...] = jnp.dot(a_ref[...], wb[...], preferred_element_type=F32).astype(o_ref.dtype)


def _project(a, w, col_tiles, tn, out_dtype):
    m, k = a.shape
    tm = _tile(m, 1024)
    nj = len(col_tiles)
    grid_spec = pltpu.PrefetchScalarGridSpec(
        num_scalar_prefetch=1,
        grid=(nj, m // tm),
        in_specs=[pl.BlockSpec((tm, k), lambda j, i, c: (i, 0)),
                  pl.BlockSpec((k, tn), lambda j, i, c: (0, c[j]))],
        out_specs=pl.BlockSpec((tm, tn), lambda j, i, c: (i, j)),
        scratch_shapes=[pltpu.VMEM((k, tn), BF16)],
    )
    return pl.pallas_call(
        _proj_kernel,
        grid_spec=grid_spec,
        out_shape=jax.ShapeDtypeStruct((m, nj * tn), out_dtype),
        compiler_params=_cparams(("arbitrary", "arbitrary")),
    )(jnp.asarray(col_tiles, I32), a, w)


def _rope_tables(t, dim):
    half = dim // 2
    pos = jnp.arange(t, dtype=I32)
    row = (pos // GRID_W).astype(F32)
    col = (pos % GRID_W).astype(F32)
    inv = ROPE_THETA ** (-jnp.arange(0, half, 2, dtype=F32) / half)
    ar = row[:, None] * inv[None, :]
    ac = col[:, None] * inv[None, :]
    cr, sr, cc, sc = jnp.cos(ar), jnp.sin(ar), jnp.cos(ac), jnp.sin(ac)
    z = jnp.zeros_like(sr)
    c = jnp.concatenate([cr, cr, cc, cc], axis=1)
    s_lo = jnp.concatenate([-sr, z, -sc, z], axis=1)
    s_hi = jnp.concatenate([z, sr, z, sc], axis=1)
    return c, s_lo, s_hi


def _rope128(x, c, s_lo, s_hi):
    q4 = ATTN_HEAD_DIM // 4
    return (x * c + pltpu.roll(x, LANES - q4, axis=1) * s_lo + pltpu.roll(x, q4, axis=1) * s_hi)


def _rope256(x, c, s):
    parts = [pltpu.roll(x[:, i * LANES:(i + 1) * LANES], LANES // 2, axis=1)
             for i in range(RET_QK_DIM // LANES)]
    return x * c + jnp.concatenate(parts, axis=1) * s


def _ret_kernel(lg_ref, q_ref, k_ref, v_ref, g_ref, kc_ref, vc_ref, c_ref, s_ref, o_ref,
                qs, ks, acc, st):
    h = pl.program_id(1)
    lgf = lg_ref[0, h]
    lgb = lg_ref[1, h]
    t = q_ref.shape[0]
    ch = RET_CHUNK if t % RET_CHUNK == 0 else t
    nch = t // ch
    lc = kc_ref.shape[0]
    k_scale = RET_QK_DIM ** -0.5
    tn_dims = (((0,), (0,)), ((), ()))

    ii = lax.broadcasted_iota(I32, (ch, ch), 0)
    jj = lax.broadcasted_iota(I32, (ch, ch), 1)
    dif = (ii - jj).astype(F32)
    dmat = (jnp.where(dif >= 0, jnp.exp(lgf * jnp.maximum(dif, 0.0)), 0.0)
            + jnp.where(dif <= 0, jnp.exp(lgb * jnp.maximum(-dif, 0.0)), 0.0))
    pos = lax.broadcasted_iota(I32, (ch, 1), 0).astype(F32)
    qf_dec = jnp.exp(lgf * (pos + 1.0))
    kf_dec = jnp.exp(lgf * (ch - 1.0 - pos))
    qb_dec = jnp.exp(lgb * (ch - pos))
    kb_dec = jnp.exp(lgb * pos)
    chv = jnp.full((1, 1), float(ch), F32)
    cf_dec = jnp.exp(lgf * chv)
    cb_dec = jnp.exp(lgb * chv)

    cpos = lax.broadcasted_iota(I32, (lc, 1), 0).astype(F32)
    kc = kc_ref[...].astype(F32) * k_scale
    vc = vc_ref[...]
    st[0] = lax.dot_general((kc * jnp.exp(lgf * (lc - 1.0 - cpos))).astype(BF16), vc, tn_dims,
                            preferred_element_type=F32)
    st[1] = lax.dot_general((kc * jnp.exp(lgb * cpos)).astype(BF16), vc, tn_dims,
                            preferred_element_type=F32)

    def fwd(c, carry):
        r0 = pl.multiple_of(c * ch, ch)
        cs = c_ref[pl.ds(r0, ch), :]
        sn = s_ref[pl.ds(r0, ch), :]
        q = _rope256(q_ref[pl.ds(r0, ch), :].astype(F32), cs, sn)
        k = _rope256(k_ref[pl.ds(r0, ch), :].astype(F32), cs, sn) * k_scale
        v = v_ref[pl.ds(r0, ch), :]
        qb = q.astype(BF16)
        kb = k.astype(BF16)
        qs[pl.ds(r0, ch), :] = q
        ks[pl.ds(r0, ch), :] = k
        sc = lax.dot_general(qb, kb, (((1,), (1,)), ((), ())), preferred_element_type=F32) * dmat
        o = jnp.dot(sc.astype(BF16), v, preferred_element_type=F32)
        o += jnp.dot((q * qf_dec).astype(BF16), st[0].astype(BF16), preferred_element_type=F32)
        acc[pl.ds(r0, ch), :] = o
        st[0] = st[0] * cf_dec + lax.dot_general((k * kf_dec).astype(BF16), v, tn_dims,
                                                 preferred_element_type=F32)
        return carry

    lax.fori_loop(0, nch, fwd, 0, unroll=2)

    def bwd(i, carry):
        c = nch - 1 - i
        r0 = pl.multiple_of(c * ch, ch)
        q = qs[pl.ds(r0, ch), :]
        k = ks[pl.ds(r0, ch), :]
        v = v_ref[pl.ds(r0, ch), :]
        o = jnp.dot((q * qb_dec).astype(BF16), st[1].astype(BF16), preferred_element_type=F32)
        tot = acc[pl.ds(r0, ch), :] + o
        gate = g_ref[pl.ds(r0, ch), :].astype(F32)
        o_ref[pl.ds(r0, ch), :] = (gate * jax.nn.sigmoid(gate) * _rms(tot)).astype(o_ref.dtype)
        st[1] = st[1] * cb_dec + lax.dot_general((k * kb_dec).astype(BF16), v, tn_dims,
                                                 preferred_element_type=F32)
        return carry

    lax.fori_loop(0, nch, bwd, 0, unroll=2)


def _retention(p, pc, lg, tab_c, tab_s, b, t, lc, offs, coffs):
    h, dk, dv = RET_HEADS, RET_QK_DIM, RET_V_DIM
    qo, ko, vo, go = (_blk(offs["rq"], dk), _blk(offs["rk"], dk), _blk(offs["rv"], dv), _blk(offs["rg"], dv))
    kco, vco = _blk(coffs["rk"], dk), _blk(coffs["rv"], dv)
    grid_spec = pltpu.PrefetchScalarGridSpec(
        num_scalar_prefetch=1,
        grid=(b, h),
        in_specs=[pl.BlockSpec((t, dk), lambda bi, hi, lg: (bi, qo + hi)),
                  pl.BlockSpec((t, dk), lambda bi, hi, lg: (bi, ko + hi)),
                  pl.BlockSpec((t, dv), lambda bi, hi, lg: (bi, vo + hi)),
                  pl.BlockSpec((t, dv), lambda bi, hi, lg: (bi, go + hi)),
                  pl.BlockSpec((lc, dk), lambda bi, hi, lg: (bi, kco + hi)),
                  pl.BlockSpec((lc, dv), lambda bi, hi, lg: (bi, vco + hi)),
                  pl.BlockSpec((t, dk), lambda bi, hi, lg: (0, 0)),
                  pl.BlockSpec((t, dk), lambda bi, hi, lg: (0, 0))],
        out_specs=pl.BlockSpec((t, dv), lambda bi, hi, lg: (bi, hi)),
        scratch_shapes=[pltpu.VMEM((t, dk), F32), pltpu.VMEM((t, dk), F32),
                        pltpu.VMEM((t, dv), F32), pltpu.VMEM((2, dk, dv), F32)],
    )
    return pl.pallas_call(
        _ret_kernel,
        grid_spec=grid_spec,
        out_shape=jax.ShapeDtypeStruct((b * t, h * dv), BF16),
        compiler_params=_cparams(("arbitrary", "arbitrary")),
    )(lg, p, p, p, p, pc, pc, tab_c, tab_s)


def _attn_kernel(q_ref, kx_ref, vx_ref, kc_ref, vc_ref, qn_ref, kn_ref, c_ref, sl_ref, sh_ref, o_ref,
                 k_s, v_s):
    qt = pl.program_id(2)
    tq = q_ref.shape[0]
    lc = kc_ref.shape[0]
    hd = ATTN_HEAD_DIM
    scale = hd ** -0.5 * LOG2E

    @pl.when(qt == 0)
    def _():
        kn = kn_ref[...]
        k_s[0:lc, :] = (_rms(kc_ref[...].astype(F32)) * kn).astype(BF16)
        kx = _rms(kx_ref[...].astype(F32)) * kn
        k_s[lc:, :] = _rope128(kx, c_ref[...], sl_ref[...], sh_ref[...]).astype(BF16)
        v_s[0:lc, 0:hd] = vc_ref[...]
        v_s[lc:, 0:hd] = vx_ref[...]
        col = lax.broadcasted_iota(I32, (v_s.shape[0], hd), 1)
        v_s[:, hd:] = jnp.where(col == 0, 1.0, 0.0).astype(BF16)

    r0 = pl.multiple_of(qt * tq, tq)
    c = c_ref[pl.ds(r0, tq), :]
    sl = sl_ref[pl.ds(r0, tq), :]
    sh = sh_ref[pl.ds(r0, tq), :]
    qn = qn_ref[...]
    kk = k_s[...]
    vv = v_s[...]
    for g in range(ATTN_HEADS // ATTN_KV_HEADS):
        q = _rms(q_ref[:, g * hd:(g + 1) * hd].astype(F32)) * qn
        q = (_rope128(q, c, sl, sh) * scale).astype(BF16)
        s = lax.dot_general(q, kk, (((1,), (1,)), ((), ())), preferred_element_type=F32)
        m = jnp.max(s, axis=-1, keepdims=True)
        p = jnp.exp2(s - m).astype(BF16)
        ov = jnp.dot(p, vv, preferred_element_type=F32)
        o = ov[:, 0:hd] / ov[:, hd:hd + 1]
        o_ref[:, g * hd:(g + 1) * hd] = o.astype(o_ref.dtype)


def _attention(p, pc, qn, kn, tabs, b, t, lc, offs, coffs):
    hd, kvh = ATTN_HEAD_DIM, ATTN_KV_HEADS
    gw = (ATTN_HEADS // kvh) * hd
    tq = _tile(t, 256)
    qo, ko, vo = _blk(offs["aq"], gw), _blk(offs["ak"], hd), _blk(offs["av"], hd)
    kco, vco = _blk(coffs["ak"], hd), _blk(coffs["av"], hd)
    tab = pl.BlockSpec((t, hd), lambda bi, ki, qi: (0, 0))
    return pl.pallas_call(
        _attn_kernel,
        grid=(b, kvh, t // tq),
        in_specs=[pl.BlockSpec((tq, gw), lambda bi, ki, qi: (bi * (t // tq) + qi, qo + ki)),
                  pl.BlockSpec((t, hd), lambda bi, ki, qi: (bi, ko + ki)),
                  pl.BlockSpec((t, hd), lambda bi, ki, qi: (bi, vo + ki)),
                  pl.BlockSpec((lc, hd), lambda bi, ki, qi: (bi, kco + ki)),
                  pl.BlockSpec((lc, hd), lambda bi, ki, qi: (bi, vco + ki)),
                  pl.BlockSpec((1, hd), lambda bi, ki, qi: (0, 0)),
                  pl.BlockSpec((1, hd), lambda bi, ki, qi: (0, 0)),
                  tab, tab, tab],
        out_specs=pl.BlockSpec((tq, gw), lambda bi, ki, qi: (bi * (t // tq) + qi, ki)),
        out_shape=jax.ShapeDtypeStruct((b * t, ATTN_HEADS * hd), BF16),
        scratch_shapes=[pltpu.VMEM((lc + t, hd), BF16), pltpu.VMEM((lc + t, 2 * hd), BF16)],
        compiler_params=_cparams(("arbitrary", "arbitrary", "arbitrary")),
    )(p, p, p, pc, pc, qn.reshape(1, hd), kn.reshape(1, hd), *tabs)


def _merge_kernel(r_ref, a_ref, gr_ref, ga_ref, wr_ref, wa_ref, o_ref):
    ret = jnp.dot(r_ref[...], wr_ref[...], preferred_element_type=F32)
    att = jnp.dot(a_ref[...], wa_ref[...], preferred_element_type=F32)
    z = (jax.nn.sigmoid(gr_ref[...].astype(F32)) * ret + jax.nn.sigmoid(ga_ref[...].astype(F32)) * att)
    o_ref[...] = z.astype(o_ref.dtype)


def _merge(retg, att, p, w_ret_o, w_attn_o, offs):
    m, kr = retg.shape
    ka = att.shape[1]
    d = w_ret_o.shape[1]
    tm, tn = _tile(m, 1024), _tile(d, 256 if d < 2048 else 512)
    gro, gao = _blk(offs["gr"], tn), _blk(offs["ga"], tn)
    return pl.pallas_call(
        _merge_kernel,
        grid=(m // tm, d // tn),
        in_specs=[pl.BlockSpec((tm, kr), lambda i, j: (i, 0)),
                  pl.BlockSpec((tm, ka), lambda i, j: (i, 0)),
                  pl.BlockSpec((tm, tn), lambda i, j: (i, gro + j)),
                  pl.BlockSpec((tm, tn), lambda i, j: (i, gao + j)),
                  pl.BlockSpec((kr, tn), lambda i, j: (0, j)),
                  pl.BlockSpec((ka, tn), lambda i, j: (0, j))],
        out_specs=pl.BlockSpec((tm, tn), lambda i, j: (i, j)),
        out_shape=jax.ShapeDtypeStruct((m, d), BF16),
        compiler_params=_cparams(("arbitrary", "arbitrary")),
    )(retg, att, p, p, w_ret_o, w_attn_o)


def _pack_pairs(x):
    hw = x.shape[1] // 2
    lo = pltpu.bitcast(x[:, :hw].astype(BF16).astype(F32), U32)
    hi = pltpu.bitcast(x[:, hw:].astype(BF16).astype(F32), U32)
    return (hi & jnp.uint32(0xFFFF0000)) | (lo >> 16)


def _store_token_tiles(ref, packed):
    m = packed.shape[0]
    st = packed.shape[1] // LANES
    for s in range(st):
        ref[pl.ds(s, m, stride=st), :] = packed[:, s * LANES:(s + 1) * LANES]


def _load_token_tiles(ref, m):
    st = ref.shape[0] // m
    return jnp.concatenate([ref[pl.ds(s, m, stride=st), :] for s in range(st)], axis=1)


def _unpack_pairs(w):
    lo = pltpu.bitcast(w << 16, F32)
    hi = pltpu.bitcast(w & jnp.uint32(0xFFFF0000), F32)
    return lo, hi


def _post_kernel(z_ref, wo_ref, x_ref, g1_ref, pw_ref, fw_ref, sc_ref, sh_ref, rwh_ref, rwl_ref,
                 x1_ref, f_ref, lg_ref):
    tm = x_ref.shape[0]
    sub = min(tm, POST_SUB)
    st = f_ref.shape[0] // tm
    nt = (((1,), (1,)), ((), ()))
    for h in range(tm // sub):
        r0 = h * sub
        y = jnp.dot(z_ref[r0:r0 + sub, :], wo_ref[...], preferred_element_type=F32)
        x1 = x_ref[r0:r0 + sub, :] + g1_ref[0] * (_rms(y) * pw_ref[...])
        x1_ref[r0:r0 + sub, :] = x1
        f = (_rms(x1) * fw_ref[...]) * (1.0 + sc_ref[0]) + sh_ref[0]
        _store_token_tiles(f_ref.at[r0 * st:(r0 + sub) * st, :], _pack_pairs(f))
        fh = f.astype(BF16)
        fl = (f - fh.astype(F32)).astype(BF16)
        lg_ref[:, r0:r0 + sub] = (lax.dot_general(rwh_ref[...], fh, nt, preferred_element_type=F32)
                                  + lax.dot_general(rwh_ref[...], fl, nt, preferred_element_type=F32)
                                  + lax.dot_general(rwl_ref[...], fh, nt, preferred_element_type=F32))


def _post(z, w_out, x2, g1, post_w, pre_w, sc2, sh2, router_wt, rows_per_mod):
    m, d = x2.shape
    e = router_wt.shape[0]
    st = d // 2 // LANES
    assert st == SUBLANES, "token rows must be whole (8, 128) tiles"
    rw_hi = router_wt.astype(BF16)
    rw_lo = (router_wt - rw_hi.astype(F32)).astype(BF16)
    tm = _tile(rows_per_mod, 2 * POST_SUB)
    per = rows_per_mod // tm
    g = g1.shape[0]
    vec = pl.BlockSpec((1, d), lambda i: (0, 0))
    mod = pl.BlockSpec((1, 1, d), lambda i: (i // per, 0, 0))
    once = pl.Buffered(1)
    return pl.pallas_call(
        _post_kernel,
        grid=(m // tm,),
        in_specs=[pl.BlockSpec((tm, d), lambda i: (i, 0)),
                  pl.BlockSpec((d, d), lambda i: (0, 0), pipeline_mode=once),
                  pl.BlockSpec((tm, d), lambda i: (i, 0)),
                  mod, vec, vec, mod, mod,
                  pl.BlockSpec((e, d), lambda i: (0, 0), pipeline_mode=once),
                  pl.BlockSpec((e, d), lambda i: (0, 0), pipeline_mode=once)],
        out_specs=[pl.BlockSpec((tm, d), lambda i: (i, 0)),
                   pl.BlockSpec((tm * st, LANES), lambda i: (i, 0)),
                   pl.BlockSpec((e, tm), lambda i: (0, i))],
        out_shape=[jax.ShapeDtypeStruct((m, d), F32),
                   jax.ShapeDtypeStruct((m * st, LANES), U32),
                   jax.ShapeDtypeStruct((e, m), F32)],
        compiler_params=_cparams(("arbitrary",)),
    )(z, w_out, x2, g1.reshape(g, 1, d), post_w.reshape(1, d), pre_w.reshape(1, d),
      sc2.reshape(g, 1, d), sh2.reshape(g, 1, d), rw_hi, rw_lo)


def _route_kernel(lg_ref, b_ref, tri_ref, idx_ref, w_ref, pos_ref, cnt_ref, carry):
    @pl.when(pl.program_id(0) == 0)
    def _():
        carry[...] = jnp.zeros_like(carry)

    e, tn = lg_ref.shape
    gs = e // N_GROUPS
    neg = -jnp.inf
    big = float(e)
    s = jax.nn.sigmoid(lg_ref[...])
    choice = s + b_ref[...]

    row_g = lax.broadcasted_iota(I32, (gs, tn), 0).astype(F32)
    rows = []
    for g in range(N_GROUPS):
        blk = choice[g * gs:(g + 1) * gs]
        m1 = jnp.max(blk, axis=0, keepdims=True)
        i1 = jnp.min(jnp.where(blk == m1, row_g, big), axis=0, keepdims=True)
        m2 = jnp.max(jnp.where(row_g == i1, neg, blk), axis=0, keepdims=True)
        rows.append(m1 + m2)
    gscore = jnp.concatenate(rows, axis=0)

    row_n = lax.broadcasted_iota(I32, (N_GROUPS, tn), 0).astype(F32)
    sel = jnp.zeros((N_GROUPS, tn), F32)
    cur = gscore
    for _ in range(TOPK_GROUPS):
        m = jnp.max(cur, axis=0, keepdims=True)
        gi = jnp.min(jnp.where(cur == m, row_n, big), axis=0, keepdims=True)
        hit = row_n == gi
        sel = jnp.where(hit, 1.0, sel)
        cur = jnp.where(hit, neg, cur)

    masked = jnp.concatenate(
        [jnp.where(sel[g:g + 1] > 0.0, choice[g * gs:(g + 1) * gs], neg) for g in range(N_GROUPS)], axis=0)

    row_e = lax.broadcasted_iota(I32, (e, tn), 0).astype(F32)
    onehot = jnp.zeros((e, tn), F32)
    cur = masked
    idxs, ws = [], []
    for _ in range(TOP_K):
        m = jnp.max(cur, axis=0, keepdims=True)
        ik = jnp.min(jnp.where(cur == m, row_e, big), axis=0, keepdims=True)
        hit = row_e == ik
        ws.append(jnp.sum(jnp.where(hit, s, 0.0), axis=0, keepdims=True))
        cur = jnp.where(hit, neg, cur)
        onehot = jnp.where(hit, 1.0, onehot)
        idxs.append(ik)
    w = jnp.concatenate(ws, axis=0)
    w = w / jnp.sum(w, axis=0, keepdims=True) * ROUTED_SCALE

    prefix = jnp.dot(onehot.astype(BF16), tri_ref[...], preferred_element_type=F32)
    base = prefix + carry[...]
    pos = jnp.concatenate(
        [jnp.sum(jnp.where(row_e == ik, base, 0.0), axis=0, keepdims=True) for ik in idxs], axis=0)
    carry[...] = carry[...] + jnp.sum(onehot, axis=1, keepdims=True)

    idx_ref[...] = jnp.concatenate(idxs, axis=0).astype(I32)
    w_ref[...] = w
    pos_ref[...] = pos.astype(I32)
    cnt_ref[...] = jnp.broadcast_to(carry[...], cnt_ref.shape).astype(I32)


def _route(logits_t, bias):
    e, n = logits_t.shape
    tn = _tile(n, 512)
    tri = (jnp.arange(tn)[:, None] < jnp.arange(tn)[None, :]).astype(BF16)
    kspec = pl.BlockSpec((TOP_K, tn), lambda i: (0, i))
    return pl.pallas_call(
        _route_kernel,
        grid=(n // tn,),
        in_specs=[pl.BlockSpec((e, tn), lambda i: (0, i)),
                  pl.BlockSpec((e, 1), lambda i: (0, 0)),
                  pl.BlockSpec((tn, tn), lambda i: (0, 0))],
        out_specs=[kspec, kspec, kspec, pl.BlockSpec((e, LANES), lambda i: (0, 0))],
        out_shape=[jax.ShapeDtypeStruct((TOP_K, n), I32), jax.ShapeDtypeStruct((TOP_K, n), F32),
                   jax.ShapeDtypeStruct((TOP_K, n), I32), jax.ShapeDtypeStruct((e, LANES), I32)],
        scratch_shapes=[pltpu.VMEM((e, 1), F32)],
        compiler_params=_cparams(("arbitrary",)),
    )(logits_t, bias.reshape(e, 1), tri)


def _dest_kernel(idx_ref, pos_ref, off_ref, o_ref):
    e = off_ref.shape[0]
    tn = idx_ref.shape[1]
    row_e = lax.broadcasted_iota(I32, (e, tn), 0)
    off = off_ref[...]
    idx = idx_ref[...]
    rows = [jnp.sum(jnp.where(row_e == idx[k:k + 1], off, 0.0), axis=0, keepdims=True) for k in range(TOP_K)]
    o_ref[...] = jnp.concatenate(rows, axis=0).astype(I32) + pos_ref[...]


def _dest(top_idx, pos, pad_off):
    k, n = top_idx.shape
    e = pad_off.shape[0]
    tn = _tile(n, 512)
    spec = pl.BlockSpec((k, tn), lambda i: (0, i))
    return pl.pallas_call(
        _dest_kernel,
        grid=(n // tn,),
        in_specs=[spec, spec, pl.BlockSpec((e, 1), lambda i: (0, 0))],
        out_specs=spec,
        out_shape=jax.ShapeDtypeStruct((k, n), I32),
        compiler_params=_cparams(("arbitrary",)),
    )(top_idx, pos, pad_off.astype(F32).reshape(e, 1))


def _dispatch_kernel(cnt_ref, off_ref, nu_ref, dest_ref, f_ref, xs_hbm, zbuf, sem, zsem):
    i = pl.program_id(0)
    tt = dest_ref.shape[1]
    st = SUBLANES
    bm = zbuf.shape[0] // st
    n_exp = cnt_ref.shape[0]
    nb = xs_hbm.shape[0] // (bm * st)

    def zero_fill(wait):
        def fill(start, rows):
            cp = pltpu.make_async_copy(zbuf.at[pl.ds(0, rows * st), :],
                                       xs_hbm.at[pl.ds(pl.multiple_of(start * st, st), rows * st), :], zsem)
            if wait:
                cp.wait()
            else:
                cp.start()

        def per_expert(e, c):
            cnt = cnt_ref[e]
            npad = (-cnt) & (bm - 1)
            start = off_ref[e] + cnt
            p = bm // 2
            while p >= 1:
                @pl.when((npad & p) != 0)
                def _(start=start, p=p):
                    fill(start, p)
                start = start + (npad & p)
                p //= 2
            return c
        lax.fori_loop(0, n_exp, per_expert, 0)

        def per_block(b, c):
            fill(b * bm, bm)
            return c
        lax.fori_loop(nu_ref[0], nb, per_block, 0)

    def row_copy(j, dst):
        return pltpu.make_async_copy(f_ref.at[pl.ds(pl.multiple_of(j * st, st), st), :],
                                     xs_hbm.at[pl.ds(pl.multiple_of(dst * st, st), st), :], sem)

    @pl.when(i == 0)
    def _():
        zbuf[...] = jnp.zeros_like(zbuf)
        zero_fill(False)

    def issue(j, c):
        base = j * DMA_GROUP
        dsts = [[dest_ref[k, base + u] for k in range(TOP_K)] for u in range(DMA_GROUP)]
        for u in range(DMA_GROUP):
            for k in range(TOP_K):
                row_copy(base + u, dsts[u][k]).start(priority=k % 2)

        @pl.when(j >= DMA_LAG)
        def _():
            wait_group()
        return c

    def wait_group():
        for _ in range(DMA_GROUP * TOP_K):
            row_copy(0, 0).wait()

    n_groups = tt // DMA_GROUP
    lax.fori_loop(0, n_groups, issue, 0)

    def drain(j, c):
        wait_group()
        return c
    lax.fori_loop(0, min(DMA_LAG, n_groups), drain, 0)

    @pl.when(i == pl.num_programs(0) - 1)
    def _():
        zero_fill(True)


def _dispatch(counts, pad_off, n_used, dest, f_packed, n_slots):
    st = SUBLANES
    n = f_packed.shape[0] // st
    tt = _tile(n, 512)
    grid_spec = pltpu.PrefetchScalarGridSpec(
        num_scalar_prefetch=3,
        grid=(n // tt,),
        in_specs=[pl.BlockSpec((TOP_K, tt), lambda i, c, o, u: (0, i), memory_space=pltpu.SMEM),
                  pl.BlockSpec((tt * st, LANES), lambda i, c, o, u: (i, 0))],
        out_specs=pl.BlockSpec(memory_space=pl.ANY),
        scratch_shapes=[pltpu.VMEM((MOE_BLOCK * st, LANES), U32), pltpu.SemaphoreType.DMA(()),
                        pltpu.SemaphoreType.DMA(())],
    )
    return pl.pallas_call(
        _dispatch_kernel,
        grid_spec=grid_spec,
        out_shape=jax.ShapeDtypeStruct((n_slots * st, LANES), U32),
        compiler_params=_cparams(("arbitrary",)),
    )(counts, pad_off, n_used, dest, f_packed)


def _cast_rows(src, dst):
    rows = src.shape[0]
    ch = CAST_ROWS * LANES // src.shape[1]

    def body(j, c):
        r0 = pl.multiple_of(j * ch, ch)
        dst[pl.ds(r0, ch), :] = src[pl.ds(r0, ch), :].astype(dst.dtype)
        return c
    lax.fori_loop(0, rows // ch, body, 0, unroll=4)


def _expert_kernel(ob_ref, act_ref, na_ref, nu_ref, x_ref, w1_hbm, w3_hbm, w2_hbm, y_ref,
                   w1f, w3f, w2f, w1b, w3b, w2b, wsem):
    b = pl.program_id(0)
    n_used = nu_ref[0]
    n_act = na_ref[0]
    bm = MOE_BLOCK
    hw = w1b.shape[0] // 2
    ring = w1f.shape[0]

    def fetch(i, wait):
        e = act_ref[i]
        slot = i % ring
        for src, dst in ((w1_hbm, w1f), (w3_hbm, w3f), (w2_hbm, w2f)):
            cp = pltpu.make_async_copy(src.at[e], dst.at[slot], wsem.at[slot])
            if wait:
                cp.wait()
            else:
                cp.start(priority=1)

    @pl.when(b < n_used)
    def _():
        i = ob_ref[b]
        first = jnp.logical_or(b == 0, i != ob_ref[jnp.maximum(b - 1, 0)])

        @pl.when(b == 0)
        def _():
            fetch(0, False)
            for ahead in range(1, ring):
                @pl.when(n_act > ahead)
                def _(ahead=ahead):
                    fetch(ahead, False)

        @pl.when(first)
        def _():
            fetch(i, True)
            slot = i % ring
            for src, dst in ((w1f, w1b), (w3f, w3b), (w2f, w2b)):
                _cast_rows(src.at[slot], dst)

            @pl.when(i + ring < n_act)
            def _():
                fetch(i + ring, False)

        lo, hi = _unpack_pairs(_load_token_tiles(x_ref, bm))
        lo = lo.astype(BF16)
        hi = hi.astype(BF16)
        h1 = (jnp.dot(lo, w1b[0:hw, :], preferred_element_type=F32)
              + jnp.dot(hi, w1b[hw:, :], preferred_element_type=F32))
        h3 = (jnp.dot(lo, w3b[0:hw, :], preferred_element_type=F32)
              + jnp.dot(hi, w3b[hw:, :], preferred_element_type=F32))
        hid = (h1 * jax.nn.sigmoid(h1) * h3).astype(BF16)
        _store_token_tiles(y_ref, _pack_pairs(jnp.dot(hid, w2b[...], preferred_element_type=F32)))


def _experts(xs, blk_ord, active, n_active, n_used, w1, w3, w2):
    nb = blk_ord.shape[0]
    e, d, fdim = w1.shape
    rows = MOE_BLOCK * (d // 2 // LANES)
    w_elems = 3 * d * fdim
    vmem_limit = (W_BUFS * w_elems * 4 + w_elems * 2 + 4 * rows * LANES * 4
                  + MOE_BLOCK * (2 * fdim + 2 * d) * 4 + VMEM_SLACK)
    assert vmem_limit <= V7X_VMEM_BYTES
    grid_spec = pltpu.PrefetchScalarGridSpec(
        num_scalar_prefetch=4,
        grid=(nb,),
        in_specs=[pl.BlockSpec((rows, LANES), lambda b, ob, ac, na, nu: (jnp.minimum(b, nu[0] - 1), 0)),
                  pl.BlockSpec(memory_space=pl.ANY),
                  pl.BlockSpec(memory_space=pl.ANY),
                  pl.BlockSpec(memory_space=pl.ANY)],
        out_specs=pl.BlockSpec((rows, LANES), lambda b, ob, ac, na, nu: (jnp.minimum(b, nu[0] - 1), 0)),
        scratch_shapes=[pltpu.VMEM((W_BUFS, d, fdim), F32), pltpu.VMEM((W_BUFS, d, fdim), F32),
                        pltpu.VMEM((W_BUFS, fdim, d), F32),
                        pltpu.VMEM((d, fdim), BF16), pltpu.VMEM((d, fdim), BF16), pltpu.VMEM((fdim, d), BF16),
                        pltpu.SemaphoreType.DMA((W_BUFS,))],
    )
    return pl.pallas_call(
        _expert_kernel,
        grid_spec=grid_spec,
        out_shape=jax.ShapeDtypeStruct((nb * rows, LANES), U32),
        input_output_aliases={4: 0},
        compiler_params=_cparams(("arbitrary",), vmem_limit),
    )(blk_ord, active, n_active, n_used, xs, w1, w3, w2)


def _combine_kernel(dcur_ref, dnxt_ref, wt_ref, f_ref, x1_ref, g2_ref, pw_ref, s1_ref, s3_ref, s2_ref, ys_hbm,
                    o_ref, ybuf, sem):
    i = pl.program_id(0)
    tm = x1_ref.shape[0]
    hw = x1_ref.shape[1] // 2
    st = f_ref.shape[0] // tm
    slot = i % 2

    def row_copy(src, k, r, s):
        return pltpu.make_async_copy(ys_hbm.at[pl.ds(pl.multiple_of(src * st, st), st), :],
                                     ybuf.at[s, k, pl.ds(pl.multiple_of(r * st, st), st), :], sem.at[s])

    def start_gather(d_ref, s):
        def body(j, c):
            base = j * DMA_GROUP
            srcs = [[d_ref[k, base + u] for k in range(TOP_K)] for u in range(DMA_GROUP)]
            for u in range(DMA_GROUP):
                for k in range(TOP_K):
                    row_copy(srcs[u][k], k, base + u, s).start(priority=k % 2)
            return c
        lax.fori_loop(0, tm // DMA_GROUP, body, 0)

    def wait_gather(s):
        def body(r, c):
            for k in range(TOP_K):
                row_copy(0, k, r, s).wait()
            return c
        lax.fori_loop(0, tm, body, 0, unroll=2)

    @pl.when(i == 0)
    def _():
        start_gather(dcur_ref, 0)

    @pl.when(i + 1 < pl.num_programs(0))
    def _():
        start_gather(dnxt_ref, 1 - slot)

    lo, hi = _unpack_pairs(_load_token_tiles(f_ref, tm))
    lo = lo.astype(BF16)
    hi = hi.astype(BF16)
    h1 = (jnp.dot(lo, s1_ref[0:hw, :], preferred_element_type=F32)
          + jnp.dot(hi, s1_ref[hw:, :], preferred_element_type=F32))
    h3 = (jnp.dot(lo, s3_ref[0:hw, :], preferred_element_type=F32)
          + jnp.dot(hi, s3_ref[hw:, :], preferred_element_type=F32))
    shared = jnp.dot((h1 * jax.nn.sigmoid(h1) * h3).astype(BF16), s2_ref[...], preferred_element_type=F32)

    wait_gather(slot)
    wt = wt_ref[...]
    r_lo = shared[:, :hw]
    r_hi = shared[:, hw:]
    for k in range(TOP_K):
        a, c = _unpack_pairs(_load_token_tiles(ybuf.at[slot, k], tm))
        w = wt[:, k:k + 1]
        r_lo = r_lo + a * w
        r_hi = r_hi + c * w
    moe = jnp.concatenate([r_lo, r_hi], axis=1)
    o_ref[...] = x1_ref[...] + g2_ref[0] * (_rms(moe) * pw_ref[...])


def _combine(ys, dest, top_wt, f_packed, x1, g2, post_w, s1, s3, s2, rows_per_mod):
    m, d = x1.shape
    st = d // 2 // LANES
    fdim = s1.shape[1]
    tm = _tile(rows_per_mod, 256)
    per = rows_per_mod // tm
    g = g2.shape[0]
    nblk = m // tm
    smem = functools.partial(pl.BlockSpec, memory_space=pltpu.SMEM)
    return pl.pallas_call(
        _combine_kernel,
        grid=(nblk,),
        in_specs=[smem((TOP_K, tm), lambda i: (0, i)),
                  smem((TOP_K, tm), lambda i: (0, jnp.minimum(i + 1, nblk - 1))),
                  pl.BlockSpec((tm, TOP_K), lambda i: (i, 0)),
                  pl.BlockSpec((tm * st, LANES), lambda i: (i, 0)),
                  pl.BlockSpec((tm, d), lambda i: (i, 0)),
                  pl.BlockSpec((1, 1, d), lambda i: (i // per, 0, 0)),
                  pl.BlockSpec((1, d), lambda i: (0, 0)),
                  pl.BlockSpec((d, fdim), lambda i: (0, 0)),
                  pl.BlockSpec((d, fdim), lambda i: (0, 0)),
                  pl.BlockSpec((fdim, d), lambda i: (0, 0)),
                  pl.BlockSpec(memory_space=pl.ANY)],
        out_specs=pl.BlockSpec((tm, d), lambda i: (i, 0)),
        out_shape=jax.ShapeDtypeStruct((m, d), F32),
        scratch_shapes=[pltpu.VMEM((2, TOP_K, tm * st, LANES), U32), pltpu.SemaphoreType.DMA((2,))],
        compiler_params=_cparams(("arbitrary",)),
    )(dest, dest, top_wt, f_packed, x1, g2.reshape(g, 1, d), post_w.reshape(1, d), s1, s3, s2, ys)


def _layer(x, c, ctx, c_ctx, ada_w, ada_b, mix_pre_norm, mix_post_norm, ffn_pre_norm, ffn_post_norm,
           w_in, ret_decay_fwd, ret_decay_bwd, attn_q_norm, attn_k_norm, w_ret_o, w_attn_o, w_out,
           router_w, router_b, exp_w1, exp_w3, exp_w2, shared_w1, shared_w3, shared_w2):
    b, t, d = x.shape
    lc = ctx.shape[1]
    n = b * t
    rq_w, rv_w = RET_HEADS * RET_QK_DIM, RET_HEADS * RET_V_DIM
    aq_w, akv_w = ATTN_HEADS * ATTN_HEAD_DIM, ATTN_KV_HEADS * ATTN_HEAD_DIM
    names = ("rq", "rk", "rv", "rg", "aq", "ak", "av", "gr", "ga")
    widths = (rq_w, rq_w, rv_w, rv_w, aq_w, akv_w, akv_w, d, d)
    offs, o = {}, 0
    for nm, wd in zip(names, widths):
        offs[nm] = o
        o += wd

    rows = -(-(b + 1) // 16) * 16
    cond = jnp.zeros((rows, d), F32).at[:b].set(c).at[b].set(c_ctx)
    mod = _adaln(cond, ada_w, ada_b)
    sh1, sc1, g1, sh2, sc2, g2 = [mod[:b, i * d:(i + 1) * d] for i in range(6)]
    ch1, cs1 = mod[b:b + 1, 0:d], mod[b:b + 1, d:2 * d]

    x2 = x.reshape(n, d)
    h_x = _norm_mod(x2, mix_pre_norm, sc1, sh1, t)
    h_c = _norm_mod(ctx.reshape(b * lc, d), mix_pre_norm, cs1, ch1, b * lc)

    tn = _tile(w_in.shape[1], 1024)
    p = _project(h_x, w_in, list(range(w_in.shape[1] // tn)), tn, BF16)
    cnames = ("rv", "rk", "ak", "av")
    tnc = functools.reduce(math.gcd, [offs[nm] for nm in cnames] + [widths[names.index(nm)] for nm in cnames], tn)
    coffs, ctiles = {}, []
    for nm in cnames:
        coffs[nm] = len(ctiles) * tnc
        ctiles += list(range(offs[nm] // tnc, (offs[nm] + widths[names.index(nm)]) // tnc))
    pc = _project(h_c, w_in, ctiles, tnc, BF16)

    lg = jnp.stack([jax.nn.log_sigmoid(ret_decay_fwd.astype(F32)),
                    jax.nn.log_sigmoid(ret_decay_bwd.astype(F32))])
    rc, rs_lo, rs_hi = _rope_tables(t, RET_QK_DIM)
    retg = _retention(p, pc, lg, rc, rs_lo + rs_hi, b, t, lc, offs, coffs)
    att = _attention(p, pc, attn_q_norm, attn_k_norm, _rope_tables(t, ATTN_HEAD_DIM), b, t, lc, offs, coffs)

    z = _merge(retg, att, p, w_ret_o.astype(BF16), w_attn_o.astype(BF16), offs)
    x1, f_packed, logits_t = _post(z, w_out.astype(BF16), x2, g1, mix_post_norm, ffn_pre_norm, sc2, sh2,
                                   router_w.T, t)

    top_idx, top_w, pos, cnt = _route(logits_t, router_b)

    bm = MOE_BLOCK
    e = router_w.shape[1]
    nb = (n * TOP_K) // bm + e
    counts = cnt[:, 0]
    padded = (counts + bm - 1) // bm * bm
    pad_end = jnp.cumsum(padded)
    pad_off = pad_end - padded
    blk_start = jnp.arange(nb, dtype=I32) * bm
    blk_expert = jnp.minimum(jnp.sum((pad_end[None, :] <= blk_start[:, None]).astype(I32), axis=1), e - 1)
    n_used = (pad_end[-1] // bm).astype(I32).reshape(1)
    is_act = (counts > 0).astype(I32)
    exp_ord = jnp.cumsum(is_act) - is_act
    n_active = jnp.sum(is_act).astype(I32).reshape(1)
    eids = jnp.arange(e, dtype=I32)
    active = jnp.sum(jnp.where((exp_ord[None, :] == eids[:, None]) & (is_act[None, :] > 0), eids[None, :], 0),
                     axis=1).astype(I32)
    blk_ord = exp_ord[blk_expert].astype(I32)

    dest = _dest(top_idx, pos, pad_off)
    xs = _dispatch(counts, pad_off, n_used, dest, f_packed, nb * bm)
    ys = _experts(xs, blk_ord, active, n_active, n_used, exp_w1, exp_w3, exp_w2)
    out = _combine(ys, dest, top_w.T, f_packed, x1, g2, ffn_post_norm, shared_w1.astype(BF16),
                   shared_w3.astype(BF16), shared_w2.astype(BF16), t)
    return out.reshape(b, t, d)


def kernel(x, c, ctx, c_ctx, ada_w, ada_b, mix_pre_norm, mix_post_norm, ffn_pre_norm, ffn_post_norm,
           w_in, ret_decay_fwd, ret_decay_bwd, attn_q_norm, attn_k_norm, w_ret_o, w_attn_o, w_out,
           router_w, router_b, exp_w1, exp_w3, exp_w2, shared_w1, shared_w3, shared_w2):
    assert ada_w.shape[0] == 1, "single-layer operation"
    return _layer(x, c, ctx, c_ctx, ada_w[0], ada_b[0], mix_pre_norm[0], mix_post_norm[0], ffn_pre_norm[0],
                  ffn_post_norm[0], w_in[0], ret_decay_fwd[0], ret_decay_bwd[0], attn_q_norm[0],
                  attn_k_norm[0], w_ret_o[0], w_attn_o[0], w_out[0], router_w[0], router_b[0],
                  exp_w1[0], exp_w3[0], exp_w2[0], shared_w1[0], shared_w3[0], shared_w2[0])
```

```python
import functools
import math

import jax
import jax.numpy as jnp
import numpy as np
from jax import lax
from jax.experimental import pallas as pl
from jax.experimental.pallas import tpu as pltpu

F32 = jnp.float32
BF16 = jnp.bfloat16
U32 = jnp.uint32
I32 = jnp.int32

GRID_W = 64
NORM_EPS = 1e-6
ROPE_THETA = 10000.0
RET_HEADS = 4
RET_QK_DIM = 256
RET_V_DIM = 512
ATTN_HEADS = 16
ATTN_KV_HEADS = 4
ATTN_HEAD_DIM = 128
N_EXPERTS = 256
TOP_K = 8
N_GROUPS = 8
TOPK_GROUPS = 4
EXPERT_DIM = 512
ROUTED_SCALE = 2.5
LOG2E = 1.4426950408889634

V7X_VMEM_BYTES = 64 * 1024 * 1024
VMEM_LIMIT = V7X_VMEM_BYTES - 12 * 1024 * 1024
VMEM_SLACK = 4 * 1024 * 1024
LANES = 128
SUBLANES = 8

RET_CHUNK = 256
MOE_BLOCK = 256
DMA_GROUP = 2
DMA_LAG = 32
CAST_ROWS = 256
POST_SUB = 256
ATTN_SUB = 128
W_BUFS = 3


def _cparams(sem, vmem_limit=VMEM_LIMIT):
    return pltpu.CompilerParams(dimension_semantics=sem, vmem_limit_bytes=vmem_limit)


def _tile(dim, pref):
    if dim <= pref:
        return dim
    for t in range(pref, 0, -LANES):
        if dim % t == 0:
            return t
    raise ValueError((dim, pref))


def _blk(off, width):
    assert off % width == 0, (off, width)
    return off // width


def _ada_kernel(s_ref, w_ref, b_ref, o_ref):
    s = s_ref[...]
    s = (s * jax.nn.sigmoid(s)).astype(BF16)
    o_ref[...] = jnp.dot(s, w_ref[...].astype(BF16), preferred_element_type=F32) + b_ref[...]


def _adaln(cond, w, b):
    r, d = cond.shape
    n = w.shape[1]
    tn = _tile(n, 1024)
    return pl.pallas_call(
        _ada_kernel,
        grid=(n // tn,),
        in_specs=[pl.BlockSpec((r, d), lambda j: (0, 0)),
                  pl.BlockSpec((d, tn), lambda j: (0, j)),
                  pl.BlockSpec((1, tn), lambda j: (0, j))],
        out_specs=pl.BlockSpec((r, tn), lambda j: (0, j)),
        out_shape=jax.ShapeDtypeStruct((r, n), F32),
        compiler_params=_cparams(("arbitrary",)),
    )(cond, w, b.reshape(1, n))


def _rms(x):
    return x * lax.rsqrt(jnp.mean(x * x, axis=-1, keepdims=True) + NORM_EPS)


def _norm_mod_kernel(x_ref, w_ref, sc_ref, sh_ref, o_ref):
    y = _rms(x_ref[...]) * w_ref[...]
    o_ref[...] = (y * (1.0 + sc_ref[0]) + sh_ref[0]).astype(o_ref.dtype)


def _norm_mod(x2, w, sc, sh, rows_per_mod):
    m, d = x2.shape
    tm = _tile(rows_per_mod, 512)
    per = rows_per_mod // tm
    g = sc.shape[0]
    return pl.pallas_call(
        _norm_mod_kernel,
        grid=(m // tm,),
        in_specs=[pl.BlockSpec((tm, d), lambda i: (i, 0)),
                  pl.BlockSpec((1, d), lambda i: (0, 0)),
                  pl.BlockSpec((1, 1, d), lambda i: (i // per, 0, 0)),
                  pl.BlockSpec((1, 1, d), lambda i: (i // per, 0, 0))],
        out_specs=pl.BlockSpec((tm, d), lambda i: (i, 0)),
        out_shape=jax.ShapeDtypeStruct((m, d), BF16),
        compiler_params=_cparams(("arbitrary",)),
    )(x2, w.reshape(1, d), sc.reshape(g, 1, d), sh.reshape(g, 1, d))


def _proj_kernel(cols_ref, a_ref, w_ref, o_ref, wb):
    @pl.when(pl.program_id(1) == 0)
    def _():
        _cast_rows(w_ref, wb)

    o_ref[...] = jnp.dot(a_ref[...], wb[...], preferred_element_type=F32).astype(o_ref.dtype)


def _project(a, w, col_tiles, tn, out_dtype):
    m, k = a.shape
    tm = _tile(m, 1024)
    nj = len(col_tiles)
    grid_spec = pltpu.PrefetchScalarGridSpec(
        num_scalar_prefetch=1,
        grid=(nj, m // tm),
        in_specs=[pl.BlockSpec((tm, k), lambda j, i, c: (i, 0)),
                  pl.BlockSpec((k, tn), lambda j, i, c: (0, c[j]))],
        out_specs=pl.BlockSpec((tm, tn), lambda j, i, c: (i, j)),
        scratch_shapes=[pltpu.VMEM((k, tn), BF16)],
    )
    return pl.pallas_call(
        _proj_kernel,
        grid_spec=grid_spec,
        out_shape=jax.ShapeDtypeStruct((m, nj * tn), out_dtype),
        compiler_params=_cparams(("arbitrary", "arbitrary")),
    )(jnp.asarray(col_tiles, I32), a, w)


def _rope_tables(t, dim):
    half = dim // 2
    pos = jnp.arange(t, dtype=I32)
    row = (pos // GRID_W).astype(F32)
    col = (pos % GRID_W).astype(F32)
    inv = ROPE_THETA ** (-jnp.arange(0, half, 2, dtype=F32) / half)
    ar = row[:, None] * inv[None, :]
    ac = col[:, None] * inv[None, :]
    cr, sr, cc, sc = jnp.cos(ar), jnp.sin(ar), jnp.cos(ac), jnp.sin(ac)
    z = jnp.zeros_like(sr)
    c = jnp.concatenate([cr, cr, cc, cc], axis=1)
    s_lo = jnp.concatenate([-sr, z, -sc, z], axis=1)
    s_hi = jnp.concatenate([z, sr, z, sc], axis=1)
    return c, s_lo, s_hi


def _rope128(x, c, s_lo, s_hi):
    q4 = ATTN_HEAD_DIM // 4
    return (x * c + pltpu.roll(x, LANES - q4, axis=1) * s_lo + pltpu.roll(x, q4, axis=1) * s_hi)


def _rope256(x, c, s):
    parts = [pltpu.roll(x[:, i * LANES:(i + 1) * LANES], LANES // 2, axis=1)
             for i in range(RET_QK_DIM // LANES)]
    return x * c + jnp.concatenate(parts, axis=1) * s


def _ret_kernel(lg_ref, q_ref, k_ref, v_ref, g_ref, kc_ref, vc_ref, c_ref, s_ref, o_ref,
                qs, ks, acc, st):
    h = pl.program_id(1)
    lgf = lg_ref[0, h]
    lgb = lg_ref[1, h]
    t = q_ref.shape[0]
    ch = RET_CHUNK if t % RET_CHUNK == 0 else t
    nch = t // ch
    lc = kc_ref.shape[0]
    k_scale = RET_QK_DIM ** -0.5
    tn_dims = (((0,), (0,)), ((), ()))

    ii = lax.broadcasted_iota(I32, (ch, ch), 0)
    jj = lax.broadcasted_iota(I32, (ch, ch), 1)
    dif = (ii - jj).astype(F32)
    dmat = (jnp.where(dif >= 0, jnp.exp(lgf * jnp.maximum(dif, 0.0)), 0.0)
            + jnp.where(dif <= 0, jnp.exp(lgb * jnp.maximum(-dif, 0.0)), 0.0))
    pos = lax.broadcasted_iota(I32, (ch, 1), 0).astype(F32)
    qf_dec = jnp.exp(lgf * (pos + 1.0))
    kf_dec = jnp.exp(lgf * (ch - 1.0 - pos))
    qb_dec = jnp.exp(lgb * (ch - pos))
    kb_dec = jnp.exp(lgb * pos)
    chv = jnp.full((1, 1), float(ch), F32)
    cf_dec = jnp.exp(lgf * chv)
    cb_dec = jnp.exp(lgb * chv)

    cpos = lax.broadcasted_iota(I32, (lc, 1), 0).astype(F32)
    kc = kc_ref[...].astype(F32) * k_scale
    vc = vc_ref[...]
    st[0] = lax.dot_general((kc * jnp.exp(lgf * (lc - 1.0 - cpos))).astype(BF16), vc, tn_dims,
                            preferred_element_type=F32)
    st[1] = lax.dot_general((kc * jnp.exp(lgb * cpos)).astype(BF16), vc, tn_dims,
                            preferred_element_type=F32)

    def fwd(c, carry):
        r0 = pl.multiple_of(c * ch, ch)
        cs = c_ref[pl.ds(r0, ch), :]
        sn = s_ref[pl.ds(r0, ch), :]
        q = _rope256(q_ref[pl.ds(r0, ch), :].astype(F32), cs, sn)
        k = _rope256(k_ref[pl.ds(r0, ch), :].astype(F32), cs, sn) * k_scale
        v = v_ref[pl.ds(r0, ch), :]
        qb = q.astype(BF16)
        kb = k.astype(BF16)
        qs[pl.ds(r0, ch), :] = q
        ks[pl.ds(r0, ch), :] = k
        sc = lax.dot_general(qb, kb, (((1,), (1,)), ((), ())), preferred_element_type=F32) * dmat
        o = jnp.dot(sc.astype(BF16), v, preferred_element_type=F32)
        o += jnp.dot((q * qf_dec).astype(BF16), st[0].astype(BF16), preferred_element_type=F32)
        acc[pl.ds(r0, ch), :] = o
        st[0] = st[0] * cf_dec + lax.dot_general((k * kf_dec).astype(BF16), v, tn_dims,
                                                 preferred_element_type=F32)
        return carry

    lax.fori_loop(0, nch, fwd, 0, unroll=2)

    def bwd(i, carry):
        c = nch - 1 - i
        r0 = pl.multiple_of(c * ch, ch)
        q = qs[pl.ds(r0, ch), :]
        k = ks[pl.ds(r0, ch), :]
        v = v_ref[pl.ds(r0, ch), :]
        o = jnp.dot((q * qb_dec).astype(BF16), st[1].astype(BF16), preferred_element_type=F32)
        tot = acc[pl.ds(r0, ch), :] + o
        gate = g_ref[pl.ds(r0, ch), :].astype(F32)
        o_ref[pl.ds(r0, ch), :] = (gate * jax.nn.sigmoid(gate) * _rms(tot)).astype(o_ref.dtype)
        st[1] = st[1] * cb_dec + lax.dot_general((k * kb_dec).astype(BF16), v, tn_dims,
                                                 preferred_element_type=F32)
        return carry

    lax.fori_loop(0, nch, bwd, 0, unroll=2)


def _retention(p, pc, lg, tab_c, tab_s, b, t, lc, offs, coffs):
    h, dk, dv = RET_HEADS, RET_QK_DIM, RET_V_DIM
    qo, ko, vo, go = (_blk(offs["rq"], dk), _blk(offs["rk"], dk), _blk(offs["rv"], dv), _blk(offs["rg"], dv))
    kco, vco = _blk(coffs["rk"], dk), _blk(coffs["rv"], dv)
    grid_spec = pltpu.PrefetchScalarGridSpec(
        num_scalar_prefetch=1,
        grid=(b, h),
        in_specs=[pl.BlockSpec((t, dk), lambda bi, hi, lg: (bi, qo + hi)),
                  pl.BlockSpec((t, dk), lambda bi, hi, lg: (bi, ko + hi)),
                  pl.BlockSpec((t, dv), lambda bi, hi, lg: (bi, vo + hi)),
                  pl.BlockSpec((t, dv), lambda bi, hi, lg: (bi, go + hi)),
                  pl.BlockSpec((lc, dk), lambda bi, hi, lg: (bi, kco + hi)),
                  pl.BlockSpec((lc, dv), lambda bi, hi, lg: (bi, vco + hi)),
                  pl.BlockSpec((t, dk), lambda bi, hi, lg: (0, 0)),
                  pl.BlockSpec((t, dk), lambda bi, hi, lg: (0, 0))],
        out_specs=pl.BlockSpec((t, dv), lambda bi, hi, lg: (bi, hi)),
        scratch_shapes=[pltpu.VMEM((t, dk), F32), pltpu.VMEM((t, dk), F32),
                        pltpu.VMEM((t, dv), F32), pltpu.VMEM((2, dk, dv), F32)],
    )
    return pl.pallas_call(
        _ret_kernel,
        grid_spec=grid_spec,
        out_shape=jax.ShapeDtypeStruct((b * t, h * dv), BF16),
        compiler_params=_cparams(("arbitrary", "arbitrary")),
    )(lg, p, p, p, p, pc, pc, tab_c, tab_s)


def _attn_kernel(q_ref, kx_ref, vx_ref, kc_ref, vc_ref, qn_ref, kn_ref, c_ref, sl_ref, sh_ref, o_ref,
                 k_s, v_s):
    qt = pl.program_id(2)
    tq = q_ref.shape[0]
    lc = kc_ref.shape[0]
    hd = ATTN_HEAD_DIM
    scale = hd ** -0.5 * LOG2E

    @pl.when(qt == 0)
    def _():
        kn = kn_ref[...]
        k_s[0:lc, :] = (_rms(kc_ref[...].astype(F32)) * kn).astype(BF16)
        kx = _rms(kx_ref[...].astype(F32)) * kn
        k_s[lc:, :] = _rope128(kx, c_ref[...], sl_ref[...], sh_ref[...]).astype(BF16)
        v_s[0:lc, 0:hd] = vc_ref[...]
        v_s[lc:, 0:hd] = vx_ref[...]
        col = lax.broadcasted_iota(I32, (v_s.shape[0], hd), 1)
        v_s[:, hd:] = jnp.where(col == 0, 1.0, 0.0).astype(BF16)

    qn = qn_ref[...]
    kk = k_s[...]
    vv = v_s[...]
    sub = min(tq, ATTN_SUB)
    for u in range(tq // sub):
        r0 = pl.multiple_of(qt * tq + u * sub, sub)
        c = c_ref[pl.ds(r0, sub), :]
        sl = sl_ref[pl.ds(r0, sub), :]
        sh = sh_ref[pl.ds(r0, sub), :]
        for g in range(ATTN_HEADS // ATTN_KV_HEADS):
            q = _rms(q_ref[u * sub:(u + 1) * sub, g * hd:(g + 1) * hd].astype(F32)) * qn
            q = (_rope128(q, c, sl, sh) * scale).astype(BF16)
            s = lax.dot_general(q, kk, (((1,), (1,)), ((), ())), preferred_element_type=F32)
            m = jnp.max(s, axis=-1, keepdims=True)
            p = jnp.exp2(s - m).astype(BF16)
            ov = jnp.dot(p, vv, preferred_element_type=F32)
            o = ov[:, 0:hd] / ov[:, hd:hd + 1]
            o_ref[u * sub:(u + 1) * sub, g * hd:(g + 1) * hd] = o.astype(o_ref.dtype)


def _attention(p, pc, qn, kn, tabs, b, t, lc, offs, coffs):
    hd, kvh = ATTN_HEAD_DIM, ATTN_KV_HEADS
    gw = (ATTN_HEADS // kvh) * hd
    tq = _tile(t, 256)
    qo, ko, vo = _blk(offs["aq"], gw), _blk(offs["ak"], hd), _blk(offs["av"], hd)
    kco, vco = _blk(coffs["ak"], hd), _blk(coffs["av"], hd)
    tab = pl.BlockSpec((t, hd), lambda bi, ki, qi: (0, 0))
    return pl.pallas_call(
        _attn_kernel,
        grid=(b, kvh, t // tq),
        in_specs=[pl.BlockSpec((tq, gw), lambda bi, ki, qi: (bi * (t // tq) + qi, qo + ki)),
                  pl.BlockSpec((t, hd), lambda bi, ki, qi: (bi, ko + ki)),
                  pl.BlockSpec((t, hd), lambda bi, ki, qi: (bi, vo + ki)),
                  pl.BlockSpec((lc, hd), lambda bi, ki, qi: (bi, kco + ki)),
                  pl.BlockSpec((lc, hd), lambda bi, ki, qi: (bi, vco + ki)),
                  pl.BlockSpec((1, hd), lambda bi, ki, qi: (0, 0)),
                  pl.BlockSpec((1, hd), lambda bi, ki, qi: (0, 0)),
                  tab, tab, tab],
        out_specs=pl.BlockSpec((tq, gw), lambda bi, ki, qi: (bi * (t // tq) + qi, ki)),
        out_shape=jax.ShapeDtypeStruct((b * t, ATTN_HEADS * hd), BF16),
        scratch_shapes=[pltpu.VMEM((lc + t, hd), BF16), pltpu.VMEM((lc + t, 2 * hd), BF16)],
        compiler_params=_cparams(("arbitrary", "arbitrary", "arbitrary")),
    )(p, p, p, pc, pc, qn.reshape(1, hd), kn.reshape(1, hd), *tabs)


def _merge_kernel(r_ref, a_ref, gr_ref, ga_ref, wr_ref, wa_ref, o_ref):
    ret = jnp.dot(r_ref[...], wr_ref[...], preferred_element_type=F32)
    att = jnp.dot(a_ref[...], wa_ref[...], preferred_element_type=F32)
    z = (jax.nn.sigmoid(gr_ref[...].astype(F32)) * ret + jax.nn.sigmoid(ga_ref[...].astype(F32)) * att)
    o_ref[...] = z.astype(o_ref.dtype)


def _merge(retg, att, p, w_ret_o, w_attn_o, offs):
    m, kr = retg.shape
    ka = att.shape[1]
    d = w_ret_o.shape[1]
    tm = _tile(m, 1024)
    tn = functools.reduce(math.gcd, (offs["gr"], offs["ga"], d), 512)
    gro, gao = _blk(offs["gr"], tn), _blk(offs["ga"], tn)
    return pl.pallas_call(
        _merge_kernel,
        grid=(m // tm, d // tn),
        in_specs=[pl.BlockSpec((tm, kr), lambda i, j: (i, 0)),
                  pl.BlockSpec((tm, ka), lambda i, j: (i, 0)),
                  pl.BlockSpec((tm, tn), lambda i, j: (i, gro + j)),
                  pl.BlockSpec((tm, tn), lambda i, j: (i, gao + j)),
                  pl.BlockSpec((kr, tn), lambda i, j: (0, j)),
                  pl.BlockSpec((ka, tn), lambda i, j: (0, j))],
        out_specs=pl.BlockSpec((tm, tn), lambda i, j: (i, j)),
        out_shape=jax.ShapeDtypeStruct((m, d), BF16),
        compiler_params=_cparams(("arbitrary", "arbitrary")),
    )(retg, att, p, p, w_ret_o, w_attn_o)


def _pack_pairs(x):
    hw = x.shape[1] // 2
    lo = pltpu.bitcast(x[:, :hw].astype(BF16).astype(F32), U32)
    hi = pltpu.bitcast(x[:, hw:].astype(BF16).astype(F32), U32)
    return (hi & jnp.uint32(0xFFFF0000)) | (lo >> 16)


def _store_token_tiles(ref, packed):
    m = packed.shape[0]
    st = packed.shape[1] // LANES
    for s in range(st):
        ref[pl.ds(s, m, stride=st), :] = packed[:, s * LANES:(s + 1) * LANES]


def _load_token_tiles(ref, m):
    st = ref.shape[0] // m
    return jnp.concatenate([ref[pl.ds(s, m, stride=st), :] for s in range(st)], axis=1)


def _unpack_pairs(w):
    lo = pltpu.bitcast(w << 16, F32)
    hi = pltpu.bitcast(w & jnp.uint32(0xFFFF0000), F32)
    return lo, hi


def _post_kernel(z_ref, wo_ref, x_ref, g1_ref, pw_ref, fw_ref, sc_ref, sh_ref, rwh_ref, rwl_ref,
                 x1_ref, f_ref, lg_ref):
    tm = x_ref.shape[0]
    sub = min(tm, POST_SUB)
    st = f_ref.shape[0] // tm
    nt = (((1,), (1,)), ((), ()))
    fhs, fls = [], []
    for h in range(tm // sub):
        r0 = h * sub
        y = jnp.dot(z_ref[r0:r0 + sub, :], wo_ref[...], preferred_element_type=F32)
        x1 = x_ref[r0:r0 + sub, :] + g1_ref[0] * (_rms(y) * pw_ref[...])
        x1_ref[r0:r0 + sub, :] = x1
        f = (_rms(x1) * fw_ref[...]) * (1.0 + sc_ref[0]) + sh_ref[0]
        _store_token_tiles(f_ref.at[r0 * st:(r0 + sub) * st, :], _pack_pairs(f))
        fh = f.astype(BF16)
        fhs.append(fh)
        fls.append((f - fh.astype(F32)).astype(BF16))
    fh = jnp.concatenate(fhs, axis=0)
    fl = jnp.concatenate(fls, axis=0)
    lg_ref[...] = (lax.dot_general(rwh_ref[...], fh, nt, preferred_element_type=F32)
                   + lax.dot_general(rwh_ref[...], fl, nt, preferred_element_type=F32)
                   + lax.dot_general(rwl_ref[...], fh, nt, preferred_element_type=F32))


def _post(z, w_out, x2, g1, post_w, pre_w, sc2, sh2, router_wt, rows_per_mod):
    m, d = x2.shape
    e = router_wt.shape[0]
    st = d // 2 // LANES
    assert st == SUBLANES, "token rows must be whole (8, 128) tiles"
    rw_hi = router_wt.astype(BF16)
    rw_lo = (router_wt - rw_hi.astype(F32)).astype(BF16)
    tm = _tile(rows_per_mod, 2 * POST_SUB)
    per = rows_per_mod // tm
    g = g1.shape[0]
    vec = pl.BlockSpec((1, d), lambda i: (0, 0))
    mod = pl.BlockSpec((1, 1, d), lambda i: (i // per, 0, 0))
    once = pl.Buffered(1)
    return pl.pallas_call(
        _post_kernel,
        grid=(m // tm,),
        in_specs=[pl.BlockSpec((tm, d), lambda i: (i, 0)),
                  pl.BlockSpec((d, d), lambda i: (0, 0), pipeline_mode=once),
                  pl.BlockSpec((tm, d), lambda i: (i, 0)),
                  mod, vec, vec, mod, mod,
                  pl.BlockSpec((e, d), lambda i: (0, 0), pipeline_mode=once),
                  pl.BlockSpec((e, d), lambda i: (0, 0), pipeline_mode=once)],
        out_specs=[pl.BlockSpec((tm, d), lambda i: (i, 0)),
                   pl.BlockSpec((tm * st, LANES), lambda i: (i, 0)),
                   pl.BlockSpec((e, tm), lambda i: (0, i))],
        out_shape=[jax.ShapeDtypeStruct((m, d), F32),
                   jax.ShapeDtypeStruct((m * st, LANES), U32),
                   jax.ShapeDtypeStruct((e, m), F32)],
        compiler_params=_cparams(("arbitrary",)),
    )(z, w_out, x2, g1.reshape(g, 1, d), post_w.reshape(1, d), pre_w.reshape(1, d),
      sc2.reshape(g, 1, d), sh2.reshape(g, 1, d), rw_hi, rw_lo)


def _route_kernel(lg_ref, b_ref, tri_ref, idx_ref, w_ref, pos_ref, cnt_ref, carry):
    @pl.when(pl.program_id(0) == 0)
    def _():
        carry[...] = jnp.zeros_like(carry)

    e, tn = lg_ref.shape
    gs = e // N_GROUPS
    neg = -jnp.inf
    big = float(e)
    s = jax.nn.sigmoid(lg_ref[...])
    choice = s + b_ref[...]

    row_g = lax.broadcasted_iota(I32, (gs, tn), 0).astype(F32)
    rows = []
    for g in range(N_GROUPS):
        blk = choice[g * gs:(g + 1) * gs]
        m1 = jnp.max(blk, axis=0, keepdims=True)
        i1 = jnp.min(jnp.where(blk == m1, row_g, big), axis=0, keepdims=True)
        m2 = jnp.max(jnp.where(row_g == i1, neg, blk), axis=0, keepdims=True)
        rows.append(m1 + m2)
    gscore = jnp.concatenate(rows, axis=0)

    row_n = lax.broadcasted_iota(I32, (N_GROUPS, tn), 0).astype(F32)
    sel = jnp.zeros((N_GROUPS, tn), F32)
    cur = gscore
    for _ in range(TOPK_GROUPS):
        m = jnp.max(cur, axis=0, keepdims=True)
        gi = jnp.min(jnp.where(cur == m, row_n, big), axis=0, keepdims=True)
        hit = row_n == gi
        sel = jnp.where(hit, 1.0, sel)
        cur = jnp.where(hit, neg, cur)

    masked = jnp.concatenate(
        [jnp.where(sel[g:g + 1] > 0.0, choice[g * gs:(g + 1) * gs], neg) for g in range(N_GROUPS)], axis=0)

    row_e = lax.broadcasted_iota(I32, (e, tn), 0).astype(F32)
    onehot = jnp.zeros((e, tn), F32)
    cur = masked
    idxs, ws = [], []
    for _ in range(TOP_K):
        m = jnp.max(cur, axis=0, keepdims=True)
        ik = jnp.min(jnp.where(cur == m, row_e, big), axis=0, keepdims=True)
        hit = row_e == ik
        ws.append(jnp.sum(jnp.where(hit, s, 0.0), axis=0, keepdims=True))
        cur = jnp.where(hit, neg, cur)
        onehot = jnp.where(hit, 1.0, onehot)
        idxs.append(ik)
    w = jnp.concatenate(ws, axis=0)
    w = w / jnp.sum(w, axis=0, keepdims=True) * ROUTED_SCALE

    prefix = jnp.dot(onehot.astype(BF16), tri_ref[...], preferred_element_type=F32)
    base = prefix + carry[...]
    pos = jnp.concatenate(
        [jnp.sum(jnp.where(row_e == ik, base, 0.0), axis=0, keepdims=True) for ik in idxs], axis=0)
    carry[...] = carry[...] + jnp.sum(onehot, axis=1, keepdims=True)

    idx_ref[...] = jnp.concatenate(idxs, axis=0).astype(I32)
    w_ref[...] = w
    pos_ref[...] = pos.astype(I32)
    cnt_ref[...] = jnp.broadcast_to(carry[...], cnt_ref.shape).astype(I32)


def _route(logits_t, bias):
    e, n = logits_t.shape
    tn = _tile(n, 512)
    tri = (jnp.arange(tn)[:, None] < jnp.arange(tn)[None, :]).astype(BF16)
    kspec = pl.BlockSpec((TOP_K, tn), lambda i: (0, i))
    return pl.pallas_call(
        _route_kernel,
        grid=(n // tn,),
        in_specs=[pl.BlockSpec((e, tn), lambda i: (0, i)),
                  pl.BlockSpec((e, 1), lambda i: (0, 0)),
                  pl.BlockSpec((tn, tn), lambda i: (0, 0))],
        out_specs=[kspec, kspec, kspec, pl.BlockSpec((e, LANES), lambda i: (0, 0))],
        out_shape=[jax.ShapeDtypeStruct((TOP_K, n), I32), jax.ShapeDtypeStruct((TOP_K, n), F32),
                   jax.ShapeDtypeStruct((TOP_K, n), I32), jax.ShapeDtypeStruct((e, LANES), I32)],
        scratch_shapes=[pltpu.VMEM((e, 1), F32)],
        compiler_params=_cparams(("arbitrary",)),
    )(logits_t, bias.reshape(e, 1), tri)


def _dest_kernel(idx_ref, pos_ref, off_ref, o_ref):
    e = off_ref.shape[0]
    tn = idx_ref.shape[1]
    row_e = lax.broadcasted_iota(I32, (e, tn), 0)
    off = off_ref[...]
    idx = idx_ref[...]
    rows = [jnp.sum(jnp.where(row_e == idx[k:k + 1], off, 0.0), axis=0, keepdims=True) for k in range(TOP_K)]
    o_ref[...] = jnp.concatenate(rows, axis=0).astype(I32) + pos_ref[...]


def _dest(top_idx, pos, pad_off):
    k, n = top_idx.shape
    e = pad_off.shape[0]
    tn = _tile(n, 512)
    spec = pl.BlockSpec((k, tn), lambda i: (0, i))
    return pl.pallas_call(
        _dest_kernel,
        grid=(n // tn,),
        in_specs=[spec, spec, pl.BlockSpec((e, 1), lambda i: (0, 0))],
        out_specs=spec,
        out_shape=jax.ShapeDtypeStruct((k, n), I32),
        compiler_params=_cparams(("arbitrary",)),
    )(top_idx, pos, pad_off.astype(F32).reshape(e, 1))


def _dispatch_kernel(cnt_ref, off_ref, nu_ref, dest_ref, f_ref, xs_hbm, zbuf, sem, zsem):
    i = pl.program_id(0)
    tt = dest_ref.shape[1]
    st = SUBLANES
    bm = zbuf.shape[0] // st
    n_exp = cnt_ref.shape[0]
    nb = xs_hbm.shape[0] // (bm * st)

    def zero_fill(wait):
        def fill(start, rows):
            cp = pltpu.make_async_copy(zbuf.at[pl.ds(0, rows * st), :],
                                       xs_hbm.at[pl.ds(pl.multiple_of(start * st, st), rows * st), :], zsem)
            if wait:
                cp.wait()
            else:
                cp.start()

        def per_expert(e, c):
            cnt = cnt_ref[e]
            npad = (-cnt) & (bm - 1)
            start = off_ref[e] + cnt
            p = bm // 2
            while p >= 1:
                @pl.when((npad & p) != 0)
                def _(start=start, p=p):
                    fill(start, p)
                start = start + (npad & p)
                p //= 2
            return c
        lax.fori_loop(0, n_exp, per_expert, 0)

        def per_block(b, c):
            fill(b * bm, bm)
            return c
        lax.fori_loop(nu_ref[0], nb, per_block, 0)

    def row_copy(j, dst):
        return pltpu.make_async_copy(f_ref.at[pl.ds(pl.multiple_of(j * st, st), st), :],
                                     xs_hbm.at[pl.ds(pl.multiple_of(dst * st, st), st), :], sem)

    @pl.when(i == 0)
    def _():
        zbuf[...] = jnp.zeros_like(zbuf)
        zero_fill(False)

    def issue(j, c):
        base = j * DMA_GROUP
        dsts = [[dest_ref[k, base + u] for k in range(TOP_K)] for u in range(DMA_GROUP)]
        for u in range(DMA_GROUP):
            for k in range(TOP_K):
                row_copy(base + u, dsts[u][k]).start(priority=k % 2)

        @pl.when(j >= DMA_LAG)
        def _():
            wait_group()
        return c

    def wait_group():
        for _ in range(DMA_GROUP * TOP_K):
            row_copy(0, 0).wait()

    n_groups = tt // DMA_GROUP
    lax.fori_loop(0, n_groups, issue, 0)

    def drain(j, c):
        wait_group()
        return c
    lax.fori_loop(0, min(DMA_LAG, n_groups), drain, 0)

    @pl.when(i == pl.num_programs(0) - 1)
    def _():
        zero_fill(True)


def _dispatch(counts, pad_off, n_used, dest, f_packed, n_slots):
    st = SUBLANES
    n = f_packed.shape[0] // st
    tt = _tile(n, 512)
    grid_spec = pltpu.PrefetchScalarGridSpec(
        num_scalar_prefetch=3,
        grid=(n // tt,),
        in_specs=[pl.BlockSpec((TOP_K, tt), lambda i, c, o, u: (0, i), memory_space=pltpu.SMEM),
                  pl.BlockSpec((tt * st, LANES), lambda i, c, o, u: (i, 0))],
        out_specs=pl.BlockSpec(memory_space=pl.ANY),
        scratch_shapes=[pltpu.VMEM((MOE_BLOCK * st, LANES), U32), pltpu.SemaphoreType.DMA(()),
                        pltpu.SemaphoreType.DMA(())],
    )
    return pl.pallas_call(
        _dispatch_kernel,
        grid_spec=grid_spec,
        out_shape=jax.ShapeDtypeStruct((n_slots * st, LANES), U32),
        compiler_params=_cparams(("arbitrary",)),
    )(counts, pad_off, n_used, dest, f_packed)


def _cast_rows(src, dst):
    rows = src.shape[0]
    ch = CAST_ROWS * LANES // src.shape[1]

    def body(j, c):
        r0 = pl.multiple_of(j * ch, ch)
        dst[pl.ds(r0, ch), :] = src[pl.ds(r0, ch), :].astype(dst.dtype)
        return c
    lax.fori_loop(0, rows // ch, body, 0, unroll=4)


def _expert_kernel(ob_ref, act_ref, na_ref, nu_ref, x_ref, w1_hbm, w3_hbm, w2_hbm, y_ref,
                   w1f, w3f, w2f, w1b, w3b, w2b, wsem):
    b = pl.program_id(0)
    n_used = nu_ref[0]
    n_act = na_ref[0]
    bm = MOE_BLOCK
    hw = w1b.shape[0] // 2
    ring = w1f.shape[0]

    def fetch(i, wait):
        e = act_ref[i]
        slot = i % ring
        for src, dst in ((w1_hbm, w1f), (w3_hbm, w3f), (w2_hbm, w2f)):
            cp = pltpu.make_async_copy(src.at[e], dst.at[slot], wsem.at[slot])
            if wait:
                cp.wait()
            else:
                cp.start(priority=1)

    @pl.when(b < n_used)
    def _():
        i = ob_ref[b]
        first = jnp.logical_or(b == 0, i != ob_ref[jnp.maximum(b - 1, 0)])

        @pl.when(b == 0)
        def _():
            fetch(0, False)
            for ahead in range(1, ring):
                @pl.when(n_act > ahead)
                def _(ahead=ahead):
                    fetch(ahead, False)

        @pl.when(first)
        def _():
            fetch(i, True)
            slot = i % ring
            for src, dst in ((w1f, w1b), (w3f, w3b), (w2f, w2b)):
                _cast_rows(src.at[slot], dst)

            @pl.when(i + ring < n_act)
            def _():
                fetch(i + ring, False)

        lo, hi = _unpack_pairs(_load_token_tiles(x_ref, bm))
        lo = lo.astype(BF16)
        hi = hi.astype(BF16)
        h1 = (jnp.dot(lo, w1b[0:hw, :], preferred_element_type=F32)
              + jnp.dot(hi, w1b[hw:, :], preferred_element_type=F32))
        h3 = (jnp.dot(lo, w3b[0:hw, :], preferred_element_type=F32)
              + jnp.dot(hi, w3b[hw:, :], preferred_element_type=F32))
        hid = (h1 * jax.nn.sigmoid(h1) * h3).astype(BF16)
        _store_token_tiles(y_ref, _pack_pairs(jnp.dot(hid, w2b[...], preferred_element_type=F32)))


def _experts(xs, blk_ord, active, n_active, n_used, w1, w3, w2):
    nb = blk_ord.shape[0]
    e, d, fdim = w1.shape
    rows = MOE_BLOCK * (d // 2 // LANES)
    w_elems = 3 * d * fdim
    vmem_limit = (W_BUFS * w_elems * 4 + w_elems * 2 + 4 * rows * LANES * 4
                  + MOE_BLOCK * (2 * fdim + 2 * d) * 4 + VMEM_SLACK)
    assert vmem_limit <= V7X_VMEM_BYTES
    grid_spec = pltpu.PrefetchScalarGridSpec(
        num_scalar_prefetch=4,
        grid=(nb,),
        in_specs=[pl.BlockSpec((rows, LANES), lambda b, ob, ac, na, nu: (jnp.minimum(b, nu[0] - 1), 0)),
                  pl.BlockSpec(memory_space=pl.ANY),
                  pl.BlockSpec(memory_space=pl.ANY),
                  pl.BlockSpec(memory_space=pl.ANY)],
        out_specs=pl.BlockSpec((rows, LANES), lambda b, ob, ac, na, nu: (jnp.minimum(b, nu[0] - 1), 0)),
        scratch_shapes=[pltpu.VMEM((W_BUFS, d, fdim), F32), pltpu.VMEM((W_BUFS, d, fdim), F32),
                        pltpu.VMEM((W_BUFS, fdim, d), F32),
                        pltpu.VMEM((d, fdim), BF16), pltpu.VMEM((d, fdim), BF16), pltpu.VMEM((fdim, d), BF16),
                        pltpu.SemaphoreType.DMA((W_BUFS,))],
    )
    return pl.pallas_call(
        _expert_kernel,
        grid_spec=grid_spec,
        out_shape=jax.ShapeDtypeStruct((nb * rows, LANES), U32),
        input_output_aliases={4: 0},
        compiler_params=_cparams(("arbitrary",), vmem_limit),
    )(blk_ord, active, n_active, n_used, xs, w1, w3, w2)


def _combine_kernel(dcur_ref, dnxt_ref, wt_ref, f_ref, x1_ref, g2_ref, pw_ref, s1_ref, s3_ref, s2_ref, ys_hbm,
                    o_ref, ybuf, sem):
    i = pl.program_id(0)
    tm = x1_ref.shape[0]
    hw = x1_ref.shape[1] // 2
    st = f_ref.shape[0] // tm
    slot = i % 2

    def row_copy(src, k, r, s):
        return pltpu.make_async_copy(ys_hbm.at[pl.ds(pl.multiple_of(src * st, st), st), :],
                                     ybuf.at[s, k, pl.ds(pl.multiple_of(r * st, st), st), :], sem.at[s])

    def start_gather(d_ref, s):
        def body(j, c):
            base = j * DMA_GROUP
            srcs = [[d_ref[k, base + u] for k in range(TOP_K)] for u in range(DMA_GROUP)]
            for u in range(DMA_GROUP):
                for k in range(TOP_K):
                    row_copy(srcs[u][k], k, base + u, s).start(priority=k % 2)
            return c
        lax.fori_loop(0, tm // DMA_GROUP, body, 0)

    def wait_gather(s):
        def body(r, c):
            for k in range(TOP_K):
                row_copy(0, k, r, s).wait()
            return c
        lax.fori_loop(0, tm, body, 0, unroll=2)

    @pl.when(i == 0)
    def _():
        start_gather(dcur_ref, 0)

    @pl.when(i + 1 < pl.num_programs(0))
    def _():
        start_gather(dnxt_ref, 1 - slot)

    lo, hi = _unpack_pairs(_load_token_tiles(f_ref, tm))
    lo = lo.astype(BF16)
    hi = hi.astype(BF16)
    h1 = (jnp.dot(lo, s1_ref[0:hw, :], preferred_element_type=F32)
          + jnp.dot(hi, s1_ref[hw:, :], preferred_element_type=F32))
    h3 = (jnp.dot(lo, s3_ref[0:hw, :], preferred_element_type=F32)
          + jnp.dot(hi, s3_ref[hw:, :], preferred_element_type=F32))
    shared = jnp.dot((h1 * jax.nn.sigmoid(h1) * h3).astype(BF16), s2_ref[...], preferred_element_type=F32)

    wait_gather(slot)
    wt = wt_ref[...]
    r_lo = shared[:, :hw]
    r_hi = shared[:, hw:]
    for k in range(TOP_K):
        a, c = _unpack_pairs(_load_token_tiles(ybuf.at[slot, k], tm))
        w = wt[:, k:k + 1]
        r_lo = r_lo + a * w
        r_hi = r_hi + c * w
    moe = jnp.concatenate([r_lo, r_hi], axis=1)
    o_ref[...] = x1_ref[...] + g2_ref[0] * (_rms(moe) * pw_ref[...])


def _combine(ys, dest, top_wt, f_packed, x1, g2, post_w, s1, s3, s2, rows_per_mod):
    m, d = x1.shape
    st = d // 2 // LANES
    fdim = s1.shape[1]
    tm = _tile(rows_per_mod, 256)
    per = rows_per_mod // tm
    g = g2.shape[0]
    nblk = m // tm
    smem = functools.partial(pl.BlockSpec, memory_space=pltpu.SMEM)
    return pl.pallas_call(
        _combine_kernel,
        grid=(nblk,),
        in_specs=[smem((TOP_K, tm), lambda i: (0, i)),
                  smem((TOP_K, tm), lambda i: (0, jnp.minimum(i + 1, nblk - 1))),
                  pl.BlockSpec((tm, TOP_K), lambda i: (i, 0)),
                  pl.BlockSpec((tm * st, LANES), lambda i: (i, 0)),
                  pl.BlockSpec((tm, d), lambda i: (i, 0)),
                  pl.BlockSpec((1, 1, d), lambda i: (i // per, 0, 0)),
                  pl.BlockSpec((1, d), lambda i: (0, 0)),
                  pl.BlockSpec((d, fdim), lambda i: (0, 0)),
                  pl.BlockSpec((d, fdim), lambda i: (0, 0)),
                  pl.BlockSpec((fdim, d), lambda i: (0, 0)),
                  pl.BlockSpec(memory_space=pl.ANY)],
        out_specs=pl.BlockSpec((tm, d), lambda i: (i, 0)),
        out_shape=jax.ShapeDtypeStruct((m, d), F32),
        scratch_shapes=[pltpu.VMEM((2, TOP_K, tm * st, LANES), U32), pltpu.SemaphoreType.DMA((2,))],
        compiler_params=_cparams(("arbitrary",)),
    )(dest, dest, top_wt, f_packed, x1, g2.reshape(g, 1, d), post_w.reshape(1, d), s1, s3, s2, ys)


def _layer(x, c, ctx, c_ctx, ada_w, ada_b, mix_pre_norm, mix_post_norm, ffn_pre_norm, ffn_post_norm,
           w_in, ret_decay_fwd, ret_decay_bwd, attn_q_norm, attn_k_norm, w_ret_o, w_attn_o, w_out,
           router_w, router_b, exp_w1, exp_w3, exp_w2, shared_w1, shared_w3, shared_w2):
    b, t, d = x.shape
    lc = ctx.shape[1]
    n = b * t
    rq_w, rv_w = RET_HEADS * RET_QK_DIM, RET_HEADS * RET_V_DIM
    aq_w, akv_w = ATTN_HEADS * ATTN_HEAD_DIM, ATTN_KV_HEADS * ATTN_HEAD_DIM
    names = ("rq", "rk", "rv", "rg", "aq", "ak", "av", "gr", "ga")
    widths = (rq_w, rq_w, rv_w, rv_w, aq_w, akv_w, akv_w, d, d)
    offs, o = {}, 0
    for nm, wd in zip(names, widths):
        offs[nm] = o
        o += wd

    rows = -(-(b + 1) // 16) * 16
    cond = jnp.zeros((rows, d), F32).at[:b].set(c).at[b].set(c_ctx)
    mod = _adaln(cond, ada_w, ada_b)
    sh1, sc1, g1, sh2, sc2, g2 = [mod[:b, i * d:(i + 1) * d] for i in range(6)]
    ch1, cs1 = mod[b:b + 1, 0:d], mod[b:b + 1, d:2 * d]

    x2 = x.reshape(n, d)
    h_x = _norm_mod(x2, mix_pre_norm, sc1, sh1, t)
    h_c = _norm_mod(ctx.reshape(b * lc, d), mix_pre_norm, cs1, ch1, b * lc)

    tn = _tile(w_in.shape[1], 1024)
    p = _project(h_x, w_in, list(range(w_in.shape[1] // tn)), tn, BF16)
    cnames = ("rv", "rk", "ak", "av")
    tnc = functools.reduce(math.gcd, [offs[nm] for nm in cnames] + [widths[names.index(nm)] for nm in cnames], tn)
    coffs, ctiles = {}, []
    for nm in cnames:
        coffs[nm] = len(ctiles) * tnc
        ctiles += list(range(offs[nm] // tnc, (offs[nm] + widths[names.index(nm)]) // tnc))
    pc = _project(h_c, w_in, ctiles, tnc, BF16)

    lg = jnp.stack([jax.nn.log_sigmoid(ret_decay_fwd.astype(F32)),
                    jax.nn.log_sigmoid(ret_decay_bwd.astype(F32))])
    rc, rs_lo, rs_hi = _rope_tables(t, RET_QK_DIM)
    retg = _retention(p, pc, lg, rc, rs_lo + rs_hi, b, t, lc, offs, coffs)
    att = _attention(p, pc, attn_q_norm, attn_k_norm, _rope_tables(t, ATTN_HEAD_DIM), b, t, lc, offs, coffs)

    z = _merge(retg, att, p, w_ret_o.astype(BF16), w_attn_o.astype(BF16), offs)
    x1, f_packed, logits_t = _post(z, w_out.astype(BF16), x2, g1, mix_post_norm, ffn_pre_norm, sc2, sh2,
                                   router_w.T, t)

    top_idx, top_w, pos, cnt = _route(logits_t, router_b)

    bm = MOE_BLOCK
    e = router_w.shape[1]
    nb = (n * TOP_K) // bm + e
    counts = cnt[:, 0]
    padded = (counts + bm - 1) // bm * bm
    pad_end = jnp.cumsum(padded)
    pad_off = pad_end - padded
    blk_start = jnp.arange(nb, dtype=I32) * bm
    blk_expert = jnp.minimum(jnp.sum((pad_end[None, :] <= blk_start[:, None]).astype(I32), axis=1), e - 1)
    n_used = (pad_end[-1] // bm).astype(I32).reshape(1)
    is_act = (counts > 0).astype(I32)
    exp_ord = jnp.cumsum(is_act) - is_act
    n_active = jnp.sum(is_act).astype(I32).reshape(1)
    eids = jnp.arange(e, dtype=I32)
    active = jnp.sum(jnp.where((exp_ord[None, :] == eids[:, None]) & (is_act[None, :] > 0), eids[None, :], 0),
                     axis=1).astype(I32)
    blk_ord = exp_ord[blk_expert].astype(I32)

    dest = _dest(top_idx, pos, pad_off)
    xs = _dispatch(counts, pad_off, n_used, dest, f_packed, nb * bm)
    ys = _experts(xs, blk_ord, active, n_active, n_used, exp_w1, exp_w3, exp_w2)
    out = _combine(ys, dest, top_w.T, f_packed, x1, g2, ffn_post_norm, shared_w1.astype(BF16),
                   shared_w3.astype(BF16), shared_w2.astype(BF16), t)
    return out.reshape(b, t, d)


def kernel(x, c, ctx, c_ctx, ada_w, ada_b, mix_pre_norm, mix_post_norm, ffn_pre_norm, ffn_post_norm,
           w_in, ret_decay_fwd, ret_decay_bwd, attn_q_norm, attn_k_norm, w_ret_o, w_attn_o, w_out,
           router_w, router_b, exp_w1, exp_w3, exp_w2, shared_w1, shared_w3, shared_w2):
    assert ada_w.shape[0] == 1, "single-layer operation"
    return _layer(x, c, ctx, c_ctx, ada_w[0], ada_b[0], mix_pre_norm[0], mix_post_norm[0], ffn_pre_norm[0],
                  ffn_post_norm[0], w_in[0], ret_decay_fwd[0], ret_decay_bwd[0], attn_q_norm[0],
                  attn_k_norm[0], w_ret_o[0], w_attn_o[0], w_out[0], router_w[0], router_b[0],
                  exp_w1[0], exp_w3[0], exp_w2[0], shared_w1[0], shared_w3[0], shared_w2[0])
```

```python
import functools
import math

import jax
import jax.numpy as jnp
import numpy as np
from jax import lax
from jax.experimental import pallas as pl
from jax.experimental.pallas import tpu as pltpu

F32 = jnp.float32
BF16 = jnp.bfloat16
U32 = jnp.uint32
I32 = jnp.int32

GRID_W = 64
NORM_EPS = 1e-6
ROPE_THETA = 10000.0
RET_HEADS = 4
RET_QK_DIM = 256
RET_V_DIM = 512
ATTN_HEADS = 16
ATTN_KV_HEADS = 4
ATTN_HEAD_DIM = 128
N_EXPERTS = 256
TOP_K = 8
N_GROUPS = 8
TOPK_GROUPS = 4
EXPERT_DIM = 512
ROUTED_SCALE = 2.5
LOG2E = 1.4426950408889634

V7X_VMEM_BYTES = 64 * 1024 * 1024
VMEM_LIMIT = V7X_VMEM_BYTES - 12 * 1024 * 1024
VMEM_SLACK = 4 * 1024 * 1024
LANES = 128
SUBLANES = 8

RET_CHUNK = 256
MOE_BLOCK = 256
DMA_GROUP = 2
DMA_LAG = 32
CAST_ROWS = 256
POST_SUB = 256
ATTN_SUB = 128
W_BUFS = 3


def _cparams(sem, vmem_limit=VMEM_LIMIT):
    return pltpu.CompilerParams(dimension_semantics=sem, vmem_limit_bytes=vmem_limit)


def _tile(dim, pref):
    if dim <= pref:
        return dim
    for t in range(pref, 0, -LANES):
        if dim % t == 0:
            return t
    raise ValueError((dim, pref))


def _blk(off, width):
    assert off % width == 0, (off, width)
    return off // width


def _ada_kernel(s_ref, w_ref, b_ref, o_ref):
    s = s_ref[...]
    s = (s * jax.nn.sigmoid(s)).astype(BF16)
    o_ref[...] = jnp.dot(s, w_ref[...].astype(BF16), preferred_element_type=F32) + b_ref[...]


def _adaln(cond, w, b):
    r, d = cond.shape
    n = w.shape[1]
    tn = _tile(n, 1024)
    return pl.pallas_call(
        _ada_kernel,
        grid=(n // tn,),
        in_specs=[pl.BlockSpec((r, d), lambda j: (0, 0)),
                  pl.BlockSpec((d, tn), lambda j: (0, j)),
                  pl.BlockSpec((1, tn), lambda j: (0, j))],
        out_specs=pl.BlockSpec((r, tn), lambda j: (0, j)),
        out_shape=jax.ShapeDtypeStruct((r, n), F32),
        compiler_params=_cparams(("arbitrary",)),
    )(cond, w, b.reshape(1, n))


def _rms(x):
    return x * lax.rsqrt(jnp.mean(x * x, axis=-1, keepdims=True) + NORM_EPS)


def _norm_mod_kernel(x_ref, w_ref, sc_ref, sh_ref, o_ref):
    y = _rms(x_ref[...]) * w_ref[...]
    o_ref[...] = (y * (1.0 + sc_ref[0]) + sh_ref[0]).astype(o_ref.dtype)


def _norm_mod(x2, w, sc, sh, rows_per_mod):
    m, d = x2.shape
    tm = _tile(rows_per_mod, 512)
    per = rows_per_mod // tm
    g = sc.shape[0]
    return pl.pallas_call(
        _norm_mod_kernel,
        grid=(m // tm,),
        in_specs=[pl.BlockSpec((tm, d), lambda i: (i, 0)),
                  pl.BlockSpec((1, d), lambda i: (0, 0)),
                  pl.BlockSpec((1, 1, d), lambda i: (i // per, 0, 0)),
                  pl.BlockSpec((1, 1, d), lambda i: (i // per, 0, 0))],
        out_specs=pl.BlockSpec((tm, d), lambda i: (i, 0)),
        out_shape=jax.ShapeDtypeStruct((m, d), BF16),
        compiler_params=_cparams(("arbitrary",)),
    )(x2, w.reshape(1, d), sc.reshape(g, 1, d), sh.reshape(g, 1, d))


def _proj_kernel(cols_ref, a_ref, w_ref, o_ref, wb):
    @pl.when(pl.program_id(1) == 0)
    def _():
        _cast_rows(w_ref, wb)

    o_ref[...] = jnp.dot(a_ref[...], wb[...], preferred_element_type=F32).astype(o_ref.dtype)


def _project(a, w, col_tiles, tn, out_dtype):
    m, k = a.shape
    tm = _tile(m, 1024)
    nj = len(col_tiles)
    grid_spec = pltpu.PrefetchScalarGridSpec(
        num_scalar_prefetch=1,
        grid=(nj, m // tm),
        in_specs=[pl.BlockSpec((tm, k), lambda j, i, c: (i, 0)),
                  pl.BlockSpec((k, tn), lambda j, i, c: (0, c[j]))],
        out_specs=pl.BlockSpec((tm, tn), lambda j, i, c: (i, j)),
        scratch_shapes=[pltpu.VMEM((k, tn), BF16)],
    )
    return pl.pallas_call(
        _proj_kernel,
        grid_spec=grid_spec,
        out_shape=jax.ShapeDtypeStruct((m, nj * tn), out_dtype),
        compiler_params=_cparams(("arbitrary", "arbitrary")),
    )(jnp.asarray(col_tiles, I32), a, w)


def _rope_tables(t, dim):
    half = dim // 2
    pos = jnp.arange(t, dtype=I32)
    row = (pos // GRID_W).astype(F32)
    col = (pos % GRID_W).astype(F32)
    inv = ROPE_THETA ** (-jnp.arange(0, half, 2, dtype=F32) / half)
    ar = row[:, None] * inv[None, :]
    ac = col[:, None] * inv[None, :]
    cr, sr, cc, sc = jnp.cos(ar), jnp.sin(ar), jnp.cos(ac), jnp.sin(ac)
    z = jnp.zeros_like(sr)
    c = jnp.concatenate([cr, cr, cc, cc], axis=1)
    s_lo = jnp.concatenate([-sr, z, -sc, z], axis=1)
    s_hi = jnp.concatenate([z, sr, z, sc], axis=1)
    return c, s_lo, s_hi


def _rope128(x, c, s_lo, s_hi):
    q4 = ATTN_HEAD_DIM // 4
    return (x * c + pltpu.roll(x, LANES - q4, axis=1) * s_lo + pltpu.roll(x, q4, axis=1) * s_hi)


def _rope256(x, c, s):
    parts = [pltpu.roll(x[:, i * LANES:(i + 1) * LANES], LANES // 2, axis=1)
             for i in range(RET_QK_DIM // LANES)]
    return x * c + jnp.concatenate(parts, axis=1) * s


def _ret_kernel(lg_ref, q_ref, k_ref, v_ref, g_ref, kc_ref, vc_ref, c_ref, s_ref, o_ref,
                qs, ks, acc, st):
    h = pl.program_id(1)
    lgf = lg_ref[0, h]
    lgb = lg_ref[1, h]
    t = q_ref.shape[0]
    ch = RET_CHUNK if t % RET_CHUNK == 0 else t
    nch = t // ch
    lc = kc_ref.shape[0]
    k_scale = RET_QK_DIM ** -0.5
    tn_dims = (((0,), (0,)), ((), ()))

    ii = lax.broadcasted_iota(I32, (ch, ch), 0)
    jj = lax.broadcasted_iota(I32, (ch, ch), 1)
    dif = (ii - jj).astype(F32)
    dmat = (jnp.where(dif >= 0, jnp.exp(lgf * jnp.maximum(dif, 0.0)), 0.0)
            + jnp.where(dif <= 0, jnp.exp(lgb * jnp.maximum(-dif, 0.0)), 0.0))
    pos = lax.broadcasted_iota(I32, (ch, 1), 0).astype(F32)
    qf_dec = jnp.exp(lgf * (pos + 1.0))
    kf_dec = jnp.exp(lgf * (ch - 1.0 - pos))
    qb_dec = jnp.exp(lgb * (ch - pos))
    kb_dec = jnp.exp(lgb * pos)
    chv = jnp.full((1, 1), float(ch), F32)
    cf_dec = jnp.exp(lgf * chv)
    cb_dec = jnp.exp(lgb * chv)

    cpos = lax.broadcasted_iota(I32, (lc, 1), 0).astype(F32)
    kc = kc_ref[...].astype(F32) * k_scale
    vc = vc_ref[...]
    st[0] = lax.dot_general((kc * jnp.exp(lgf * (lc - 1.0 - cpos))).astype(BF16), vc, tn_dims,
                            preferred_element_type=F32)
    st[1] = lax.dot_general((kc * jnp.exp(lgb * cpos)).astype(BF16), vc, tn_dims,
                            preferred_element_type=F32)

    def fwd(c, carry):
        r0 = pl.multiple_of(c * ch, ch)
        cs = c_ref[pl.ds(r0, ch), :]
        sn = s_ref[pl.ds(r0, ch), :]
        q = _rope256(q_ref[pl.ds(r0, ch), :].astype(F32), cs, sn)
        k = _rope256(k_ref[pl.ds(r0, ch), :].astype(F32), cs, sn) * k_scale
        v = v_ref[pl.ds(r0, ch), :]
        qb = q.astype(BF16)
        kb = k.astype(BF16)
        qs[pl.ds(r0, ch), :] = q
        ks[pl.ds(r0, ch), :] = k
        sc = lax.dot_general(qb, kb, (((1,), (1,)), ((), ())), preferred_element_type=F32) * dmat
        o = jnp.dot(sc.astype(BF16), v, preferred_element_type=F32)
        o += jnp.dot((q * qf_dec).astype(BF16), st[0].astype(BF16), preferred_element_type=F32)
        acc[pl.ds(r0, ch), :] = o
        st[0] = st[0] * cf_dec + lax.dot_general((k * kf_dec).astype(BF16), v, tn_dims,
                                                 preferred_element_type=F32)
        return carry

    lax.fori_loop(0, nch, fwd, 0, unroll=2)

    def bwd(i, carry):
        c = nch - 1 - i
        r0 = pl.multiple_of(c * ch, ch)
        q = qs[pl.ds(r0, ch), :]
        k = ks[pl.ds(r0, ch), :]
        v = v_ref[pl.ds(r0, ch), :]
        o = jnp.dot((q * qb_dec).astype(BF16), st[1].astype(BF16), preferred_element_type=F32)
        tot = acc[pl.ds(r0, ch), :] + o
        gate = g_ref[pl.ds(r0, ch), :].astype(F32)
        o_ref[pl.ds(r0, ch), :] = (gate * jax.nn.sigmoid(gate) * _rms(tot)).astype(o_ref.dtype)
        st[1] = st[1] * cb_dec + lax.dot_general((k * kb_dec).astype(BF16), v, tn_dims,
                                                 preferred_element_type=F32)
        return carry

    lax.fori_loop(0, nch, bwd, 0, unroll=2)


def _retention(p, pc, lg, tab_c, tab_s, b, t, lc, offs, coffs):
    h, dk, dv = RET_HEADS, RET_QK_DIM, RET_V_DIM
    qo, ko, vo, go = (_blk(offs["rq"], dk), _blk(offs["rk"], dk), _blk(offs["rv"], dv), _blk(offs["rg"], dv))
    kco, vco = _blk(coffs["rk"], dk), _blk(coffs["rv"], dv)
    grid_spec = pltpu.PrefetchScalarGridSpec(
        num_scalar_prefetch=1,
        grid=(b, h),
        in_specs=[pl.BlockSpec((t, dk), lambda bi, hi, lg: (bi, qo + hi)),
                  pl.BlockSpec((t, dk), lambda bi, hi, lg: (bi, ko + hi)),
                  pl.BlockSpec((t, dv), lambda bi, hi, lg: (bi, vo + hi)),
                  pl.BlockSpec((t, dv), lambda bi, hi, lg: (bi, go + hi)),
                  pl.BlockSpec((lc, dk), lambda bi, hi, lg: (bi, kco + hi)),
                  pl.BlockSpec((lc, dv), lambda bi, hi, lg: (bi, vco + hi)),
                  pl.BlockSpec((t, dk), lambda bi, hi, lg: (0, 0)),
                  pl.BlockSpec((t, dk), lambda bi, hi, lg: (0, 0))],
        out_specs=pl.BlockSpec((t, dv), lambda bi, hi, lg: (bi, hi)),
        scratch_shapes=[pltpu.VMEM((t, dk), F32), pltpu.VMEM((t, dk), F32),
                        pltpu.VMEM((t, dv), F32), pltpu.VMEM((2, dk, dv), F32)],
    )
    return pl.pallas_call(
        _ret_kernel,
        grid_spec=grid_spec,
        out_shape=jax.ShapeDtypeStruct((b * t, h * dv), BF16),
        compiler_params=_cparams(("arbitrary", "arbitrary")),
    )(lg, p, p, p, p, pc, pc, tab_c, tab_s)


def _attn_kernel(q_ref, kx_ref, vx_ref, kc_ref, vc_ref, qn_ref, kn_ref, c_ref, sl_ref, sh_ref, o_ref,
                 k_s, v_s):
    qt = pl.program_id(2)
    tq = q_ref.shape[0]
    lc = kc_ref.shape[0]
    hd = ATTN_HEAD_DIM
    scale = hd ** -0.5 * LOG2E

    @pl.when(qt == 0)
    def _():
        kn = kn_ref[...]
        k_s[0:lc, :] = (_rms(kc_ref[...].astype(F32)) * kn).astype(BF16)
        kx = _rms(kx_ref[...].astype(F32)) * kn
        k_s[lc:, :] = _rope128(kx, c_ref[...], sl_ref[...], sh_ref[...]).astype(BF16)
        v_s[0:lc, 0:hd] = vc_ref[...]
        v_s[lc:, 0:hd] = vx_ref[...]
        col = lax.broadcasted_iota(I32, (v_s.shape[0], hd), 1)
        v_s[:, hd:] = jnp.where(col == 0, 1.0, 0.0).astype(BF16)

    qn = qn_ref[...]
    kk = k_s[...]
    vv = v_s[...]
    sub = min(tq, ATTN_SUB)
    for u in range(tq // sub):
        r0 = pl.multiple_of(qt * tq + u * sub, sub)
        c = c_ref[pl.ds(r0, sub), :]
        sl = sl_ref[pl.ds(r0, sub), :]
        sh = sh_ref[pl.ds(r0, sub), :]
        for g in range(ATTN_HEADS // ATTN_KV_HEADS):
            q = _rms(q_ref[u * sub:(u + 1) * sub, g * hd:(g + 1) * hd].astype(F32)) * qn
            q = (_rope128(q, c, sl, sh) * scale).astype(BF16)
            s = lax.dot_general(q, kk, (((1,), (1,)), ((), ())), preferred_element_type=F32)
            m = jnp.max(s, axis=-1, keepdims=True)
            p = jnp.exp2(s - m).astype(BF16)
            ov = jnp.dot(p, vv, preferred_element_type=F32)
            o = ov[:, 0:hd] / ov[:, hd:hd + 1]
            o_ref[u * sub:(u + 1) * sub, g * hd:(g + 1) * hd] = o.astype(o_ref.dtype)


def _attention(p, pc, qn, kn, tabs, b, t, lc, offs, coffs):
    hd, kvh = ATTN_HEAD_DIM, ATTN_KV_HEADS
    gw = (ATTN_HEADS // kvh) * hd
    tq = _tile(t, 256)
    qo, ko, vo = _blk(offs["aq"], gw), _blk(offs["ak"], hd), _blk(offs["av"], hd)
    kco, vco = _blk(coffs["ak"], hd), _blk(coffs["av"], hd)
    tab = pl.BlockSpec((t, hd), lambda bi, ki, qi: (0, 0))
    return pl.pallas_call(
        _attn_kernel,
        grid=(b, kvh, t // tq),
        in_specs=[pl.BlockSpec((tq, gw), lambda bi, ki, qi: (bi * (t // tq) + qi, qo + ki)),
                  pl.BlockSpec((t, hd), lambda bi, ki, qi: (bi, ko + ki)),
                  pl.BlockSpec((t, hd), lambda bi, ki, qi: (bi, vo + ki)),
                  pl.BlockSpec((lc, hd), lambda bi, ki, qi: (bi, kco + ki)),
                  pl.BlockSpec((lc, hd), lambda bi, ki, qi: (bi, vco + ki)),
                  pl.BlockSpec((1, hd), lambda bi, ki, qi: (0, 0)),
                  pl.BlockSpec((1, hd), lambda bi, ki, qi: (0, 0)),
                  tab, tab, tab],
        out_specs=pl.BlockSpec((tq, gw), lambda bi, ki, qi: (bi * (t // tq) + qi, ki)),
        out_shape=jax.ShapeDtypeStruct((b * t, ATTN_HEADS * hd), BF16),
        scratch_shapes=[pltpu.VMEM((lc + t, hd), BF16), pltpu.VMEM((lc + t, 2 * hd), BF16)],
        compiler_params=_cparams(("arbitrary", "arbitrary", "arbitrary")),
    )(p, p, p, pc, pc, qn.reshape(1, hd), kn.reshape(1, hd), *tabs)


def _merge_kernel(r_ref, a_ref, gr_ref, ga_ref, wr_ref, wa_ref, o_ref):
    ret = jnp.dot(r_ref[...], wr_ref[...], preferred_element_type=F32)
    att = jnp.dot(a_ref[...], wa_ref[...], preferred_element_type=F32)
    z = (jax.nn.sigmoid(gr_ref[...].astype(F32)) * ret + jax.nn.sigmoid(ga_ref[...].astype(F32)) * att)
    o_ref[...] = z.astype(o_ref.dtype)


def _merge(retg, att, p, w_ret_o, w_attn_o, offs):
    m, kr = retg.shape
    ka = att.shape[1]
    d = w_ret_o.shape[1]
    tm = _tile(m, 1024)
    tn = functools.reduce(math.gcd, (offs["gr"], offs["ga"], d), 512)
    gro, gao = _blk(offs["gr"], tn), _blk(offs["ga"], tn)
    return pl.pallas_call(
        _merge_kernel,
        grid=(m // tm, d // tn),
        in_specs=[pl.BlockSpec((tm, kr), lambda i, j: (i, 0)),
                  pl.BlockSpec((tm, ka), lambda i, j: (i, 0)),
                  pl.BlockSpec((tm, tn), lambda i, j: (i, gro + j)),
                  pl.BlockSpec((tm, tn), lambda i, j: (i, gao + j)),
                  pl.BlockSpec((kr, tn), lambda i, j: (0, j)),
                  pl.BlockSpec((ka, tn), lambda i, j: (0, j))],
        out_specs=pl.BlockSpec((tm, tn), lambda i, j: (i, j)),
        out_shape=jax.ShapeDtypeStruct((m, d), BF16),
        compiler_params=_cparams(("arbitrary", "arbitrary")),
    )(retg, att, p, p, w_ret_o, w_attn_o)


def _pack_pairs(x):
    hw = x.shape[1] // 2
    lo = pltpu.bitcast(x[:, :hw].astype(BF16).astype(F32), U32)
    hi = pltpu.bitcast(x[:, hw:].astype(BF16).astype(F32), U32)
    return (hi & jnp.uint32(0xFFFF0000)) | (lo >> 16)


def _store_token_tiles(ref, packed):
    m = packed.shape[0]
    st = packed.shape[1] // LANES
    for s in range(st):
        ref[pl.ds(s, m, stride=st), :] = packed[:, s * LANES:(s + 1) * LANES]


def _load_token_tiles(ref, m):
    st = ref.shape[0] // m
    return jnp.concatenate([ref[pl.ds(s, m, stride=st), :] for s in range(st)], axis=1)


def _unpack_pairs(w):
    lo = pltpu.bitcast(w << 16, F32)
    hi = pltpu.bitcast(w & jnp.uint32(0xFFFF0000), F32)
    return lo, hi


def _post_kernel(z_ref, wo_ref, x_ref, g1_ref, pw_ref, fw_ref, sc_ref, sh_ref, rwh_ref, rwl_ref,
                 x1_ref, f_ref, lg_ref):
    tm = x_ref.shape[0]
    sub = min(tm, POST_SUB)
    st = f_ref.shape[0] // tm
    nt = (((1,), (1,)), ((), ()))
    fhs, fls = [], []
    for h in range(tm // sub):
        r0 = h * sub
        y = jnp.dot(z_ref[r0:r0 + sub, :], wo_ref[...], preferred_element_type=F32)
        x1 = x_ref[r0:r0 + sub, :] + g1_ref[0] * (_rms(y) * pw_ref[...])
        x1_ref[r0:r0 + sub, :] = x1
        f = (_rms(x1) * fw_ref[...]) * (1.0 + sc_ref[0]) + sh_ref[0]
        _store_token_tiles(f_ref.at[r0 * st:(r0 + sub) * st, :], _pack_pairs(f))
        fh = f.astype(BF16)
        fhs.append(fh)
        fls.append((f - fh.astype(F32)).astype(BF16))
    fh = jnp.concatenate(fhs, axis=0)
    fl = jnp.concatenate(fls, axis=0)
    lg_ref[...] = (lax.dot_general(rwh_ref[...], fh, nt, preferred_element_type=F32)
                   + lax.dot_general(rwh_ref[...], fl, nt, preferred_element_type=F32)
                   + lax.dot_general(rwl_ref[...], fh, nt, preferred_element_type=F32))


def _post(z, w_out, x2, g1, post_w, pre_w, sc2, sh2, router_wt, rows_per_mod):
    m, d = x2.shape
    e = router_wt.shape[0]
    st = d // 2 // LANES
    assert st == SUBLANES, "token rows must be whole (8, 128) tiles"
    rw_hi = router_wt.astype(BF16)
    rw_lo = (router_wt - rw_hi.astype(F32)).astype(BF16)
    tm = _tile(rows_per_mod, 2 * POST_SUB)
    per = rows_per_mod // tm
    g = g1.shape[0]
    vec = pl.BlockSpec((1, d), lambda i: (0, 0))
    mod = pl.BlockSpec((1, 1, d), lambda i: (i // per, 0, 0))
    once = pl.Buffered(1)
    return pl.pallas_call(
        _post_kernel,
        grid=(m // tm,),
        in_specs=[pl.BlockSpec((tm, d), lambda i: (i, 0)),
                  pl.BlockSpec((d, d), lambda i: (0, 0), pipeline_mode=once),
                  pl.BlockSpec((tm, d), lambda i: (i, 0)),
                  mod, vec, vec, mod, mod,
                  pl.BlockSpec((e, d), lambda i: (0, 0), pipeline_mode=once),
                  pl.BlockSpec((e, d), lambda i: (0, 0), pipeline_mode=once)],
        out_specs=[pl.BlockSpec((tm, d), lambda i: (i, 0)),
                   pl.BlockSpec((tm * st, LANES), lambda i: (i, 0)),
                   pl.BlockSpec((e, tm), lambda i: (0, i))],
        out_shape=[jax.ShapeDtypeStruct((m, d), F32),
                   jax.ShapeDtypeStruct((m * st, LANES), U32),
                   jax.ShapeDtypeStruct((e, m), F32)],
        compiler_params=_cparams(("arbitrary",)),
    )(z, w_out, x2, g1.reshape(g, 1, d), post_w.reshape(1, d), pre_w.reshape(1, d),
      sc2.reshape(g, 1, d), sh2.reshape(g, 1, d), rw_hi, rw_lo)


def _route_kernel(lg_ref, b_ref, tri_ref, idx_ref, w_ref, pos_ref, cnt_ref, carry):
    @pl.when(pl.program_id(0) == 0)
    def _():
        carry[...] = jnp.zeros_like(carry)

    e, tn = lg_ref.shape
    gs = e // N_GROUPS
    neg = -jnp.inf
    big = float(e)
    s = jax.nn.sigmoid(lg_ref[...])
    choice = s + b_ref[...]

    row_g = lax.broadcasted_iota(I32, (gs, tn), 0).astype(F32)
    rows = []
    for g in range(N_GROUPS):
        blk = choice[g * gs:(g + 1) * gs]
        m1 = jnp.max(blk, axis=0, keepdims=True)
        i1 = jnp.min(jnp.where(blk == m1, row_g, big), axis=0, keepdims=True)
        m2 = jnp.max(jnp.where(row_g == i1, neg, blk), axis=0, keepdims=True)
        rows.append(m1 + m2)
    gscore = jnp.concatenate(rows, axis=0)

    row_n = lax.broadcasted_iota(I32, (N_GROUPS, tn), 0).astype(F32)
    sel = jnp.zeros((N_GROUPS, tn), F32)
    cur = gscore
    for _ in range(TOPK_GROUPS):
        m = jnp.max(cur, axis=0, keepdims=True)
        gi = jnp.min(jnp.where(cur == m, row_n, big), axis=0, keepdims=True)
        hit = row_n == gi
        sel = jnp.where(hit, 1.0, sel)
        cur = jnp.where(hit, neg, cur)

    masked = jnp.concatenate(
        [jnp.where(sel[g:g + 1] > 0.0, choice[g * gs:(g + 1) * gs], neg) for g in range(N_GROUPS)], axis=0)

    row_e = lax.broadcasted_iota(I32, (e, tn), 0).astype(F32)
    onehot = jnp.zeros((e, tn), F32)
    cur = masked
    idxs, ws = [], []
    for _ in range(TOP_K):
        m = jnp.max(cur, axis=0, keepdims=True)
        ik = jnp.min(jnp.where(cur == m, row_e, big), axis=0, keepdims=True)
        hit = row_e == ik
        ws.append(jnp.sum(jnp.where(hit, s, 0.0), axis=0, keepdims=True))
        cur = jnp.where(hit, neg, cur)
        onehot = jnp.where(hit, 1.0, onehot)
        idxs.append(ik)
    w = jnp.concatenate(ws, axis=0)
    w = w / jnp.sum(w, axis=0, keepdims=True) * ROUTED_SCALE

    prefix = jnp.dot(onehot.astype(BF16), tri_ref[...], preferred_element_type=F32)
    base = prefix + carry[...]
    pos = jnp.concatenate(
        [jnp.sum(jnp.where(row_e == ik, base, 0.0), axis=0, keepdims=True) for ik in idxs], axis=0)
    carry[...] = carry[...] + jnp.sum(onehot, axis=1, keepdims=True)

    idx_ref[...] = jnp.concatenate(idxs, axis=0).astype(I32)
    w_ref[...] = w
    pos_ref[...] = pos.astype(I32)
    cnt_ref[...] = jnp.broadcast_to(carry[...], cnt_ref.shape).astype(I32)


def _route(logits_t, bias):
    e, n = logits_t.shape
    tn = _tile(n, 512)
    tri = (jnp.arange(tn)[:, None] < jnp.arange(tn)[None, :]).astype(BF16)
    kspec = pl.BlockSpec((TOP_K, tn), lambda i: (0, i))
    return pl.pallas_call(
        _route_kernel,
        grid=(n // tn,),
        in_specs=[pl.BlockSpec((e, tn), lambda i: (0, i)),
                  pl.BlockSpec((e, 1), lambda i: (0, 0)),
                  pl.BlockSpec((tn, tn), lambda i: (0, 0))],
        out_specs=[kspec, kspec, kspec, pl.BlockSpec((e, LANES), lambda i: (0, 0))],
        out_shape=[jax.ShapeDtypeStruct((TOP_K, n), I32), jax.ShapeDtypeStruct((TOP_K, n), F32),
                   jax.ShapeDtypeStruct((TOP_K, n), I32), jax.ShapeDtypeStruct((e, LANES), I32)],
        scratch_shapes=[pltpu.VMEM((e, 1), F32)],
        compiler_params=_cparams(("arbitrary",)),
    )(logits_t, bias.reshape(e, 1), tri)


def _dest_kernel(idx_ref, pos_ref, off_ref, o_ref):
    e = off_ref.shape[0]
    tn = idx_ref.shape[1]
    row_e = lax.broadcasted_iota(I32, (e, tn), 0)
    off = off_ref[...]
    idx = idx_ref[...]
    rows = [jnp.sum(jnp.where(row_e == idx[k:k + 1], off, 0.0), axis=0, keepdims=True) for k in range(TOP_K)]
    o_ref[...] = jnp.concatenate(rows, axis=0).astype(I32) + pos_ref[...]


def _dest(top_idx, pos, pad_off):
    k, n = top_idx.shape
    e = pad_off.shape[0]
    tn = _tile(n, 512)
    spec = pl.BlockSpec((k, tn), lambda i: (0, i))
    return pl.pallas_call(
        _dest_kernel,
        grid=(n // tn,),
        in_specs=[spec, spec, pl.BlockSpec((e, 1), lambda i: (0, 0))],
        out_specs=spec,
        out_shape=jax.ShapeDtypeStruct((k, n), I32),
        compiler_params=_cparams(("arbitrary",)),
    )(top_idx, pos, pad_off.astype(F32).reshape(e, 1))


def _dispatch_kernel(cnt_ref, off_ref, nu_ref, dest_ref, f_ref, xs_hbm, zbuf, sem, zsem):
    i = pl.program_id(0)
    tt = dest_ref.shape[1]
    st = SUBLANES
    bm = zbuf.shape[0] // st
    n_exp = cnt_ref.shape[0]
    nb = xs_hbm.shape[0] // (bm * st)

    def zero_fill(wait):
        def fill(start, rows):
            cp = pltpu.make_async_copy(zbuf.at[pl.ds(0, rows * st), :],
                                       xs_hbm.at[pl.ds(pl.multiple_of(start * st, st), rows * st), :], zsem)
            if wait:
                cp.wait()
            else:
                cp.start()

        def per_expert(e, c):
            cnt = cnt_ref[e]
            npad = (-cnt) & (bm - 1)
            start = off_ref[e] + cnt
            p = bm // 2
            while p >= 1:
                @pl.when((npad & p) != 0)
                def _(start=start, p=p):
                    fill(start, p)
                start = start + (npad & p)
                p //= 2
            return c
        lax.fori_loop(0, n_exp, per_expert, 0)

        def per_block(b, c):
            fill(b * bm, bm)
            return c
        lax.fori_loop(nu_ref[0], nb, per_block, 0)

    def row_copy(j, dst):
        return pltpu.make_async_copy(f_ref.at[pl.ds(pl.multiple_of(j * st, st), st), :],
                                     xs_hbm.at[pl.ds(pl.multiple_of(dst * st, st), st), :], sem)

    @pl.when(i == 0)
    def _():
        zbuf[...] = jnp.zeros_like(zbuf)
        zero_fill(False)

    def issue(j, c):
        base = j * DMA_GROUP
        dsts = [[dest_ref[k, base + u] for k in range(TOP_K)] for u in range(DMA_GROUP)]
        for u in range(DMA_GROUP):
            for k in range(TOP_K):
                row_copy(base + u, dsts[u][k]).start(priority=k % 2)

        @pl.when(j >= DMA_LAG)
        def _():
            wait_group()
        return c

    def wait_group():
        for _ in range(DMA_GROUP * TOP_K):
            row_copy(0, 0).wait()

    n_groups = tt // DMA_GROUP
    lax.fori_loop(0, n_groups, issue, 0)

    def drain(j, c):
        wait_group()
        return c
    lax.fori_loop(0, min(DMA_LAG, n_groups), drain, 0)

    @pl.when(i == pl.num_programs(0) - 1)
    def _():
        zero_fill(True)


def _dispatch(counts, pad_off, n_used, dest, f_packed, n_slots):
    st = SUBLANES
    n = f_packed.shape[0] // st
    tt = _tile(n, 512)
    grid_spec = pltpu.PrefetchScalarGridSpec(
        num_scalar_prefetch=3,
        grid=(n // tt,),
        in_specs=[pl.BlockSpec((TOP_K, tt), lambda i, c, o, u: (0, i), memory_space=pltpu.SMEM),
                  pl.BlockSpec((tt * st, LANES), lambda i, c, o, u: (i, 0))],
        out_specs=pl.BlockSpec(memory_space=pl.ANY),
        scratch_shapes=[pltpu.VMEM((MOE_BLOCK * st, LANES), U32), pltpu.SemaphoreType.DMA(()),
                        pltpu.SemaphoreType.DMA(())],
    )
    return pl.pallas_call(
        _dispatch_kernel,
        grid_spec=grid_spec,
        out_shape=jax.ShapeDtypeStruct((n_slots * st, LANES), U32),
        compiler_params=_cparams(("arbitrary",)),
    )(counts, pad_off, n_used, dest, f_packed)


def _cast_rows(src, dst):
    rows = src.shape[0]
    ch = CAST_ROWS * LANES // src.shape[1]

    def body(j, c):
        r0 = pl.multiple_of(j * ch, ch)
        dst[pl.ds(r0, ch), :] = src[pl.ds(r0, ch), :].astype(dst.dtype)
        return c
    lax.fori_loop(0, rows // ch, body, 0, unroll=4)


def _expert_kernel(ob_ref, bv_ref, act_ref, na_ref, nu_ref, x_ref, w1_hbm, w3_hbm, w2_hbm, y_ref,
                   w1f, w3f, w2f, w1b, w3b, w2b, wsem):
    b = pl.program_id(0)
    n_used = nu_ref[0]
    n_act = na_ref[0]
    bm = MOE_BLOCK
    hw = w1b.shape[0] // 2
    st = x_ref.shape[0] // bm
    ring = w1f.shape[0]

    def fetch(i, wait):
        e = act_ref[i]
        slot = i % ring
        for src, dst in ((w1_hbm, w1f), (w3_hbm, w3f), (w2_hbm, w2f)):
            cp = pltpu.make_async_copy(src.at[e], dst.at[slot], wsem.at[slot])
            if wait:
                cp.wait()
            else:
                cp.start(priority=1)

    @pl.when(b < n_used)
    def _():
        i = ob_ref[b]
        first = jnp.logical_or(b == 0, i != ob_ref[jnp.maximum(b - 1, 0)])

        @pl.when(b == 0)
        def _():
            fetch(0, False)
            for ahead in range(1, ring):
                @pl.when(n_act > ahead)
                def _(ahead=ahead):
                    fetch(ahead, False)

        @pl.when(first)
        def _():
            fetch(i, True)
            slot = i % ring
            for src, dst in ((w1f, w1b), (w3f, w3b), (w2f, w2b)):
                _cast_rows(src.at[slot], dst)

            @pl.when(i + ring < n_act)
            def _():
                fetch(i + ring, False)

        def ffn(rows):
            lo, hi = _unpack_pairs(_load_token_tiles(x_ref.at[0:rows * st, :], rows))
            lo = lo.astype(BF16)
            hi = hi.astype(BF16)
            h1 = (jnp.dot(lo, w1b[0:hw, :], preferred_element_type=F32)
                  + jnp.dot(hi, w1b[hw:, :], preferred_element_type=F32))
            h3 = (jnp.dot(lo, w3b[0:hw, :], preferred_element_type=F32)
                  + jnp.dot(hi, w3b[hw:, :], preferred_element_type=F32))
            hid = (h1 * jax.nn.sigmoid(h1) * h3).astype(BF16)
            _store_token_tiles(y_ref.at[0:rows * st, :],
                               _pack_pairs(jnp.dot(hid, w2b[...], preferred_element_type=F32)))

        half = bm // 2
        valid = bv_ref[b]

        @pl.when(valid > half)
        def _():
            ffn(bm)

        @pl.when(valid <= half)
        def _():
            ffn(half)
            y_ref[half * st:, :] = x_ref[half * st:, :]


def _experts(xs, blk_ord, blk_valid, active, n_active, n_used, w1, w3, w2):
    nb = blk_ord.shape[0]
    e, d, fdim = w1.shape
    rows = MOE_BLOCK * (d // 2 // LANES)
    w_elems = 3 * d * fdim
    vmem_limit = (W_BUFS * w_elems * 4 + w_elems * 2 + 4 * rows * LANES * 4
                  + MOE_BLOCK * (2 * fdim + 2 * d) * 4 + VMEM_SLACK)
    assert vmem_limit <= V7X_VMEM_BYTES
    grid_spec = pltpu.PrefetchScalarGridSpec(
        num_scalar_prefetch=5,
        grid=(nb,),
        in_specs=[pl.BlockSpec((rows, LANES), lambda b, ob, bv, ac, na, nu: (jnp.minimum(b, nu[0] - 1), 0)),
                  pl.BlockSpec(memory_space=pl.ANY),
                  pl.BlockSpec(memory_space=pl.ANY),
                  pl.BlockSpec(memory_space=pl.ANY)],
        out_specs=pl.BlockSpec((rows, LANES), lambda b, ob, bv, ac, na, nu: (jnp.minimum(b, nu[0] - 1), 0)),
        scratch_shapes=[pltpu.VMEM((W_BUFS, d, fdim), F32), pltpu.VMEM((W_BUFS, d, fdim), F32),
                        pltpu.VMEM((W_BUFS, fdim, d), F32),
                        pltpu.VMEM((d, fdim), BF16), pltpu.VMEM((d, fdim), BF16), pltpu.VMEM((fdim, d), BF16),
                        pltpu.SemaphoreType.DMA((W_BUFS,))],
    )
    return pl.pallas_call(
        _expert_kernel,
        grid_spec=grid_spec,
        out_shape=jax.ShapeDtypeStruct((nb * rows, LANES), U32),
        input_output_aliases={5: 0},
        compiler_params=_cparams(("arbitrary",), vmem_limit),
    )(blk_ord, blk_valid, active, n_active, n_used, xs, w1, w3, w2)


def _combine_kernel(dcur_ref, dnxt_ref, wt_ref, f_ref, x1_ref, g2_ref, pw_ref, s1_ref, s3_ref, s2_ref, ys_hbm,
                    o_ref, ybuf, sem):
    i = pl.program_id(0)
    tm = x1_ref.shape[0]
    hw = x1_ref.shape[1] // 2
    st = f_ref.shape[0] // tm
    slot = i % 2

    def row_copy(src, k, r, s):
        return pltpu.make_async_copy(ys_hbm.at[pl.ds(pl.multiple_of(src * st, st), st), :],
                                     ybuf.at[s, k, pl.ds(pl.multiple_of(r * st, st), st), :], sem.at[s])

    def start_gather(d_ref, s):
        def body(j, c):
            base = j * DMA_GROUP
            srcs = [[d_ref[k, base + u] for k in range(TOP_K)] for u in range(DMA_GROUP)]
            for u in range(DMA_GROUP):
                for k in range(TOP_K):
                    row_copy(srcs[u][k], k, base + u, s).start(priority=k % 2)
            return c
        lax.fori_loop(0, tm // DMA_GROUP, body, 0)

    def wait_gather(s):
        def body(r, c):
            for k in range(TOP_K):
                row_copy(0, k, r, s).wait()
            return c
        lax.fori_loop(0, tm, body, 0, unroll=2)

    @pl.when(i == 0)
    def _():
        start_gather(dcur_ref, 0)

    @pl.when(i + 1 < pl.num_programs(0))
    def _():
        start_gather(dnxt_ref, 1 - slot)

    lo, hi = _unpack_pairs(_load_token_tiles(f_ref, tm))
    lo = lo.astype(BF16)
    hi = hi.astype(BF16)
    h1 = (jnp.dot(lo, s1_ref[0:hw, :], preferred_element_type=F32)
          + jnp.dot(hi, s1_ref[hw:, :], preferred_element_type=F32))
    h3 = (jnp.dot(lo, s3_ref[0:hw, :], preferred_element_type=F32)
          + jnp.dot(hi, s3_ref[hw:, :], preferred_element_type=F32))
    shared = jnp.dot((h1 * jax.nn.sigmoid(h1) * h3).astype(BF16), s2_ref[...], preferred_element_type=F32)

    wait_gather(slot)
    wt = wt_ref[...]
    r_lo = shared[:, :hw]
    r_hi = shared[:, hw:]
    for k in range(TOP_K):
        a, c = _unpack_pairs(_load_token_tiles(ybuf.at[slot, k], tm))
        w = wt[:, k:k + 1]
        r_lo = r_lo + a * w
        r_hi = r_hi + c * w
    moe = jnp.concatenate([r_lo, r_hi], axis=1)
    o_ref[...] = x1_ref[...] + g2_ref[0] * (_rms(moe) * pw_ref[...])


def _combine(ys, dest, top_wt, f_packed, x1, g2, post_w, s1, s3, s2, rows_per_mod):
    m, d = x1.shape
    st = d // 2 // LANES
    fdim = s1.shape[1]
    tm = _tile(rows_per_mod, 256)
    per = rows_per_mod // tm
    g = g2.shape[0]
    nblk = m // tm
    smem = functools.partial(pl.BlockSpec, memory_space=pltpu.SMEM)
    return pl.pallas_call(
        _combine_kernel,
        grid=(nblk,),
        in_specs=[smem((TOP_K, tm), lambda i: (0, i)),
                  smem((TOP_K, tm), lambda i: (0, jnp.minimum(i + 1, nblk - 1))),
                  pl.BlockSpec((tm, TOP_K), lambda i: (i, 0)),
                  pl.BlockSpec((tm * st, LANES), lambda i: (i, 0)),
                  pl.BlockSpec((tm, d), lambda i: (i, 0)),
                  pl.BlockSpec((1, 1, d), lambda i: (i // per, 0, 0)),
                  pl.BlockSpec((1, d), lambda i: (0, 0)),
                  pl.BlockSpec((d, fdim), lambda i: (0, 0)),
                  pl.BlockSpec((d, fdim), lambda i: (0, 0)),
                  pl.BlockSpec((fdim, d), lambda i: (0, 0)),
                  pl.BlockSpec(memory_space=pl.ANY)],
        out_specs=pl.BlockSpec((tm, d), lambda i: (i, 0)),
        out_shape=jax.ShapeDtypeStruct((m, d), F32),
        scratch_shapes=[pltpu.VMEM((2, TOP_K, tm * st, LANES), U32), pltpu.SemaphoreType.DMA((2,))],
        compiler_params=_cparams(("arbitrary",)),
    )(dest, dest, top_wt, f_packed, x1, g2.reshape(g, 1, d), post_w.reshape(1, d), s1, s3, s2, ys)


def _layer(x, c, ctx, c_ctx, ada_w, ada_b, mix_pre_norm, mix_post_norm, ffn_pre_norm, ffn_post_norm,
           w_in, ret_decay_fwd, ret_decay_bwd, attn_q_norm, attn_k_norm, w_ret_o, w_attn_o, w_out,
           router_w, router_b, exp_w1, exp_w3, exp_w2, shared_w1, shared_w3, shared_w2):
    b, t, d = x.shape
    lc = ctx.shape[1]
    n = b * t
    rq_w, rv_w = RET_HEADS * RET_QK_DIM, RET_HEADS * RET_V_DIM
    aq_w, akv_w = ATTN_HEADS * ATTN_HEAD_DIM, ATTN_KV_HEADS * ATTN_HEAD_DIM
    names = ("rq", "rk", "rv", "rg", "aq", "ak", "av", "gr", "ga")
    widths = (rq_w, rq_w, rv_w, rv_w, aq_w, akv_w, akv_w, d, d)
    offs, o = {}, 0
    for nm, wd in zip(names, widths):
        offs[nm] = o
        o += wd

    rows = -(-(b + 1) // 16) * 16
    cond = jnp.zeros((rows, d), F32).at[:b].set(c).at[b].set(c_ctx)
    mod = _adaln(cond, ada_w, ada_b)
    sh1, sc1, g1, sh2, sc2, g2 = [mod[:b, i * d:(i + 1) * d] for i in range(6)]
    ch1, cs1 = mod[b:b + 1, 0:d], mod[b:b + 1, d:2 * d]

    x2 = x.reshape(n, d)
    h_x = _norm_mod(x2, mix_pre_norm, sc1, sh1, t)
    h_c = _norm_mod(ctx.reshape(b * lc, d), mix_pre_norm, cs1, ch1, b * lc)

    tn = _tile(w_in.shape[1], 1024)
    p = _project(h_x, w_in, list(range(w_in.shape[1] // tn)), tn, BF16)
    cnames = ("rv", "rk", "ak", "av")
    tnc = functools.reduce(math.gcd, [offs[nm] for nm in cnames] + [widths[names.index(nm)] for nm in cnames], tn)
    coffs, ctiles = {}, []
    for nm in cnames:
        coffs[nm] = len(ctiles) * tnc
        ctiles += list(range(offs[nm] // tnc, (offs[nm] + widths[names.index(nm)]) // tnc))
    pc = _project(h_c, w_in, ctiles, tnc, BF16)

    lg = jnp.stack([jax.nn.log_sigmoid(ret_decay_fwd.astype(F32)),
                    jax.nn.log_sigmoid(ret_decay_bwd.astype(F32))])
    rc, rs_lo, rs_hi = _rope_tables(t, RET_QK_DIM)
    retg = _retention(p, pc, lg, rc, rs_lo + rs_hi, b, t, lc, offs, coffs)
    att = _attention(p, pc, attn_q_norm, attn_k_norm, _rope_tables(t, ATTN_HEAD_DIM), b, t, lc, offs, coffs)

    z = _merge(retg, att, p, w_ret_o.astype(BF16), w_attn_o.astype(BF16), offs)
    x1, f_packed, logits_t = _post(z, w_out.astype(BF16), x2, g1, mix_post_norm, ffn_pre_norm, sc2, sh2,
                                   router_w.T, t)

    top_idx, top_w, pos, cnt = _route(logits_t, router_b)

    bm = MOE_BLOCK
    e = router_w.shape[1]
    nb = (n * TOP_K) // bm + e
    counts = cnt[:, 0]
    padded = (counts + bm - 1) // bm * bm
    pad_end = jnp.cumsum(padded)
    pad_off = pad_end - padded
    blk_start = jnp.arange(nb, dtype=I32) * bm
    blk_expert = jnp.minimum(jnp.sum((pad_end[None, :] <= blk_start[:, None]).astype(I32), axis=1), e - 1)
    n_used = (pad_end[-1] // bm).astype(I32).reshape(1)
    is_act = (counts > 0).astype(I32)
    exp_ord = jnp.cumsum(is_act) - is_act
    n_active = jnp.sum(is_act).astype(I32).reshape(1)
    eids = jnp.arange(e, dtype=I32)
    active = jnp.sum(jnp.where((exp_ord[None, :] == eids[:, None]) & (is_act[None, :] > 0), eids[None, :], 0),
                     axis=1).astype(I32)
    blk_ord = exp_ord[blk_expert].astype(I32)
    blk_in_exp = jnp.arange(nb, dtype=I32) - (pad_off // bm)[blk_expert]
    blk_valid = jnp.clip(counts[blk_expert] - blk_in_exp * bm, 0, bm).astype(I32)

    dest = _dest(top_idx, pos, pad_off)
    xs = _dispatch(counts, pad_off, n_used, dest, f_packed, nb * bm)
    ys = _experts(xs, blk_ord, blk_valid, active, n_active, n_used, exp_w1, exp_w3, exp_w2)
    out = _combine(ys, dest, top_w.T, f_packed, x1, g2, ffn_post_norm, shared_w1.astype(BF16),
                   shared_w3.astype(BF16), shared_w2.astype(BF16), t)
    return out.reshape(b, t, d)


def kernel(x, c, ctx, c_ctx, ada_w, ada_b, mix_pre_norm, mix_post_norm, ffn_pre_norm, ffn_post_norm,
           w_in, ret_decay_fwd, ret_decay_bwd, attn_q_norm, attn_k_norm, w_ret_o, w_attn_o, w_out,
           router_w, router_b, exp_w1, exp_w3, exp_w2, shared_w1, shared_w3, shared_w2):
    assert ada_w.shape[0] == 1, "single-layer operation"
    return _layer(x, c, ctx, c_ctx, ada_w[0], ada_b[0], mix_pre_norm[0], mix_post_norm[0], ffn_pre_norm[0],
                  ffn_post_norm[0], w_in[0], ret_decay_fwd[0], ret_decay_bwd[0], attn_q_norm[0],
                  attn_k_norm[0], w_ret_o[0], w_attn_o[0], w_out[0], router_w[0], router_b[0],
                  exp_w1[0], exp_w3[0], exp_w2[0], shared_w1[0], shared_w3[0], shared_w2[0])
```

```python
import functools
import math

import jax
import jax.numpy as jnp
import numpy as np
from jax import lax
from jax.experimental import pallas as pl
from jax.experimental.pallas import tpu as pltpu

F32 = jnp.float32
BF16 = jnp.bfloat16
U32 = jnp.uint32
I32 = jnp.int32

GRID_W = 64
NORM_EPS = 1e-6
ROPE_THETA = 10000.0
RET_HEADS = 4
RET_QK_DIM = 256
RET_V_DIM = 512
ATTN_HEADS = 16
ATTN_KV_HEADS = 4
ATTN_HEAD_DIM = 128
N_EXPERTS = 256
TOP_K = 8
N_GROUPS = 8
TOPK_GROUPS = 4
EXPERT_DIM = 512
ROUTED_SCALE = 2.5
LOG2E = 1.4426950408889634

V7X_VMEM_BYTES = 64 * 1024 * 1024
VMEM_LIMIT = V7X_VMEM_BYTES - 12 * 1024 * 1024
VMEM_SLACK = 4 * 1024 * 1024
LANES = 128
SUBLANES = 8

RET_CHUNK = 256
MOE_BLOCK = 256
DMA_GROUP = 4
DMA_LAG = 16
CAST_ROWS = 256
POST_SUB = 256
ATTN_SUB = 128
W_BUFS = 3


def _cparams(sem, vmem_limit=VMEM_LIMIT):
    return pltpu.CompilerParams(dimension_semantics=sem, vmem_limit_bytes=vmem_limit)


def _tile(dim, pref):
    if dim <= pref:
        return dim
    for t in range(pref, 0, -LANES):
        if dim % t == 0:
            return t
    raise ValueError((dim, pref))


def _blk(off, width):
    assert off % width == 0, (off, width)
    return off // width


def _ada_kernel(s_ref, w_ref, b_ref, o_ref):
    s = s_ref[...]
    s = (s * jax.nn.sigmoid(s)).astype(BF16)
    o_ref[...] = jnp.dot(s, w_ref[...].astype(BF16), preferred_element_type=F32) + b_ref[...]


def _adaln(cond, w, b):
    r, d = cond.shape
    n = w.shape[1]
    tn = _tile(n, 1024)
    return pl.pallas_call(
        _ada_kernel,
        grid=(n // tn,),
        in_specs=[pl.BlockSpec((r, d), lambda j: (0, 0)),
                  pl.BlockSpec((d, tn), lambda j: (0, j)),
                  pl.BlockSpec((1, tn), lambda j: (0, j))],
        out_specs=pl.BlockSpec((r, tn), lambda j: (0, j)),
        out_shape=jax.ShapeDtypeStruct((r, n), F32),
        compiler_params=_cparams(("arbitrary",)),
    )(cond, w, b.reshape(1, n))


def _rms(x):
    return x * lax.rsqrt(jnp.mean(x * x, axis=-1, keepdims=True) + NORM_EPS)


def _norm_mod_kernel(x_ref, w_ref, sc_ref, sh_ref, o_ref):
    y = _rms(x_ref[...]) * w_ref[...]
    o_ref[...] = (y * (1.0 + sc_ref[0]) + sh_ref[0]).astype(o_ref.dtype)


def _norm_mod(x2, w, sc, sh, rows_per_mod):
    m, d = x2.shape
    tm = _tile(rows_per_mod, 512)
    per = rows_per_mod // tm
    g = sc.shape[0]
    return pl.pallas_call(
        _norm_mod_kernel,
        grid=(m // tm,),
        in_specs=[pl.BlockSpec((tm, d), lambda i: (i, 0)),
                  pl.BlockSpec((1, d), lambda i: (0, 0)),
                  pl.BlockSpec((1, 1, d), lambda i: (i // per, 0, 0)),
                  pl.BlockSpec((1, 1, d), lambda i: (i // per, 0, 0))],
        out_specs=pl.BlockSpec((tm, d), lambda i: (i, 0)),
        out_shape=jax.ShapeDtypeStruct((m, d), BF16),
        compiler_params=_cparams(("arbitrary",)),
    )(x2, w.reshape(1, d), sc.reshape(g, 1, d), sh.reshape(g, 1, d))


def _proj_kernel(cols_ref, a_ref, w_ref, o_ref, wb):
    @pl.when(pl.program_id(1) == 0)
    def _():
        _cast_rows(w_ref, wb)

    o_ref[...] = jnp.dot(a_ref[...], wb[...], preferred_element_type=F32).astype(o_ref.dtype)


def _project(a, w, col_tiles, tn, out_dtype):
    m, k = a.shape
    tm = _tile(m, 1024)
    nj = len(col_tiles)
    grid_spec = pltpu.PrefetchScalarGridSpec(
        num_scalar_prefetch=1,
        grid=(nj, m // tm),
        in_specs=[pl.BlockSpec((tm, k), lambda j, i, c: (i, 0)),
                  pl.BlockSpec((k, tn), lambda j, i, c: (0, c[j]))],
        out_specs=pl.BlockSpec((tm, tn), lambda j, i, c: (i, j)),
        scratch_shapes=[pltpu.VMEM((k, tn), BF16)],
    )
    return pl.pallas_call(
        _proj_kernel,
        grid_spec=grid_spec,
        out_shape=jax.ShapeDtypeStruct((m, nj * tn), out_dtype),
        compiler_params=_cparams(("arbitrary", "arbitrary")),
    )(jnp.asarray(col_tiles, I32), a, w)


def _rope_tables(t, dim):
    half = dim // 2
    pos = jnp.arange(t, dtype=I32)
    row = (pos // GRID_W).astype(F32)
    col = (pos % GRID_W).astype(F32)
    inv = ROPE_THETA ** (-jnp.arange(0, half, 2, dtype=F32) / half)
    ar = row[:, None] * inv[None, :]
    ac = col[:, None] * inv[None, :]
    cr, sr, cc, sc = jnp.cos(ar), jnp.sin(ar), jnp.cos(ac), jnp.sin(ac)
    z = jnp.zeros_like(sr)
    c = jnp.concatenate([cr, cr, cc, cc], axis=1)
    s_lo = jnp.concatenate([-sr, z, -sc, z], axis=1)
    s_hi = jnp.concatenate([z, sr, z, sc], axis=1)
    return c, s_lo, s_hi


def _rope128(x, c, s_lo, s_hi):
    q4 = ATTN_HEAD_DIM // 4
    return (x * c + pltpu.roll(x, LANES - q4, axis=1) * s_lo + pltpu.roll(x, q4, axis=1) * s_hi)


def _rope256(x, c, s):
    parts = [pltpu.roll(x[:, i * LANES:(i + 1) * LANES], LANES // 2, axis=1)
             for i in range(RET_QK_DIM // LANES)]
    return x * c + jnp.concatenate(parts, axis=1) * s


def _ret_kernel(lg_ref, q_ref, k_ref, v_ref, g_ref, kc_ref, vc_ref, c_ref, s_ref, o_ref,
                qs, ks, acc, st):
    h = pl.program_id(1)
    lgf = lg_ref[0, h]
    lgb = lg_ref[1, h]
    t = q_ref.shape[0]
    ch = RET_CHUNK if t % RET_CHUNK == 0 else t
    nch = t // ch
    lc = kc_ref.shape[0]
    k_scale = RET_QK_DIM ** -0.5
    tn_dims = (((0,), (0,)), ((), ()))

    ii = lax.broadcasted_iota(I32, (ch, ch), 0)
    jj = lax.broadcasted_iota(I32, (ch, ch), 1)
    dif = (ii - jj).astype(F32)
    dmat = (jnp.where(dif >= 0, jnp.exp(lgf * jnp.maximum(dif, 0.0)), 0.0)
            + jnp.where(dif <= 0, jnp.exp(lgb * jnp.maximum(-dif, 0.0)), 0.0))
    pos = lax.broadcasted_iota(I32, (ch, 1), 0).astype(F32)
    qf_dec = jnp.exp(lgf * (pos + 1.0))
    kf_dec = jnp.exp(lgf * (ch - 1.0 - pos))
    qb_dec = jnp.exp(lgb * (ch - pos))
    kb_dec = jnp.exp(lgb * pos)
    chv = jnp.full((1, 1), float(ch), F32)
    cf_dec = jnp.exp(lgf * chv)
    cb_dec = jnp.exp(lgb * chv)

    cpos = lax.broadcasted_iota(I32, (lc, 1), 0).astype(F32)
    kc = kc_ref[...].astype(F32) * k_scale
    vc = vc_ref[...]
    st[0] = lax.dot_general((kc * jnp.exp(lgf * (lc - 1.0 - cpos))).astype(BF16), vc, tn_dims,
                            preferred_element_type=F32)
    st[1] = lax.dot_general((kc * jnp.exp(lgb * cpos)).astype(BF16), vc, tn_dims,
                            preferred_element_type=F32)

    def fwd(c, carry):
        r0 = pl.multiple_of(c * ch, ch)
        cs = c_ref[pl.ds(r0, ch), :]
        sn = s_ref[pl.ds(r0, ch), :]
        q = _rope256(q_ref[pl.ds(r0, ch), :].astype(F32), cs, sn)
        k = _rope256(k_ref[pl.ds(r0, ch), :].astype(F32), cs, sn) * k_scale
        v = v_ref[pl.ds(r0, ch), :]
        qb = q.astype(BF16)
        kb = k.astype(BF16)
        qs[pl.ds(r0, ch), :] = q
        ks[pl.ds(r0, ch), :] = k
        sc = lax.dot_general(qb, kb, (((1,), (1,)), ((), ())), preferred_element_type=F32) * dmat
        o = jnp.dot(sc.astype(BF16), v, preferred_element_type=F32)
        o += jnp.dot((q * qf_dec).astype(BF16), st[0].astype(BF16), preferred_element_type=F32)
        acc[pl.ds(r0, ch), :] = o
        st[0] = st[0] * cf_dec + lax.dot_general((k * kf_dec).astype(BF16), v, tn_dims,
                                                 preferred_element_type=F32)
        return carry

    lax.fori_loop(0, nch, fwd, 0, unroll=2)

    def bwd(i, carry):
        c = nch - 1 - i
        r0 = pl.multiple_of(c * ch, ch)
        q = qs[pl.ds(r0, ch), :]
        k = ks[pl.ds(r0, ch), :]
        v = v_ref[pl.ds(r0, ch), :]
        o = jnp.dot((q * qb_dec).astype(BF16), st[1].astype(BF16), preferred_element_type=F32)
        tot = acc[pl.ds(r0, ch), :] + o
        gate = g_ref[pl.ds(r0, ch), :].astype(F32)
        o_ref[pl.ds(r0, ch), :] = (gate * jax.nn.sigmoid(gate) * _rms(tot)).astype(o_ref.dtype)
        st[1] = st[1] * cb_dec + lax.dot_general((k * kb_dec).astype(BF16), v, tn_dims,
                                                 preferred_element_type=F32)
        return carry

    lax.fori_loop(0, nch, bwd, 0, unroll=2)


def _retention(p, pc, lg, tab_c, tab_s, b, t, lc, offs, coffs):
    h, dk, dv = RET_HEADS, RET_QK_DIM, RET_V_DIM
    qo, ko, vo, go = (_blk(offs["rq"], dk), _blk(offs["rk"], dk), _blk(offs["rv"], dv), _blk(offs["rg"], dv))
    kco, vco = _blk(coffs["rk"], dk), _blk(coffs["rv"], dv)
    grid_spec = pltpu.PrefetchScalarGridSpec(
        num_scalar_prefetch=1,
        grid=(b, h),
        in_specs=[pl.BlockSpec((t, dk), lambda bi, hi, lg: (bi, qo + hi)),
                  pl.BlockSpec((t, dk), lambda bi, hi, lg: (bi, ko + hi)),
                  pl.BlockSpec((t, dv), lambda bi, hi, lg: (bi, vo + hi)),
                  pl.BlockSpec((t, dv), lambda bi, hi, lg: (bi, go + hi)),
                  pl.BlockSpec((lc, dk), lambda bi, hi, lg: (bi, kco + hi)),
                  pl.BlockSpec((lc, dv), lambda bi, hi, lg: (bi, vco + hi)),
                  pl.BlockSpec((t, dk), lambda bi, hi, lg: (0, 0)),
                  pl.BlockSpec((t, dk), lambda bi, hi, lg: (0, 0))],
        out_specs=pl.BlockSpec((t, dv), lambda bi, hi, lg: (bi, hi)),
        scratch_shapes=[pltpu.VMEM((t, dk), F32), pltpu.VMEM((t, dk), F32),
                        pltpu.VMEM((t, dv), F32), pltpu.VMEM((2, dk, dv), F32)],
    )
    return pl.pallas_call(
        _ret_kernel,
        grid_spec=grid_spec,
        out_shape=jax.ShapeDtypeStruct((b * t, h * dv), BF16),
        compiler_params=_cparams(("arbitrary", "arbitrary")),
    )(lg, p, p, p, p, pc, pc, tab_c, tab_s)


def _attn_kernel(q_ref, kx_ref, vx_ref, kc_ref, vc_ref, qn_ref, kn_ref, c_ref, sl_ref, sh_ref, o_ref,
                 k_s, v_s):
    qt = pl.program_id(2)
    tq = q_ref.shape[0]
    lc = kc_ref.shape[0]
    hd = ATTN_HEAD_DIM
    scale = hd ** -0.5 * LOG2E

    @pl.when(qt == 0)
    def _():
        kn = kn_ref[...]
        k_s[0:lc, :] = (_rms(kc_ref[...].astype(F32)) * kn).astype(BF16)
        kx = _rms(kx_ref[...].astype(F32)) * kn
        k_s[lc:, :] = _rope128(kx, c_ref[...], sl_ref[...], sh_ref[...]).astype(BF16)
        v_s[0:lc, 0:hd] = vc_ref[...]
        v_s[lc:, 0:hd] = vx_ref[...]
        col = lax.broadcasted_iota(I32, (v_s.shape[0], hd), 1)
        v_s[:, hd:] = jnp.where(col == 0, 1.0, 0.0).astype(BF16)

    qn = qn_ref[...]
    kk = k_s[...]
    vv = v_s[...]
    sub = min(tq, ATTN_SUB)
    for u in range(tq // sub):
        r0 = pl.multiple_of(qt * tq + u * sub, sub)
        c = c_ref[pl.ds(r0, sub), :]
        sl = sl_ref[pl.ds(r0, sub), :]
        sh = sh_ref[pl.ds(r0, sub), :]
        for g in range(ATTN_HEADS // ATTN_KV_HEADS):
            q = _rms(q_ref[u * sub:(u + 1) * sub, g * hd:(g + 1) * hd].astype(F32)) * qn
            q = (_rope128(q, c, sl, sh) * scale).astype(BF16)
            s = lax.dot_general(q, kk, (((1,), (1,)), ((), ())), preferred_element_type=F32)
            m = jnp.max(s, axis=-1, keepdims=True)
            p = jnp.exp2(s - m).astype(BF16)
            ov = jnp.dot(p, vv, preferred_element_type=F32)
            o = ov[:, 0:hd] / ov[:, hd:hd + 1]
            o_ref[u * sub:(u + 1) * sub, g * hd:(g + 1) * hd] = o.astype(o_ref.dtype)


def _attention(p, pc, qn, kn, tabs, b, t, lc, offs, coffs):
    hd, kvh = ATTN_HEAD_DIM, ATTN_KV_HEADS
    gw = (ATTN_HEADS // kvh) * hd
    tq = _tile(t, 256)
    qo, ko, vo = _blk(offs["aq"], gw), _blk(offs["ak"], hd), _blk(offs["av"], hd)
    kco, vco = _blk(coffs["ak"], hd), _blk(coffs["av"], hd)
    tab = pl.BlockSpec((t, hd), lambda bi, ki, qi: (0, 0))
    return pl.pallas_call(
        _attn_kernel,
        grid=(b, kvh, t // tq),
        in_specs=[pl.BlockSpec((tq, gw), lambda bi, ki, qi: (bi * (t // tq) + qi, qo + ki)),
                  pl.BlockSpec((t, hd), lambda bi, ki, qi: (bi, ko + ki)),
                  pl.BlockSpec((t, hd), lambda bi, ki, qi: (bi, vo + ki)),
                  pl.BlockSpec((lc, hd), lambda bi, ki, qi: (bi, kco + ki)),
                  pl.BlockSpec((lc, hd), lambda bi, ki, qi: (bi, vco + ki)),
                  pl.BlockSpec((1, hd), lambda bi, ki, qi: (0, 0)),
                  pl.BlockSpec((1, hd), lambda bi, ki, qi: (0, 0)),
                  tab, tab, tab],
        out_specs=pl.BlockSpec((tq, gw), lambda bi, ki, qi: (bi * (t // tq) + qi, ki)),
        out_shape=jax.ShapeDtypeStruct((b * t, ATTN_HEADS * hd), BF16),
        scratch_shapes=[pltpu.VMEM((lc + t, hd), BF16), pltpu.VMEM((lc + t, 2 * hd), BF16)],
        compiler_params=_cparams(("arbitrary", "arbitrary", "arbitrary")),
    )(p, p, p, pc, pc, qn.reshape(1, hd), kn.reshape(1, hd), *tabs)


def _merge_kernel(r_ref, a_ref, gr_ref, ga_ref, wr_ref, wa_ref, o_ref):
    ret = jnp.dot(r_ref[...], wr_ref[...], preferred_element_type=F32)
    att = jnp.dot(a_ref[...], wa_ref[...], preferred_element_type=F32)
    z = (jax.nn.sigmoid(gr_ref[...].astype(F32)) * ret + jax.nn.sigmoid(ga_ref[...].astype(F32)) * att)
    o_ref[...] = z.astype(o_ref.dtype)


def _merge(retg, att, p, w_ret_o, w_attn_o, offs):
    m, kr = retg.shape
    ka = att.shape[1]
    d = w_ret_o.shape[1]
    tm = _tile(m, 1024)
    tn = functools.reduce(math.gcd, (offs["gr"], offs["ga"], d), 512)
    gro, gao = _blk(offs["gr"], tn), _blk(offs["ga"], tn)
    return pl.pallas_call(
        _merge_kernel,
        grid=(m // tm, d // tn),
        in_specs=[pl.BlockSpec((tm, kr), lambda i, j: (i, 0)),
                  pl.BlockSpec((tm, ka), lambda i, j: (i, 0)),
                  pl.BlockSpec((tm, tn), lambda i, j: (i, gro + j)),
                  pl.BlockSpec((tm, tn), lambda i, j: (i, gao + j)),
                  pl.BlockSpec((kr, tn), lambda i, j: (0, j)),
                  pl.BlockSpec((ka, tn), lambda i, j: (0, j))],
        out_specs=pl.BlockSpec((tm, tn), lambda i, j: (i, j)),
        out_shape=jax.ShapeDtypeStruct((m, d), BF16),
        compiler_params=_cparams(("arbitrary", "arbitrary")),
    )(retg, att, p, p, w_ret_o, w_attn_o)


def _pack_pairs(x):
    hw = x.shape[1] // 2
    lo = pltpu.bitcast(x[:, :hw].astype(BF16).astype(F32), U32)
    hi = pltpu.bitcast(x[:, hw:].astype(BF16).astype(F32), U32)
    return (hi & jnp.uint32(0xFFFF0000)) | (lo >> 16)


def _store_token_tiles(ref, packed):
    m = packed.shape[0]
    st = packed.shape[1] // LANES
    for s in range(st):
        ref[pl.ds(s, m, stride=st), :] = packed[:, s * LANES:(s + 1) * LANES]


def _load_token_tiles(ref, m):
    st = ref.shape[0] // m
    return jnp.concatenate([ref[pl.ds(s, m, stride=st), :] for s in range(st)], axis=1)


def _unpack_pairs(w):
    lo = pltpu.bitcast(w << 16, F32)
    hi = pltpu.bitcast(w & jnp.uint32(0xFFFF0000), F32)
    return lo, hi


def _post_kernel(z_ref, wo_ref, x_ref, g1_ref, pw_ref, fw_ref, sc_ref, sh_ref, rwh_ref, rwl_ref,
                 x1_ref, f_ref, lg_ref):
    tm = x_ref.shape[0]
    sub = min(tm, POST_SUB)
    st = f_ref.shape[0] // tm
    nt = (((1,), (1,)), ((), ()))
    fhs, fls = [], []
    for h in range(tm // sub):
        r0 = h * sub
        y = jnp.dot(z_ref[r0:r0 + sub, :], wo_ref[...], preferred_element_type=F32)
        x1 = x_ref[r0:r0 + sub, :] + g1_ref[0] * (_rms(y) * pw_ref[...])
        x1_ref[r0:r0 + sub, :] = x1
        f = (_rms(x1) * fw_ref[...]) * (1.0 + sc_ref[0]) + sh_ref[0]
        _store_token_tiles(f_ref.at[r0 * st:(r0 + sub) * st, :], _pack_pairs(f))
        fh = f.astype(BF16)
        fhs.append(fh)
        fls.append((f - fh.astype(F32)).astype(BF16))
    fh = jnp.concatenate(fhs, axis=0)
    fl = jnp.concatenate(fls, axis=0)
    lg_ref[...] = (lax.dot_general(rwh_ref[...], fh, nt, preferred_element_type=F32)
                   + lax.dot_general(rwh_ref[...], fl, nt, preferred_element_type=F32)
                   + lax.dot_general(rwl_ref[...], fh, nt, preferred_element_type=F32))


def _post(z, w_out, x2, g1, post_w, pre_w, sc2, sh2, router_wt, rows_per_mod):
    m, d = x2.shape
    e = router_wt.shape[0]
    st = d // 2 // LANES
    assert st == SUBLANES, "token rows must be whole (8, 128) tiles"
    rw_hi = router_wt.astype(BF16)
    rw_lo = (router_wt - rw_hi.astype(F32)).astype(BF16)
    tm = _tile(rows_per_mod, 2 * POST_SUB)
    per = rows_per_mod // tm
    g = g1.shape[0]
    vec = pl.BlockSpec((1, d), lambda i: (0, 0))
    mod = pl.BlockSpec((1, 1, d), lambda i: (i // per, 0, 0))
    once = pl.Buffered(1)
    return pl.pallas_call(
        _post_kernel,
        grid=(m // tm,),
        in_specs=[pl.BlockSpec((tm, d), lambda i: (i, 0)),
                  pl.BlockSpec((d, d), lambda i: (0, 0), pipeline_mode=once),
                  pl.BlockSpec((tm, d), lambda i: (i, 0)),
                  mod, vec, vec, mod, mod,
                  pl.BlockSpec((e, d), lambda i: (0, 0), pipeline_mode=once),
                  pl.BlockSpec((e, d), lambda i: (0, 0), pipeline_mode=once)],
        out_specs=[pl.BlockSpec((tm, d), lambda i: (i, 0)),
                   pl.BlockSpec((tm * st, LANES), lambda i: (i, 0)),
                   pl.BlockSpec((e, tm), lambda i: (0, i))],
        out_shape=[jax.ShapeDtypeStruct((m, d), F32),
                   jax.ShapeDtypeStruct((m * st, LANES), U32),
                   jax.ShapeDtypeStruct((e, m), F32)],
        compiler_params=_cparams(("arbitrary",)),
    )(z, w_out, x2, g1.reshape(g, 1, d), post_w.reshape(1, d), pre_w.reshape(1, d),
      sc2.reshape(g, 1, d), sh2.reshape(g, 1, d), rw_hi, rw_lo)


def _route_kernel(lg_ref, b_ref, tri_ref, idx_ref, w_ref, pos_ref, cnt_ref, carry):
    @pl.when(pl.program_id(0) == 0)
    def _():
        carry[...] = jnp.zeros_like(carry)

    e, tn = lg_ref.shape
    gs = e // N_GROUPS
    neg = -jnp.inf
    big = float(e)
    s = jax.nn.sigmoid(lg_ref[...])
    choice = s + b_ref[...]

    row_g = lax.broadcasted_iota(I32, (gs, tn), 0).astype(F32)
    rows = []
    for g in range(N_GROUPS):
        blk = choice[g * gs:(g + 1) * gs]
        m1 = jnp.max(blk, axis=0, keepdims=True)
        i1 = jnp.min(jnp.where(blk == m1, row_g, big), axis=0, keepdims=True)
        m2 = jnp.max(jnp.where(row_g == i1, neg, blk), axis=0, keepdims=True)
        rows.append(m1 + m2)
    gscore = jnp.concatenate(rows, axis=0)

    row_n = lax.broadcasted_iota(I32, (N_GROUPS, tn), 0).astype(F32)
    sel = jnp.zeros((N_GROUPS, tn), F32)
    cur = gscore
    for _ in range(TOPK_GROUPS):
        m = jnp.max(cur, axis=0, keepdims=True)
        gi = jnp.min(jnp.where(cur == m, row_n, big), axis=0, keepdims=True)
        hit = row_n == gi
        sel = jnp.where(hit, 1.0, sel)
        cur = jnp.where(hit, neg, cur)

    masked = jnp.concatenate(
        [jnp.where(sel[g:g + 1] > 0.0, choice[g * gs:(g + 1) * gs], neg) for g in range(N_GROUPS)], axis=0)

    row_e = lax.broadcasted_iota(I32, (e, tn), 0).astype(F32)
    onehot = jnp.zeros((e, tn), F32)
    cur = masked
    idxs, ws = [], []
    for _ in range(TOP_K):
        m = jnp.max(cur, axis=0, keepdims=True)
        ik = jnp.min(jnp.where(cur == m, row_e, big), axis=0, keepdims=True)
        hit = row_e == ik
        ws.append(jnp.sum(jnp.where(hit, s, 0.0), axis=0, keepdims=True))
        cur = jnp.where(hit, neg, cur)
        onehot = jnp.where(hit, 1.0, onehot)
        idxs.append(ik)
    w = jnp.concatenate(ws, axis=0)
    w = w / jnp.sum(w, axis=0, keepdims=True) * ROUTED_SCALE

    prefix = jnp.dot(onehot.astype(BF16), tri_ref[...], preferred_element_type=F32)
    base = prefix + carry[...]
    pos = jnp.concatenate(
        [jnp.sum(jnp.where(row_e == ik, base, 0.0), axis=0, keepdims=True) for ik in idxs], axis=0)
    carry[...] = carry[...] + jnp.sum(onehot, axis=1, keepdims=True)

    idx_ref[...] = jnp.concatenate(idxs, axis=0).astype(I32)
    w_ref[...] = w
    pos_ref[...] = pos.astype(I32)
    cnt_ref[...] = jnp.broadcast_to(carry[...], cnt_ref.shape).astype(I32)


def _route(logits_t, bias):
    e, n = logits_t.shape
    tn = _tile(n, 512)
    tri = (jnp.arange(tn)[:, None] < jnp.arange(tn)[None, :]).astype(BF16)
    kspec = pl.BlockSpec((TOP_K, tn), lambda i: (0, i))
    return pl.pallas_call(
        _route_kernel,
        grid=(n // tn,),
        in_specs=[pl.BlockSpec((e, tn), lambda i: (0, i)),
                  pl.BlockSpec((e, 1), lambda i: (0, 0)),
                  pl.BlockSpec((tn, tn), lambda i: (0, 0))],
        out_specs=[kspec, kspec, kspec, pl.BlockSpec((e, LANES), lambda i: (0, 0))],
        out_shape=[jax.ShapeDtypeStruct((TOP_K, n), I32), jax.ShapeDtypeStruct((TOP_K, n), F32),
                   jax.ShapeDtypeStruct((TOP_K, n), I32), jax.ShapeDtypeStruct((e, LANES), I32)],
        scratch_shapes=[pltpu.VMEM((e, 1), F32)],
        compiler_params=_cparams(("arbitrary",)),
    )(logits_t, bias.reshape(e, 1), tri)


def _dest_kernel(idx_ref, pos_ref, off_ref, o_ref):
    e = off_ref.shape[0]
    tn = idx_ref.shape[1]
    row_e = lax.broadcasted_iota(I32, (e, tn), 0)
    off = off_ref[...]
    idx = idx_ref[...]
    rows = [jnp.sum(jnp.where(row_e == idx[k:k + 1], off, 0.0), axis=0, keepdims=True) for k in range(TOP_K)]
    o_ref[...] = jnp.concatenate(rows, axis=0).astype(I32) + pos_ref[...]


def _dest(top_idx, pos, pad_off):
    k, n = top_idx.shape
    e = pad_off.shape[0]
    tn = _tile(n, 512)
    spec = pl.BlockSpec((k, tn), lambda i: (0, i))
    return pl.pallas_call(
        _dest_kernel,
        grid=(n // tn,),
        in_specs=[spec, spec, pl.BlockSpec((e, 1), lambda i: (0, 0))],
        out_specs=spec,
        out_shape=jax.ShapeDtypeStruct((k, n), I32),
        compiler_params=_cparams(("arbitrary",)),
    )(top_idx, pos, pad_off.astype(F32).reshape(e, 1))


def _dispatch_kernel(cnt_ref, off_ref, nu_ref, dest_ref, f_ref, xs_hbm, zbuf, sem, zsem):
    i = pl.program_id(0)
    tt = dest_ref.shape[1]
    st = SUBLANES
    bm = zbuf.shape[0] // st
    n_exp = cnt_ref.shape[0]
    nb = xs_hbm.shape[0] // (bm * st)

    def zero_fill(wait):
        def fill(start, rows):
            cp = pltpu.make_async_copy(zbuf.at[pl.ds(0, rows * st), :],
                                       xs_hbm.at[pl.ds(pl.multiple_of(start * st, st), rows * st), :], zsem)
            if wait:
                cp.wait()
            else:
                cp.start()

        def per_expert(e, c):
            cnt = cnt_ref[e]
            npad = (-cnt) & (bm - 1)
            start = off_ref[e] + cnt
            p = bm // 2
            while p >= 1:
                @pl.when((npad & p) != 0)
                def _(start=start, p=p):
                    fill(start, p)
                start = start + (npad & p)
                p //= 2
            return c
        lax.fori_loop(0, n_exp, per_expert, 0)

        def per_block(b, c):
            fill(b * bm, bm)
            return c
        lax.fori_loop(nu_ref[0], nb, per_block, 0)

    def row_copy(j, dst):
        return pltpu.make_async_copy(f_ref.at[pl.ds(pl.multiple_of(j * st, st), st), :],
                                     xs_hbm.at[pl.ds(pl.multiple_of(dst * st, st), st), :], sem)

    @pl.when(i == 0)
    def _():
        zbuf[...] = jnp.zeros_like(zbuf)
        zero_fill(False)

    def issue(j, c):
        base = j * DMA_GROUP
        dsts = [[dest_ref[k, base + u] for k in range(TOP_K)] for u in range(DMA_GROUP)]
        for u in range(DMA_GROUP):
            for k in range(TOP_K):
                row_copy(base + u, dsts[u][k]).start(priority=k % 2)

        @pl.when(j >= DMA_LAG)
        def _():
            wait_group()
        return c

    def wait_group():
        for _ in range(DMA_GROUP * TOP_K):
            row_copy(0, 0).wait()

    n_groups = tt // DMA_GROUP
    lax.fori_loop(0, n_groups, issue, 0)

    def drain(j, c):
        wait_group()
        return c
    lax.fori_loop(0, min(DMA_LAG, n_groups), drain, 0)

    @pl.when(i == pl.num_programs(0) - 1)
    def _():
        zero_fill(True)


def _dispatch(counts, pad_off, n_used, dest, f_packed, n_slots):
    st = SUBLANES
    n = f_packed.shape[0] // st
    tt = _tile(n, 512)
    grid_spec = pltpu.PrefetchScalarGridSpec(
        num_scalar_prefetch=3,
        grid=(n // tt,),
        in_specs=[pl.BlockSpec((TOP_K, tt), lambda i, c, o, u: (0, i), memory_space=pltpu.SMEM),
                  pl.BlockSpec((tt * st, LANES), lambda i, c, o, u: (i, 0))],
        out_specs=pl.BlockSpec(memory_space=pl.ANY),
        scratch_shapes=[pltpu.VMEM((MOE_BLOCK * st, LANES), U32), pltpu.SemaphoreType.DMA(()),
                        pltpu.SemaphoreType.DMA(())],
    )
    return pl.pallas_call(
        _dispatch_kernel,
        grid_spec=grid_spec,
        out_shape=jax.ShapeDtypeStruct((n_slots * st, LANES), U32),
        compiler_params=_cparams(("arbitrary",)),
    )(counts, pad_off, n_used, dest, f_packed)


def _cast_rows(src, dst):
    rows = src.shape[0]
    ch = CAST_ROWS * LANES // src.shape[1]

    def body(j, c):
        r0 = pl.multiple_of(j * ch, ch)
        dst[pl.ds(r0, ch), :] = src[pl.ds(r0, ch), :].astype(dst.dtype)
        return c
    lax.fori_loop(0, rows // ch, body, 0, unroll=4)


def _expert_kernel(ob_ref, bv_ref, act_ref, na_ref, nu_ref, x_ref, w1_hbm, w3_hbm, w2_hbm, y_ref,
                   w1f, w3f, w2f, w1b, w3b, w2b, wsem):
    b = pl.program_id(0)
    n_used = nu_ref[0]
    n_act = na_ref[0]
    bm = MOE_BLOCK
    hw = w1b.shape[0] // 2
    st = x_ref.shape[0] // bm
    ring = w1f.shape[0]

    def fetch(i, wait):
        e = act_ref[i]
        slot = i % ring
        for src, dst in ((w1_hbm, w1f), (w3_hbm, w3f), (w2_hbm, w2f)):
            cp = pltpu.make_async_copy(src.at[e], dst.at[slot], wsem.at[slot])
            if wait:
                cp.wait()
            else:
                cp.start(priority=1)

    @pl.when(b < n_used)
    def _():
        i = ob_ref[b]
        first = jnp.logical_or(b == 0, i != ob_ref[jnp.maximum(b - 1, 0)])

        @pl.when(b == 0)
        def _():
            fetch(0, False)
            for ahead in range(1, ring):
                @pl.when(n_act > ahead)
                def _(ahead=ahead):
                    fetch(ahead, False)

        @pl.when(first)
        def _():
            fetch(i, True)
            slot = i % ring
            for src, dst in ((w1f, w1b), (w3f, w3b), (w2f, w2b)):
                _cast_rows(src.at[slot], dst)

            @pl.when(i + ring < n_act)
            def _():
                fetch(i + ring, False)

        def ffn(rows):
            lo, hi = _unpack_pairs(_load_token_tiles(x_ref.at[0:rows * st, :], rows))
            lo = lo.astype(BF16)
            hi = hi.astype(BF16)
            h1 = (jnp.dot(lo, w1b[0:hw, :], preferred_element_type=F32)
                  + jnp.dot(hi, w1b[hw:, :], preferred_element_type=F32))
            h3 = (jnp.dot(lo, w3b[0:hw, :], preferred_element_type=F32)
                  + jnp.dot(hi, w3b[hw:, :], preferred_element_type=F32))
            hid = (h1 * jax.nn.sigmoid(h1) * h3).astype(BF16)
            _store_token_tiles(y_ref.at[0:rows * st, :],
                               _pack_pairs(jnp.dot(hid, w2b[...], preferred_element_type=F32)))

        half = bm // 2
        valid = bv_ref[b]

        @pl.when(valid > half)
        def _():
            ffn(bm)

        @pl.when(valid <= half)
        def _():
            ffn(half)
            y_ref[half * st:, :] = x_ref[half * st:, :]


def _experts(xs, blk_ord, blk_valid, active, n_active, n_used, w1, w3, w2):
    nb = blk_ord.shape[0]
    e, d, fdim = w1.shape
    rows = MOE_BLOCK * (d // 2 // LANES)
    w_elems = 3 * d * fdim
    vmem_limit = (W_BUFS * w_elems * 4 + w_elems * 2 + 4 * rows * LANES * 4
                  + MOE_BLOCK * (2 * fdim + 2 * d) * 4 + VMEM_SLACK)
    assert vmem_limit <= V7X_VMEM_BYTES
    grid_spec = pltpu.PrefetchScalarGridSpec(
        num_scalar_prefetch=5,
        grid=(nb,),
        in_specs=[pl.BlockSpec((rows, LANES), lambda b, ob, bv, ac, na, nu: (jnp.minimum(b, nu[0] - 1), 0)),
                  pl.BlockSpec(memory_space=pl.ANY),
                  pl.BlockSpec(memory_space=pl.ANY),
                  pl.BlockSpec(memory_space=pl.ANY)],
        out_specs=pl.BlockSpec((rows, LANES), lambda b, ob, bv, ac, na, nu: (jnp.minimum(b, nu[0] - 1), 0)),
        scratch_shapes=[pltpu.VMEM((W_BUFS, d, fdim), F32), pltpu.VMEM((W_BUFS, d, fdim), F32),
                        pltpu.VMEM((W_BUFS, fdim, d), F32),
                        pltpu.VMEM((d, fdim), BF16), pltpu.VMEM((d, fdim), BF16), pltpu.VMEM((fdim, d), BF16),
                        pltpu.SemaphoreType.DMA((W_BUFS,))],
    )
    return pl.pallas_call(
        _expert_kernel,
        grid_spec=grid_spec,
        out_shape=jax.ShapeDtypeStruct((nb * rows, LANES), U32),
        input_output_aliases={5: 0},
        compiler_params=_cparams(("arbitrary",), vmem_limit),
    )(blk_ord, blk_valid, active, n_active, n_used, xs, w1, w3, w2)


def _combine_kernel(dcur_ref, dnxt_ref, wt_ref, f_ref, x1_ref, g2_ref, pw_ref, s1_ref, s3_ref, s2_ref, ys_hbm,
                    o_ref, ybuf, sem):
    i = pl.program_id(0)
    tm = x1_ref.shape[0]
    hw = x1_ref.shape[1] // 2
    st = f_ref.shape[0] // tm
    slot = i % 2

    def row_copy(src, k, r, s):
        return pltpu.make_async_copy(ys_hbm.at[pl.ds(pl.multiple_of(src * st, st), st), :],
                                     ybuf.at[s, k, pl.ds(pl.multiple_of(r * st, st), st), :], sem.at[s])

    def start_gather(d_ref, s):
        def body(j, c):
            base = j * DMA_GROUP
            srcs = [[d_ref[k, base + u] for k in range(TOP_K)] for u in range(DMA_GROUP)]
            for u in range(DMA_GROUP):
                for k in range(TOP_K):
                    row_copy(srcs[u][k], k, base + u, s).start(priority=k % 2)
            return c
        lax.fori_loop(0, tm // DMA_GROUP, body, 0)

    def wait_gather(s):
        def body(r, c):
            for k in range(TOP_K):
                row_copy(0, k, r, s).wait()
            return c
        lax.fori_loop(0, tm, body, 0, unroll=2)

    @pl.when(i == 0)
    def _():
        start_gather(dcur_ref, 0)

    @pl.when(i + 1 < pl.num_programs(0))
    def _():
        start_gather(dnxt_ref, 1 - slot)

    lo, hi = _unpack_pairs(_load_token_tiles(f_ref, tm))
    lo = lo.astype(BF16)
    hi = hi.astype(BF16)
    h1 = (jnp.dot(lo, s1_ref[0:hw, :], preferred_element_type=F32)
          + jnp.dot(hi, s1_ref[hw:, :], preferred_element_type=F32))
    h3 = (jnp.dot(lo, s3_ref[0:hw, :], preferred_element_type=F32)
          + jnp.dot(hi, s3_ref[hw:, :], preferred_element_type=F32))
    shared = jnp.dot((h1 * jax.nn.sigmoid(h1) * h3).astype(BF16), s2_ref[...], preferred_element_type=F32)

    wait_gather(slot)
    wt = wt_ref[...]
    r_lo = shared[:, :hw]
    r_hi = shared[:, hw:]
    for k in range(TOP_K):
        a, c = _unpack_pairs(_load_token_tiles(ybuf.at[slot, k], tm))
        w = wt[:, k:k + 1]
        r_lo = r_lo + a * w
        r_hi = r_hi + c * w
    moe = jnp.concatenate([r_lo, r_hi], axis=1)
    o_ref[...] = x1_ref[...] + g2_ref[0] * (_rms(moe) * pw_ref[...])


def _combine(ys, dest, top_wt, f_packed, x1, g2, post_w, s1, s3, s2, rows_per_mod):
    m, d = x1.shape
    st = d // 2 // LANES
    fdim = s1.shape[1]
    tm = _tile(rows_per_mod, 256)
    per = rows_per_mod // tm
    g = g2.shape[0]
    nblk = m // tm
    smem = functools.partial(pl.BlockSpec, memory_space=pltpu.SMEM)
    return pl.pallas_call(
        _combine_kernel,
        grid=(nblk,),
        in_specs=[smem((TOP_K, tm), lambda i: (0, i)),
                  smem((TOP_K, tm), lambda i: (0, jnp.minimum(i + 1, nblk - 1))),
                  pl.BlockSpec((tm, TOP_K), lambda i: (i, 0)),
                  pl.BlockSpec((tm * st, LANES), lambda i: (i, 0)),
                  pl.BlockSpec((tm, d), lambda i: (i, 0)),
                  pl.BlockSpec((1, 1, d), lambda i: (i // per, 0, 0)),
                  pl.BlockSpec((1, d), lambda i: (0, 0)),
                  pl.BlockSpec((d, fdim), lambda i: (0, 0)),
                  pl.BlockSpec((d, fdim), lambda i: (0, 0)),
                  pl.BlockSpec((fdim, d), lambda i: (0, 0)),
                  pl.BlockSpec(memory_space=pl.ANY)],
        out_specs=pl.BlockSpec((tm, d), lambda i: (i, 0)),
        out_shape=jax.ShapeDtypeStruct((m, d), F32),
        scratch_shapes=[pltpu.VMEM((2, TOP_K, tm * st, LANES), U32), pltpu.SemaphoreType.DMA((2,))],
        compiler_params=_cparams(("arbitrary",)),
    )(dest, dest, top_wt, f_packed, x1, g2.reshape(g, 1, d), post_w.reshape(1, d), s1, s3, s2, ys)


def _layer(x, c, ctx, c_ctx, ada_w, ada_b, mix_pre_norm, mix_post_norm, ffn_pre_norm, ffn_post_norm,
           w_in, ret_decay_fwd, ret_decay_bwd, attn_q_norm, attn_k_norm, w_ret_o, w_attn_o, w_out,
           router_w, router_b, exp_w1, exp_w3, exp_w2, shared_w1, shared_w3, shared_w2):
    b, t, d = x.shape
    lc = ctx.shape[1]
    n = b * t
    rq_w, rv_w = RET_HEADS * RET_QK_DIM, RET_HEADS * RET_V_DIM
    aq_w, akv_w = ATTN_HEADS * ATTN_HEAD_DIM, ATTN_KV_HEADS * ATTN_HEAD_DIM
    names = ("rq", "rk", "rv", "rg", "aq", "ak", "av", "gr", "ga")
    widths = (rq_w, rq_w, rv_w, rv_w, aq_w, akv_w, akv_w, d, d)
    offs, o = {}, 0
    for nm, wd in zip(names, widths):
        offs[nm] = o
        o += wd

    rows = -(-(b + 1) // 16) * 16
    cond = jnp.zeros((rows, d), F32).at[:b].set(c).at[b].set(c_ctx)
    mod = _adaln(cond, ada_w, ada_b)
    sh1, sc1, g1, sh2, sc2, g2 = [mod[:b, i * d:(i + 1) * d] for i in range(6)]
    ch1, cs1 = mod[b:b + 1, 0:d], mod[b:b + 1, d:2 * d]

    x2 = x.reshape(n, d)
    h_x = _norm_mod(x2, mix_pre_norm, sc1, sh1, t)
    h_c = _norm_mod(ctx.reshape(b * lc, d), mix_pre_norm, cs1, ch1, b * lc)

    tn = _tile(w_in.shape[1], 1024)
    p = _project(h_x, w_in, list(range(w_in.shape[1] // tn)), tn, BF16)
    cnames = ("rv", "rk", "ak", "av")
    tnc = functools.reduce(math.gcd, [offs[nm] for nm in cnames] + [widths[names.index(nm)] for nm in cnames], tn)
    coffs, ctiles = {}, []
    for nm in cnames:
        coffs[nm] = len(ctiles) * tnc
        ctiles += list(range(offs[nm] // tnc, (offs[nm] + widths[names.index(nm)]) // tnc))
    pc = _project(h_c, w_in, ctiles, tnc, BF16)

    lg = jnp.stack([jax.nn.log_sigmoid(ret_decay_fwd.astype(F32)),
                    jax.nn.log_sigmoid(ret_decay_bwd.astype(F32))])
    rc, rs_lo, rs_hi = _rope_tables(t, RET_QK_DIM)
    retg = _retention(p, pc, lg, rc, rs_lo + rs_hi, b, t, lc, offs, coffs)
    att = _attention(p, pc, attn_q_norm, attn_k_norm, _rope_tables(t, ATTN_HEAD_DIM), b, t, lc, offs, coffs)

    z = _merge(retg, att, p, w_ret_o.astype(BF16), w_attn_o.astype(BF16), offs)
    x1, f_packed, logits_t = _post(z, w_out.astype(BF16), x2, g1, mix_post_norm, ffn_pre_norm, sc2, sh2,
                                   router_w.T, t)

    top_idx, top_w, pos, cnt = _route(logits_t, router_b)

    bm = MOE_BLOCK
    e = router_w.shape[1]
    nb = (n * TOP_K) // bm + e
    counts = cnt[:, 0]
    padded = (counts + bm - 1) // bm * bm
    pad_end = jnp.cumsum(padded)
    pad_off = pad_end - padded
    blk_start = jnp.arange(nb, dtype=I32) * bm
    blk_expert = jnp.minimum(jnp.sum((pad_end[None, :] <= blk_start[:, None]).astype(I32), axis=1), e - 1)
    n_used = (pad_end[-1] // bm).astype(I32).reshape(1)
    is_act = (counts > 0).astype(I32)
    exp_ord = jnp.cumsum(is_act) - is_act
    n_active = jnp.sum(is_act).astype(I32).reshape(1)
    eids = jnp.arange(e, dtype=I32)
    active = jnp.sum(jnp.where((exp_ord[None, :] == eids[:, None]) & (is_act[None, :] > 0), eids[None, :], 0),
                     axis=1).astype(I32)
    blk_ord = exp_ord[blk_expert].astype(I32)
    blk_in_exp = jnp.arange(nb, dtype=I32) - (pad_off // bm)[blk_expert]
    blk_valid = jnp.clip(counts[blk_expert] - blk_in_exp * bm, 0, bm).astype(I32)

    dest = _dest(top_idx, pos, pad_off)
    xs = _dispatch(counts, pad_off, n_used, dest, f_packed, nb * bm)
    ys = _experts(xs, blk_ord, blk_valid, active, n_active, n_used, exp_w1, exp_w3, exp_w2)
    out = _combine(ys, dest, top_w.T, f_packed, x1, g2, ffn_post_norm, shared_w1.astype(BF16),
                   shared_w3.astype(BF16), shared_w2.astype(BF16), t)
    return out.reshape(b, t, d)


def kernel(x, c, ctx, c_ctx, ada_w, ada_b, mix_pre_norm, mix_post_norm, ffn_pre_norm, ffn_post_norm,
           w_in, ret_decay_fwd, ret_decay_bwd, attn_q_norm, attn_k_norm, w_ret_o, w_attn_o, w_out,
           router_w, router_b, exp_w1, exp_w3, exp_w2, shared_w1, shared_w3, shared_w2):
    assert ada_w.shape[0] == 1, "single-layer operation"
    return _layer(x, c, ctx, c_ctx, ada_w[0], ada_b[0], mix_pre_norm[0], mix_post_norm[0], ffn_pre_norm[0],
                  ffn_post_norm[0], w_in[0], ret_decay_fwd[0], ret_decay_bwd[0], attn_q_norm[0],
                  attn_k_norm[0], w_ret_o[0], w_attn_o[0], w_out[0], router_w[0], router_b[0],
                  exp_w1[0], exp_w3[0], exp_w2[0], shared_w1[0], shared_w3[0], shared_w2[0])
```

```python
import functools
import math

import jax
import jax.numpy as jnp
import numpy as np
from jax import lax
from jax.experimental import pallas as pl
from jax.experimental.pallas import tpu as pltpu

F32 = jnp.float32
BF16 = jnp.bfloat16
U32 = jnp.uint32
I32 = jnp.int32

GRID_W = 64
NORM_EPS = 1e-6
ROPE_THETA = 10000.0
RET_HEADS = 4
RET_QK_DIM = 256
RET_V_DIM = 512
ATTN_HEADS = 16
ATTN_KV_HEADS = 4
ATTN_HEAD_DIM = 128
N_EXPERTS = 256
TOP_K = 8
N_GROUPS = 8
TOPK_GROUPS = 4
EXPERT_DIM = 512
ROUTED_SCALE = 2.5
LOG2E = 1.4426950408889634

V7X_VMEM_BYTES = 64 * 1024 * 1024
VMEM_LIMIT = V7X_VMEM_BYTES - 12 * 1024 * 1024
VMEM_SLACK = 4 * 1024 * 1024
LANES = 128
SUBLANES = 8

RET_CHUNK = 256
MOE_BLOCK = 256
DMA_GROUP = 4
DMA_LAG = 16
CAST_ROWS = 256
POST_SUB = 256
ATTN_SUB = 128
W_BUFS = 3


def _cparams(sem, vmem_limit=VMEM_LIMIT):
    return pltpu.CompilerParams(dimension_semantics=sem, vmem_limit_bytes=vmem_limit)


def _tile(dim, pref):
    if dim <= pref:
        return dim
    for t in range(pref, 0, -LANES):
        if dim % t == 0:
            return t
    raise ValueError((dim, pref))


def _blk(off, width):
    assert off % width == 0, (off, width)
    return off // width


def _ada_kernel(s_ref, w_ref, b_ref, o_ref):
    s = s_ref[...]
    s = (s * jax.nn.sigmoid(s)).astype(BF16)
    o_ref[...] = jnp.dot(s, w_ref[...].astype(BF16), preferred_element_type=F32) + b_ref[...]


def _adaln(cond, w, b):
    r, d = cond.shape
    n = w.shape[1]
    tn = _tile(n, 1024)
    return pl.pallas_call(
        _ada_kernel,
        grid=(n // tn,),
        in_specs=[pl.BlockSpec((r, d), lambda j: (0, 0)),
                  pl.BlockSpec((d, tn), lambda j: (0, j)),
                  pl.BlockSpec((1, tn), lambda j: (0, j))],
        out_specs=pl.BlockSpec((r, tn), lambda j: (0, j)),
        out_shape=jax.ShapeDtypeStruct((r, n), F32),
        compiler_params=_cparams(("arbitrary",)),
    )(cond, w, b.reshape(1, n))


def _rms(x):
    return x * lax.rsqrt(jnp.mean(x * x, axis=-1, keepdims=True) + NORM_EPS)


def _norm_mod_kernel(x_ref, w_ref, sc_ref, sh_ref, o_ref):
    y = _rms(x_ref[...]) * w_ref[...]
    o_ref[...] = (y * (1.0 + sc_ref[0]) + sh_ref[0]).astype(o_ref.dtype)


def _norm_mod(x2, w, sc, sh, rows_per_mod):
    m, d = x2.shape
    tm = _tile(rows_per_mod, 512)
    per = rows_per_mod // tm
    g = sc.shape[0]
    return pl.pallas_call(
        _norm_mod_kernel,
        grid=(m // tm,),
        in_specs=[pl.BlockSpec((tm, d), lambda i: (i, 0)),
                  pl.BlockSpec((1, d), lambda i: (0, 0)),
                  pl.BlockSpec((1, 1, d), lambda i: (i // per, 0, 0)),
                  pl.BlockSpec((1, 1, d), lambda i: (i // per, 0, 0))],
        out_specs=pl.BlockSpec((tm, d), lambda i: (i, 0)),
        out_shape=jax.ShapeDtypeStruct((m, d), BF16),
        compiler_params=_cparams(("arbitrary",)),
    )(x2, w.reshape(1, d), sc.reshape(g, 1, d), sh.reshape(g, 1, d))


def _proj_kernel(cols_ref, a_ref, w_ref, o_ref, wb):
    @pl.when(pl.program_id(1) == 0)
    def _():
        _cast_rows(w_ref, wb)

    o_ref[...] = jnp.dot(a_ref[...], wb[...], preferred_element_type=F32).astype(o_ref.dtype)


def _project(a, w, col_tiles, tn, out_dtype):
    m, k = a.shape
    tm = _tile(m, 1024)
    nj = len(col_tiles)
    grid_spec = pltpu.PrefetchScalarGridSpec(
        num_scalar_prefetch=1,
        grid=(nj, m // tm),
        in_specs=[pl.BlockSpec((tm, k), lambda j, i, c: (i, 0)),
                  pl.BlockSpec((k, tn), lambda j, i, c: (0, c[j]))],
        out_specs=pl.BlockSpec((tm, tn), lambda j, i, c: (i, j)),
        scratch_shapes=[pltpu.VMEM((k, tn), BF16)],
    )
    return pl.pallas_call(
        _proj_kernel,
        grid_spec=grid_spec,
        out_shape=jax.ShapeDtypeStruct((m, nj * tn), out_dtype),
        compiler_params=_cparams(("arbitrary", "arbitrary")),
    )(jnp.asarray(col_tiles, I32), a, w)


def _rope_tables(t, dim):
    half = dim // 2
    pos = jnp.arange(t, dtype=I32)
    row = (pos // GRID_W).astype(F32)
    col = (pos % GRID_W).astype(F32)
    inv = ROPE_THETA ** (-jnp.arange(0, half, 2, dtype=F32) / half)
    ar = row[:, None] * inv[None, :]
    ac = col[:, None] * inv[None, :]
    cr, sr, cc, sc = jnp.cos(ar), jnp.sin(ar), jnp.cos(ac), jnp.sin(ac)
    z = jnp.zeros_like(sr)
    c = jnp.concatenate([cr, cr, cc, cc], axis=1)
    s_lo = jnp.concatenate([-sr, z, -sc, z], axis=1)
    s_hi = jnp.concatenate([z, sr, z, sc], axis=1)
    return c, s_lo, s_hi


def _rope128(x, c, s_lo, s_hi):
    q4 = ATTN_HEAD_DIM // 4
    return (x * c + pltpu.roll(x, LANES - q4, axis=1) * s_lo + pltpu.roll(x, q4, axis=1) * s_hi)


def _rope256(x, c, s):
    parts = [pltpu.roll(x[:, i * LANES:(i + 1) * LANES], LANES // 2, axis=1)
             for i in range(RET_QK_DIM // LANES)]
    return x * c + jnp.concatenate(parts, axis=1) * s


def _ret_kernel(lg_ref, q_ref, k_ref, v_ref, g_ref, kc_ref, vc_ref, c_ref, s_ref, o_ref,
                qs, ks, acc, st):
    h = pl.program_id(1)
    lgf = lg_ref[0, h]
    lgb = lg_ref[1, h]
    t = q_ref.shape[0]
    ch = RET_CHUNK if t % RET_CHUNK == 0 else t
    nch = t // ch
    lc = kc_ref.shape[0]
    k_scale = RET_QK_DIM ** -0.5
    tn_dims = (((0,), (0,)), ((), ()))

    ii = lax.broadcasted_iota(I32, (ch, ch), 0)
    jj = lax.broadcasted_iota(I32, (ch, ch), 1)
    dif = (ii - jj).astype(F32)
    dmat = (jnp.where(dif >= 0, jnp.exp(lgf * jnp.maximum(dif, 0.0)), 0.0)
            + jnp.where(dif <= 0, jnp.exp(lgb * jnp.maximum(-dif, 0.0)), 0.0))
    pos = lax.broadcasted_iota(I32, (ch, 1), 0).astype(F32)
    qf_dec = jnp.exp(lgf * (pos + 1.0))
    kf_dec = jnp.exp(lgf * (ch - 1.0 - pos))
    qb_dec = jnp.exp(lgb * (ch - pos))
    kb_dec = jnp.exp(lgb * pos)
    chv = jnp.full((1, 1), float(ch), F32)
    cf_dec = jnp.exp(lgf * chv)
    cb_dec = jnp.exp(lgb * chv)

    cpos = lax.broadcasted_iota(I32, (lc, 1), 0).astype(F32)
    kc = kc_ref[...].astype(F32) * k_scale
    vc = vc_ref[...]
    st[0] = lax.dot_general((kc * jnp.exp(lgf * (lc - 1.0 - cpos))).astype(BF16), vc, tn_dims,
                            preferred_element_type=F32)
    st[1] = lax.dot_general((kc * jnp.exp(lgb * cpos)).astype(BF16), vc, tn_dims,
                            preferred_element_type=F32)

    def fwd(c, carry):
        r0 = pl.multiple_of(c * ch, ch)
        cs = c_ref[pl.ds(r0, ch), :]
        sn = s_ref[pl.ds(r0, ch), :]
        q = _rope256(q_ref[pl.ds(r0, ch), :].astype(F32), cs, sn)
        k = _rope256(k_ref[pl.ds(r0, ch), :].astype(F32), cs, sn) * k_scale
        v = v_ref[pl.ds(r0, ch), :]
        qb = q.astype(BF16)
        kb = k.astype(BF16)
        qs[pl.ds(r0, ch), :] = q
        ks[pl.ds(r0, ch), :] = k
        sc = lax.dot_general(qb, kb, (((1,), (1,)), ((), ())), preferred_element_type=F32) * dmat
        o = jnp.dot(sc.astype(BF16), v, preferred_element_type=F32)
        o += jnp.dot((q * qf_dec).astype(BF16), st[0].astype(BF16), preferred_element_type=F32)
        acc[pl.ds(r0, ch), :] = o
        st[0] = st[0] * cf_dec + lax.dot_general((k * kf_dec).astype(BF16), v, tn_dims,
                                                 preferred_element_type=F32)
        return carry

    lax.fori_loop(0, nch, fwd, 0, unroll=2)

    def bwd(i, carry):
        c = nch - 1 - i
        r0 = pl.multiple_of(c * ch, ch)
        q = qs[pl.ds(r0, ch), :]
        k = ks[pl.ds(r0, ch), :]
        v = v_ref[pl.ds(r0, ch), :]
        o = jnp.dot((q * qb_dec).astype(BF16), st[1].astype(BF16), preferred_element_type=F32)
        tot = acc[pl.ds(r0, ch), :] + o
        gate = g_ref[pl.ds(r0, ch), :].astype(F32)
        o_ref[pl.ds(r0, ch), :] = (gate * jax.nn.sigmoid(gate) * _rms(tot)).astype(o_ref.dtype)
        st[1] = st[1] * cb_dec + lax.dot_general((k * kb_dec).astype(BF16), v, tn_dims,
                                                 preferred_element_type=F32)
        return carry

    lax.fori_loop(0, nch, bwd, 0, unroll=2)


def _retention(p, pc, lg, tab_c, tab_s, b, t, lc, offs, coffs):
    h, dk, dv = RET_HEADS, RET_QK_DIM, RET_V_DIM
    qo, ko, vo, go = (_blk(offs["rq"], dk), _blk(offs["rk"], dk), _blk(offs["rv"], dv), _blk(offs["rg"], dv))
    kco, vco = _blk(coffs["rk"], dk), _blk(coffs["rv"], dv)
    grid_spec = pltpu.PrefetchScalarGridSpec(
        num_scalar_prefetch=1,
        grid=(b, h),
        in_specs=[pl.BlockSpec((t, dk), lambda bi, hi, lg: (bi, qo + hi)),
                  pl.BlockSpec((t, dk), lambda bi, hi, lg: (bi, ko + hi)),
                  pl.BlockSpec((t, dv), lambda bi, hi, lg: (bi, vo + hi)),
                  pl.BlockSpec((t, dv), lambda bi, hi, lg: (bi, go + hi)),
                  pl.BlockSpec((lc, dk), lambda bi, hi, lg: (bi, kco + hi)),
                  pl.BlockSpec((lc, dv), lambda bi, hi, lg: (bi, vco + hi)),
                  pl.BlockSpec((t, dk), lambda bi, hi, lg: (0, 0)),
                  pl.BlockSpec((t, dk), lambda bi, hi, lg: (0, 0))],
        out_specs=pl.BlockSpec((t, dv), lambda bi, hi, lg: (bi, hi)),
        scratch_shapes=[pltpu.VMEM((t, dk), F32), pltpu.VMEM((t, dk), F32),
                        pltpu.VMEM((t, dv), F32), pltpu.VMEM((2, dk, dv), F32)],
    )
    return pl.pallas_call(
        _ret_kernel,
        grid_spec=grid_spec,
        out_shape=jax.ShapeDtypeStruct((b * t, h * dv), BF16),
        compiler_params=_cparams(("arbitrary", "arbitrary")),
    )(lg, p, p, p, p, pc, pc, tab_c, tab_s)


def _attn_kernel(q_ref, kx_ref, vx_ref, kc_ref, vc_ref, qn_ref, kn_ref, c_ref, sl_ref, sh_ref, o_ref,
                 k_s, v_s):
    qt = pl.program_id(2)
    tq = q_ref.shape[0]
    lc = kc_ref.shape[0]
    hd = ATTN_HEAD_DIM
    scale = hd ** -0.5 * LOG2E

    @pl.when(qt == 0)
    def _():
        kn = kn_ref[...]
        k_s[0:lc, :] = (_rms(kc_ref[...].astype(F32)) * kn).astype(BF16)
        kx = _rms(kx_ref[...].astype(F32)) * kn
        k_s[lc:, :] = _rope128(kx, c_ref[...], sl_ref[...], sh_ref[...]).astype(BF16)
        v_s[0:lc, 0:hd] = vc_ref[...]
        v_s[lc:, 0:hd] = vx_ref[...]
        col = lax.broadcasted_iota(I32, (v_s.shape[0], hd), 1)
        v_s[:, hd:] = jnp.where(col == 0, 1.0, 0.0).astype(BF16)

    qn = qn_ref[...]
    kk = k_s[...]
    vv = v_s[...]
    sub = min(tq, ATTN_SUB)
    for u in range(tq // sub):
        r0 = pl.multiple_of(qt * tq + u * sub, sub)
        c = c_ref[pl.ds(r0, sub), :]
        sl = sl_ref[pl.ds(r0, sub), :]
        sh = sh_ref[pl.ds(r0, sub), :]
        for g in range(ATTN_HEADS // ATTN_KV_HEADS):
            q = _rms(q_ref[u * sub:(u + 1) * sub, g * hd:(g + 1) * hd].astype(F32)) * qn
            q = (_rope128(q, c, sl, sh) * scale).astype(BF16)
            s = lax.dot_general(q, kk, (((1,), (1,)), ((), ())), preferred_element_type=F32)
            m = jnp.max(s, axis=-1, keepdims=True)
            p = jnp.exp2(s - m).astype(BF16)
            ov = jnp.dot(p, vv, preferred_element_type=F32)
            o = ov[:, 0:hd] / ov[:, hd:hd + 1]
            o_ref[u * sub:(u + 1) * sub, g * hd:(g + 1) * hd] = o.astype(o_ref.dtype)


def _attention(p, pc, qn, kn, tabs, b, t, lc, offs, coffs):
    hd, kvh = ATTN_HEAD_DIM, ATTN_KV_HEADS
    gw = (ATTN_HEADS // kvh) * hd
    tq = _tile(t, 256)
    qo, ko, vo = _blk(offs["aq"], gw), _blk(offs["ak"], hd), _blk(offs["av"], hd)
    kco, vco = _blk(coffs["ak"], hd), _blk(coffs["av"], hd)
    tab = pl.BlockSpec((t, hd), lambda bi, ki, qi: (0, 0))
    return pl.pallas_call(
        _attn_kernel,
        grid=(b, kvh, t // tq),
        in_specs=[pl.BlockSpec((tq, gw), lambda bi, ki, qi: (bi * (t // tq) + qi, qo + ki)),
                  pl.BlockSpec((t, hd), lambda bi, ki, qi: (bi, ko + ki)),
                  pl.BlockSpec((t, hd), lambda bi, ki, qi: (bi, vo + ki)),
                  pl.BlockSpec((lc, hd), lambda bi, ki, qi: (bi, kco + ki)),
                  pl.BlockSpec((lc, hd), lambda bi, ki, qi: (bi, vco + ki)),
                  pl.BlockSpec((1, hd), lambda bi, ki, qi: (0, 0)),
                  pl.BlockSpec((1, hd), lambda bi, ki, qi: (0, 0)),
                  tab, tab, tab],
        out_specs=pl.BlockSpec((tq, gw), lambda bi, ki, qi: (bi * (t // tq) + qi, ki)),
        out_shape=jax.ShapeDtypeStruct((b * t, ATTN_HEADS * hd), BF16),
        scratch_shapes=[pltpu.VMEM((lc + t, hd), BF16), pltpu.VMEM((lc + t, 2 * hd), BF16)],
        compiler_params=_cparams(("arbitrary", "arbitrary", "arbitrary")),
    )(p, p, p, pc, pc, qn.reshape(1, hd), kn.reshape(1, hd), *tabs)


def _merge_kernel(r_ref, a_ref, gr_ref, ga_ref, wr_ref, wa_ref, o_ref):
    ret = jnp.dot(r_ref[...], wr_ref[...], preferred_element_type=F32)
    att = jnp.dot(a_ref[...], wa_ref[...], preferred_element_type=F32)
    z = (jax.nn.sigmoid(gr_ref[...].astype(F32)) * ret + jax.nn.sigmoid(ga_ref[...].astype(F32)) * att)
    o_ref[...] = z.astype(o_ref.dtype)


def _merge(retg, att, p, w_ret_o, w_attn_o, offs):
    m, kr = retg.shape
    ka = att.shape[1]
    d = w_ret_o.shape[1]
    tm = _tile(m, 1024)
    tn = functools.reduce(math.gcd, (offs["gr"], offs["ga"], d), 512)
    gro, gao = _blk(offs["gr"], tn), _blk(offs["ga"], tn)
    return pl.pallas_call(
        _merge_kernel,
        grid=(m // tm, d // tn),
        in_specs=[pl.BlockSpec((tm, kr), lambda i, j: (i, 0)),
                  pl.BlockSpec((tm, ka), lambda i, j: (i, 0)),
                  pl.BlockSpec((tm, tn), lambda i, j: (i, gro + j)),
                  pl.BlockSpec((tm, tn), lambda i, j: (i, gao + j)),
                  pl.BlockSpec((kr, tn), lambda i, j: (0, j)),
                  pl.BlockSpec((ka, tn), lambda i, j: (0, j))],
        out_specs=pl.BlockSpec((tm, tn), lambda i, j: (i, j)),
        out_shape=jax.ShapeDtypeStruct((m, d), BF16),
        compiler_params=_cparams(("arbitrary", "arbitrary")),
    )(retg, att, p, p, w_ret_o, w_attn_o)


def _pack_pairs(x):
    hw = x.shape[1] // 2
    lo = pltpu.bitcast(x[:, :hw].astype(BF16).astype(F32), U32)
    hi = pltpu.bitcast(x[:, hw:].astype(BF16).astype(F32), U32)
    return (hi & jnp.uint32(0xFFFF0000)) | (lo >> 16)


def _store_token_tiles(ref, packed):
    m = packed.shape[0]
    st = packed.shape[1] // LANES
    for s in range(st):
        ref[pl.ds(s, m, stride=st), :] = packed[:, s * LANES:(s + 1) * LANES]


def _load_token_tiles(ref, m):
    st = ref.shape[0] // m
    return jnp.concatenate([ref[pl.ds(s, m, stride=st), :] for s in range(st)], axis=1)


def _unpack_pairs(w):
    lo = pltpu.bitcast(w << 16, F32)
    hi = pltpu.bitcast(w & jnp.uint32(0xFFFF0000), F32)
    return lo, hi


def _post_kernel(z_ref, wo_ref, x_ref, g1_ref, pw_ref, fw_ref, sc_ref, sh_ref, rwh_ref, rwl_ref,
                 x1_ref, f_ref, lg_ref):
    tm = x_ref.shape[0]
    sub = min(tm, POST_SUB)
    st = f_ref.shape[0] // tm
    nt = (((1,), (1,)), ((), ()))
    fhs, fls = [], []
    for h in range(tm // sub):
        r0 = h * sub
        y = jnp.dot(z_ref[r0:r0 + sub, :], wo_ref[...], preferred_element_type=F32)
        x1 = x_ref[r0:r0 + sub, :] + g1_ref[0] * (_rms(y) * pw_ref[...])
        x1_ref[r0:r0 + sub, :] = x1
        f = (_rms(x1) * fw_ref[...]) * (1.0 + sc_ref[0]) + sh_ref[0]
        _store_token_tiles(f_ref.at[r0 * st:(r0 + sub) * st, :], _pack_pairs(f))
        fh = f.astype(BF16)
        fhs.append(fh)
        fls.append((f - fh.astype(F32)).astype(BF16))
    fh = jnp.concatenate(fhs, axis=0)
    fl = jnp.concatenate(fls, axis=0)
    lg_ref[...] = (lax.dot_general(rwh_ref[...], fh, nt, preferred_element_type=F32)
                   + lax.dot_general(rwh_ref[...], fl, nt, preferred_element_type=F32)
                   + lax.dot_general(rwl_ref[...], fh, nt, preferred_element_type=F32))


def _post(z, w_out, x2, g1, post_w, pre_w, sc2, sh2, router_wt, rows_per_mod):
    m, d = x2.shape
    e = router_wt.shape[0]
    st = d // 2 // LANES
    assert st == SUBLANES, "token rows must be whole (8, 128) tiles"
    rw_hi = router_wt.astype(BF16)
    rw_lo = (router_wt - rw_hi.astype(F32)).astype(BF16)
    tm = _tile(rows_per_mod, 2 * POST_SUB)
    per = rows_per_mod // tm
    g = g1.shape[0]
    vec = pl.BlockSpec((1, d), lambda i: (0, 0))
    mod = pl.BlockSpec((1, 1, d), lambda i: (i // per, 0, 0))
    once = pl.Buffered(1)
    return pl.pallas_call(
        _post_kernel,
        grid=(m // tm,),
        in_specs=[pl.BlockSpec((tm, d), lambda i: (i, 0)),
                  pl.BlockSpec((d, d), lambda i: (0, 0), pipeline_mode=once),
                  pl.BlockSpec((tm, d), lambda i: (i, 0)),
                  mod, vec, vec, mod, mod,
                  pl.BlockSpec((e, d), lambda i: (0, 0), pipeline_mode=once),
                  pl.BlockSpec((e, d), lambda i: (0, 0), pipeline_mode=once)],
        out_specs=[pl.BlockSpec((tm, d), lambda i: (i, 0)),
                   pl.BlockSpec((tm * st, LANES), lambda i: (i, 0)),
                   pl.BlockSpec((e, tm), lambda i: (0, i))],
        out_shape=[jax.ShapeDtypeStruct((m, d), F32),
                   jax.ShapeDtypeStruct((m * st, LANES), U32),
                   jax.ShapeDtypeStruct((e, m), F32)],
        compiler_params=_cparams(("arbitrary",)),
    )(z, w_out, x2, g1.reshape(g, 1, d), post_w.reshape(1, d), pre_w.reshape(1, d),
      sc2.reshape(g, 1, d), sh2.reshape(g, 1, d), rw_hi, rw_lo)


def _route_kernel(lg_ref, b_ref, tri_ref, idx_ref, w_ref, pos_ref, cnt_ref, carry):
    @pl.when(pl.program_id(0) == 0)
    def _():
        carry[...] = jnp.zeros_like(carry)

    e, tn = lg_ref.shape
    gs = e // N_GROUPS
    neg = -jnp.inf
    big = float(e)
    s = jax.nn.sigmoid(lg_ref[...])
    choice = s + b_ref[...]

    row_g = lax.broadcasted_iota(I32, (gs, tn), 0).astype(F32)
    rows = []
    for g in range(N_GROUPS):
        blk = choice[g * gs:(g + 1) * gs]
        m1 = jnp.max(blk, axis=0, keepdims=True)
        i1 = jnp.min(jnp.where(blk == m1, row_g, big), axis=0, keepdims=True)
        m2 = jnp.max(jnp.where(row_g == i1, neg, blk), axis=0, keepdims=True)
        rows.append(m1 + m2)
    gscore = jnp.concatenate(rows, axis=0)

    row_n = lax.broadcasted_iota(I32, (N_GROUPS, tn), 0).astype(F32)
    sel = jnp.zeros((N_GROUPS, tn), F32)
    cur = gscore
    for _ in range(TOPK_GROUPS):
        m = jnp.max(cur, axis=0, keepdims=True)
        gi = jnp.min(jnp.where(cur == m, row_n, big), axis=0, keepdims=True)
        hit = row_n == gi
        sel = jnp.where(hit, 1.0, sel)
        cur = jnp.where(hit, neg, cur)

    masked = jnp.concatenate(
        [jnp.where(sel[g:g + 1] > 0.0, choice[g * gs:(g + 1) * gs], neg) for g in range(N_GROUPS)], axis=0)

    row_e = lax.broadcasted_iota(I32, (e, tn), 0).astype(F32)
    onehot = jnp.zeros((e, tn), F32)
    cur = masked
    idxs, ws = [], []
    for _ in range(TOP_K):
        m = jnp.max(cur, axis=0, keepdims=True)
        ik = jnp.min(jnp.where(cur == m, row_e, big), axis=0, keepdims=True)
        hit = row_e == ik
        ws.append(jnp.sum(jnp.where(hit, s, 0.0), axis=0, keepdims=True))
        cur = jnp.where(hit, neg, cur)
        onehot = jnp.where(hit, 1.0, onehot)
        idxs.append(ik)
    w = jnp.concatenate(ws, axis=0)
    w = w / jnp.sum(w, axis=0, keepdims=True) * ROUTED_SCALE

    prefix = jnp.dot(onehot.astype(BF16), tri_ref[...], preferred_element_type=F32)
    base = prefix + carry[...]
    pos = jnp.concatenate(
        [jnp.sum(jnp.where(row_e == ik, base, 0.0), axis=0, keepdims=True) for ik in idxs], axis=0)
    carry[...] = carry[...] + jnp.sum(onehot, axis=1, keepdims=True)

    idx_ref[...] = jnp.concatenate(idxs, axis=0).astype(I32)
    w_ref[...] = w
    pos_ref[...] = pos.astype(I32)
    cnt_ref[...] = jnp.broadcast_to(carry[...], cnt_ref.shape).astype(I32)


def _route(logits_t, bias):
    e, n = logits_t.shape
    tn = _tile(n, 512)
    tri = (jnp.arange(tn)[:, None] < jnp.arange(tn)[None, :]).astype(BF16)
    kspec = pl.BlockSpec((TOP_K, tn), lambda i: (0, i))
    return pl.pallas_call(
        _route_kernel,
        grid=(n // tn,),
        in_specs=[pl.BlockSpec((e, tn), lambda i: (0, i)),
                  pl.BlockSpec((e, 1), lambda i: (0, 0)),
                  pl.BlockSpec((tn, tn), lambda i: (0, 0))],
        out_specs=[kspec, kspec, kspec, pl.BlockSpec((e, LANES), lambda i: (0, 0))],
        out_shape=[jax.ShapeDtypeStruct((TOP_K, n), I32), jax.ShapeDtypeStruct((TOP_K, n), F32),
                   jax.ShapeDtypeStruct((TOP_K, n), I32), jax.ShapeDtypeStruct((e, LANES), I32)],
        scratch_shapes=[pltpu.VMEM((e, 1), F32)],
        compiler_params=_cparams(("arbitrary",)),
    )(logits_t, bias.reshape(e, 1), tri)


def _dest_kernel(idx_ref, pos_ref, off_ref, o_ref):
    e = off_ref.shape[0]
    tn = idx_ref.shape[1]
    row_e = lax.broadcasted_iota(I32, (e, tn), 0)
    off = off_ref[...]
    idx = idx_ref[...]
    rows = [jnp.sum(jnp.where(row_e == idx[k:k + 1], off, 0.0), axis=0, keepdims=True) for k in range(TOP_K)]
    o_ref[...] = jnp.concatenate(rows, axis=0).astype(I32) + pos_ref[...]


def _dest(top_idx, pos, pad_off):
    k, n = top_idx.shape
    e = pad_off.shape[0]
    tn = _tile(n, 512)
    spec = pl.BlockSpec((k, tn), lambda i: (0, i))
    return pl.pallas_call(
        _dest_kernel,
        grid=(n // tn,),
        in_specs=[spec, spec, pl.BlockSpec((e, 1), lambda i: (0, 0))],
        out_specs=spec,
        out_shape=jax.ShapeDtypeStruct((k, n), I32),
        compiler_params=_cparams(("arbitrary",)),
    )(top_idx, pos, pad_off.astype(F32).reshape(e, 1))


def _dispatch_kernel(cnt_ref, off_ref, nu_ref, dest_ref, f_ref, xs_hbm, zbuf, sem, zsem):
    i = pl.program_id(0)
    tt = dest_ref.shape[1]
    st = SUBLANES
    bm = zbuf.shape[0] // st
    n_exp = cnt_ref.shape[0]
    nb = xs_hbm.shape[0] // (bm * st)

    def zero_fill(wait):
        def fill(start, rows):
            cp = pltpu.make_async_copy(zbuf.at[pl.ds(0, rows * st), :],
                                       xs_hbm.at[pl.ds(pl.multiple_of(start * st, st), rows * st), :], zsem)
            if wait:
                cp.wait()
            else:
                cp.start()

        def per_expert(e, c):
            cnt = cnt_ref[e]
            npad = (-cnt) & (bm - 1)
            start = off_ref[e] + cnt
            p = bm // 2
            while p >= 1:
                @pl.when((npad & p) != 0)
                def _(start=start, p=p):
                    fill(start, p)
                start = start + (npad & p)
                p //= 2
            return c
        lax.fori_loop(0, n_exp, per_expert, 0)

        def per_block(b, c):
            fill(b * bm, bm)
            return c
        lax.fori_loop(nu_ref[0], nb, per_block, 0)

    def row_copy(j, dst):
        return pltpu.make_async_copy(f_ref.at[pl.ds(pl.multiple_of(j * st, st), st), :],
                                     xs_hbm.at[pl.ds(pl.multiple_of(dst * st, st), st), :], sem)

    @pl.when(i == 0)
    def _():
        zbuf[...] = jnp.zeros_like(zbuf)
        zero_fill(False)

    def issue(j, c):
        base = j * DMA_GROUP
        dsts = [[dest_ref[k, base + u] for k in range(TOP_K)] for u in range(DMA_GROUP)]
        for u in range(DMA_GROUP):
            for k in range(TOP_K):
                row_copy(base + u, dsts[u][k]).start(priority=k % 2)

        @pl.when(j >= DMA_LAG)
        def _():
            wait_group()
        return c

    def wait_group():
        for _ in range(DMA_GROUP * TOP_K):
            row_copy(0, 0).wait()

    n_groups = tt // DMA_GROUP
    lax.fori_loop(0, n_groups, issue, 0)

    def drain(j, c):
        wait_group()
        return c
    lax.fori_loop(0, min(DMA_LAG, n_groups), drain, 0)

    @pl.when(i == pl.num_programs(0) - 1)
    def _():
        zero_fill(True)


def _dispatch(counts, pad_off, n_used, dest, f_packed, n_slots):
    st = SUBLANES
    n = f_packed.shape[0] // st
    tt = _tile(n, 512)
    grid_spec = pltpu.PrefetchScalarGridSpec(
        num_scalar_prefetch=3,
        grid=(n // tt,),
        in_specs=[pl.BlockSpec((TOP_K, tt), lambda i, c, o, u: (0, i), memory_space=pltpu.SMEM),
                  pl.BlockSpec((tt * st, LANES), lambda i, c, o, u: (i, 0))],
        out_specs=pl.BlockSpec(memory_space=pl.ANY),
        scratch_shapes=[pltpu.VMEM((MOE_BLOCK * st, LANES), U32), pltpu.SemaphoreType.DMA(()),
                        pltpu.SemaphoreType.DMA(())],
    )
    return pl.pallas_call(
        _dispatch_kernel,
        grid_spec=grid_spec,
        out_shape=jax.ShapeDtypeStruct((n_slots * st, LANES), U32),
        compiler_params=_cparams(("arbitrary",)),
    )(counts, pad_off, n_used, dest, f_packed)


def _cast_rows(src, dst):
    rows = src.shape[0]
    ch = CAST_ROWS * LANES // src.shape[1]

    def body(j, c):
        r0 = pl.multiple_of(j * ch, ch)
        dst[pl.ds(r0, ch), :] = src[pl.ds(r0, ch), :].astype(dst.dtype)
        return c
    lax.fori_loop(0, rows // ch, body, 0, unroll=4)


def _expert_kernel(ob_ref, bv_ref, act_ref, na_ref, nu_ref, x_ref, w1_hbm, w3_hbm, w2_hbm, y_ref,
                   w1f, w3f, w2f, w1b, w3b, w2b, wsem):
    b = pl.program_id(0)
    n_used = nu_ref[0]
    n_act = na_ref[0]
    bm = MOE_BLOCK
    hw = w1b.shape[0] // 2
    st = x_ref.shape[0] // bm
    ring = w1f.shape[0]

    def fetch(i, wait):
        e = act_ref[i]
        slot = i % ring
        for src, dst in ((w1_hbm, w1f), (w3_hbm, w3f), (w2_hbm, w2f)):
            cp = pltpu.make_async_copy(src.at[e], dst.at[slot], wsem.at[slot])
            if wait:
                cp.wait()
            else:
                cp.start(priority=1)

    @pl.when(b < n_used)
    def _():
        i = ob_ref[b]
        first = jnp.logical_or(b == 0, i != ob_ref[jnp.maximum(b - 1, 0)])

        @pl.when(b == 0)
        def _():
            fetch(0, False)
            for ahead in range(1, ring):
                @pl.when(n_act > ahead)
                def _(ahead=ahead):
                    fetch(ahead, False)

        @pl.when(first)
        def _():
            fetch(i, True)

        def cast_weights():
            slot = i % ring
            for src, dst in ((w1f, w1b), (w3f, w3b), (w2f, w2b)):
                ch = CAST_ROWS * LANES // dst.shape[1]
                for j in range(dst.shape[0] // ch):
                    dst[j * ch:(j + 1) * ch, :] = src[slot, j * ch:(j + 1) * ch, :].astype(BF16)

        def ffn(rows):
            lo, hi = _unpack_pairs(_load_token_tiles(x_ref.at[0:rows * st, :], rows))
            lo = lo.astype(BF16)
            hi = hi.astype(BF16)
            h1 = (jnp.dot(lo, w1b[0:hw, :], preferred_element_type=F32)
                  + jnp.dot(hi, w1b[hw:, :], preferred_element_type=F32))
            h3 = (jnp.dot(lo, w3b[0:hw, :], preferred_element_type=F32)
                  + jnp.dot(hi, w3b[hw:, :], preferred_element_type=F32))
            hid = (h1 * jax.nn.sigmoid(h1) * h3).astype(BF16)
            _store_token_tiles(y_ref.at[0:rows * st, :],
                               _pack_pairs(jnp.dot(hid, w2b[...], preferred_element_type=F32)))

        half = bm // 2
        full = bv_ref[b] > half
        for is_first in (True, False):
            for rows in (bm, half):
                cond = jnp.logical_and(first == is_first, full == (rows == bm))

                @pl.when(cond)
                def _(is_first=is_first, rows=rows):
                    if is_first:
                        cast_weights()
                    ffn(rows)
                    if rows < bm:
                        y_ref[rows * st:, :] = x_ref[rows * st:, :]

        @pl.when(jnp.logical_and(first, i + ring < n_act))
        def _():
            fetch(i + ring, False)


def _experts(xs, blk_ord, blk_valid, active, n_active, n_used, w1, w3, w2):
    nb = blk_ord.shape[0]
    e, d, fdim = w1.shape
    rows = MOE_BLOCK * (d // 2 // LANES)
    w_elems = 3 * d * fdim
    vmem_limit = (W_BUFS * w_elems * 4 + w_elems * 2 + 4 * rows * LANES * 4
                  + MOE_BLOCK * (2 * fdim + 2 * d) * 4 + VMEM_SLACK)
    assert vmem_limit <= V7X_VMEM_BYTES
    grid_spec = pltpu.PrefetchScalarGridSpec(
        num_scalar_prefetch=5,
        grid=(nb,),
        in_specs=[pl.BlockSpec((rows, LANES), lambda b, ob, bv, ac, na, nu: (jnp.minimum(b, nu[0] - 1), 0)),
                  pl.BlockSpec(memory_space=pl.ANY),
                  pl.BlockSpec(memory_space=pl.ANY),
                  pl.BlockSpec(memory_space=pl.ANY)],
        out_specs=pl.BlockSpec((rows, LANES), lambda b, ob, bv, ac, na, nu: (jnp.minimum(b, nu[0] - 1), 0)),
        scratch_shapes=[pltpu.VMEM((W_BUFS, d, fdim), F32), pltpu.VMEM((W_BUFS, d, fdim), F32),
                        pltpu.VMEM((W_BUFS, fdim, d), F32),
                        pltpu.VMEM((d, fdim), BF16), pltpu.VMEM((d, fdim), BF16), pltpu.VMEM((fdim, d), BF16),
                        pltpu.SemaphoreType.DMA((W_BUFS,))],
    )
    return pl.pallas_call(
        _expert_kernel,
        grid_spec=grid_spec,
        out_shape=jax.ShapeDtypeStruct((nb * rows, LANES), U32),
        input_output_aliases={5: 0},
        compiler_params=_cparams(("arbitrary",), vmem_limit),
    )(blk_ord, blk_valid, active, n_active, n_used, xs, w1, w3, w2)


def _combine_kernel(dcur_ref, dnxt_ref, wt_ref, f_ref, x1_ref, g2_ref, pw_ref, s1_ref, s3_ref, s2_ref, ys_hbm,
                    o_ref, ybuf, sem):
    i = pl.program_id(0)
    tm = x1_ref.shape[0]
    hw = x1_ref.shape[1] // 2
    st = f_ref.shape[0] // tm
    slot = i % 2

    def row_copy(src, k, r, s):
        return pltpu.make_async_copy(ys_hbm.at[pl.ds(pl.multiple_of(src * st, st), st), :],
                                     ybuf.at[s, k, pl.ds(pl.multiple_of(r * st, st), st), :], sem.at[s])

    def start_gather(d_ref, s):
        def body(j, c):
            base = j * DMA_GROUP
            srcs = [[d_ref[k, base + u] for k in range(TOP_K)] for u in range(DMA_GROUP)]
            for u in range(DMA_GROUP):
                for k in range(TOP_K):
                    row_copy(srcs[u][k], k, base + u, s).start(priority=k % 2)
            return c
        lax.fori_loop(0, tm // DMA_GROUP, body, 0)

    def wait_gather(s):
        def body(r, c):
            for k in range(TOP_K):
                row_copy(0, k, r, s).wait()
            return c
        lax.fori_loop(0, tm, body, 0, unroll=2)

    @pl.when(i == 0)
    def _():
        start_gather(dcur_ref, 0)

    @pl.when(i + 1 < pl.num_programs(0))
    def _():
        start_gather(dnxt_ref, 1 - slot)

    lo, hi = _unpack_pairs(_load_token_tiles(f_ref, tm))
    lo = lo.astype(BF16)
    hi = hi.astype(BF16)
    h1 = (jnp.dot(lo, s1_ref[0:hw, :], preferred_element_type=F32)
          + jnp.dot(hi, s1_ref[hw:, :], preferred_element_type=F32))
    h3 = (jnp.dot(lo, s3_ref[0:hw, :], preferred_element_type=F32)
          + jnp.dot(hi, s3_ref[hw:, :], preferred_element_type=F32))
    shared = jnp.dot((h1 * jax.nn.sigmoid(h1) * h3).astype(BF16), s2_ref[...], preferred_element_type=F32)

    wait_gather(slot)
    wt = wt_ref[...]
    r_lo = shared[:, :hw]
    r_hi = shared[:, hw:]
    for k in range(TOP_K):
        a, c = _unpack_pairs(_load_token_tiles(ybuf.at[slot, k], tm))
        w = wt[:, k:k + 1]
        r_lo = r_lo + a * w
        r_hi = r_hi + c * w
    moe = jnp.concatenate([r_lo, r_hi], axis=1)
    o_ref[...] = x1_ref[...] + g2_ref[0] * (_rms(moe) * pw_ref[...])


def _combine(ys, dest, top_wt, f_packed, x1, g2, post_w, s1, s3, s2, rows_per_mod):
    m, d = x1.shape
    st = d // 2 // LANES
    fdim = s1.shape[1]
    tm = _tile(rows_per_mod, 256)
    per = rows_per_mod // tm
    g = g2.shape[0]
    nblk = m // tm
    smem = functools.partial(pl.BlockSpec, memory_space=pltpu.SMEM)
    return pl.pallas_call(
        _combine_kernel,
        grid=(nblk,),
        in_specs=[smem((TOP_K, tm), lambda i: (0, i)),
                  smem((TOP_K, tm), lambda i: (0, jnp.minimum(i + 1, nblk - 1))),
                  pl.BlockSpec((tm, TOP_K), lambda i: (i, 0)),
                  pl.BlockSpec((tm * st, LANES), lambda i: (i, 0)),
                  pl.BlockSpec((tm, d), lambda i: (i, 0)),
                  pl.BlockSpec((1, 1, d), lambda i: (i // per, 0, 0)),
                  pl.BlockSpec((1, d), lambda i: (0, 0)),
                  pl.BlockSpec((d, fdim), lambda i: (0, 0)),
                  pl.BlockSpec((d, fdim), lambda i: (0, 0)),
                  pl.BlockSpec((fdim, d), lambda i: (0, 0)),
                  pl.BlockSpec(memory_space=pl.ANY)],
        out_specs=pl.BlockSpec((tm, d), lambda i: (i, 0)),
        out_shape=jax.ShapeDtypeStruct((m, d), F32),
        scratch_shapes=[pltpu.VMEM((2, TOP_K, tm * st, LANES), U32), pltpu.SemaphoreType.DMA((2,))],
        compiler_params=_cparams(("arbitrary",)),
    )(dest, dest, top_wt, f_packed, x1, g2.reshape(g, 1, d), post_w.reshape(1, d), s1, s3, s2, ys)


def _layer(x, c, ctx, c_ctx, ada_w, ada_b, mix_pre_norm, mix_post_norm, ffn_pre_norm, ffn_post_norm,
           w_in, ret_decay_fwd, ret_decay_bwd, attn_q_norm, attn_k_norm, w_ret_o, w_attn_o, w_out,
           router_w, router_b, exp_w1, exp_w3, exp_w2, shared_w1, shared_w3, shared_w2):
    b, t, d = x.shape
    lc = ctx.shape[1]
    n = b * t
    rq_w, rv_w = RET_HEADS * RET_QK_DIM, RET_HEADS * RET_V_DIM
    aq_w, akv_w = ATTN_HEADS * ATTN_HEAD_DIM, ATTN_KV_HEADS * ATTN_HEAD_DIM
    names = ("rq", "rk", "rv", "rg", "aq", "ak", "av", "gr", "ga")
    widths = (rq_w, rq_w, rv_w, rv_w, aq_w, akv_w, akv_w, d, d)
    offs, o = {}, 0
    for nm, wd in zip(names, widths):
        offs[nm] = o
        o += wd

    rows = -(-(b + 1) // 16) * 16
    cond = jnp.zeros((rows, d), F32).at[:b].set(c).at[b].set(c_ctx)
    mod = _adaln(cond, ada_w, ada_b)
    sh1, sc1, g1, sh2, sc2, g2 = [mod[:b, i * d:(i + 1) * d] for i in range(6)]
    ch1, cs1 = mod[b:b + 1, 0:d], mod[b:b + 1, d:2 * d]

    x2 = x.reshape(n, d)
    h_x = _norm_mod(x2, mix_pre_norm, sc1, sh1, t)
    h_c = _norm_mod(ctx.reshape(b * lc, d), mix_pre_norm, cs1, ch1, b * lc)

    tn = _tile(w_in.shape[1], 1024)
    p = _project(h_x, w_in, list(range(w_in.shape[1] // tn)), tn, BF16)
    cnames = ("rv", "rk", "ak", "av")
    tnc = functools.reduce(math.gcd, [offs[nm] for nm in cnames] + [widths[names.index(nm)] for nm in cnames], tn)
    coffs, ctiles = {}, []
    for nm in cnames:
        coffs[nm] = len(ctiles) * tnc
        ctiles += list(range(offs[nm] // tnc, (offs[nm] + widths[names.index(nm)]) // tnc))
    pc = _project(h_c, w_in, ctiles, tnc, BF16)

    lg = jnp.stack([jax.nn.log_sigmoid(ret_decay_fwd.astype(F32)),
                    jax.nn.log_sigmoid(ret_decay_bwd.astype(F32))])
    rc, rs_lo, rs_hi = _rope_tables(t, RET_QK_DIM)
    retg = _retention(p, pc, lg, rc, rs_lo + rs_hi, b, t, lc, offs, coffs)
    att = _attention(p, pc, attn_q_norm, attn_k_norm, _rope_tables(t, ATTN_HEAD_DIM), b, t, lc, offs, coffs)

    z = _merge(retg, att, p, w_ret_o.astype(BF16), w_attn_o.astype(BF16), offs)
    x1, f_packed, logits_t = _post(z, w_out.astype(BF16), x2, g1, mix_post_norm, ffn_pre_norm, sc2, sh2,
                                   router_w.T, t)

    top_idx, top_w, pos, cnt = _route(logits_t, router_b)

    bm = MOE_BLOCK
    e = router_w.shape[1]
    nb = (n * TOP_K) // bm + e
    counts = cnt[:, 0]
    padded = (counts + bm - 1) // bm * bm
    pad_end = jnp.cumsum(padded)
    pad_off = pad_end - padded
    blk_start = jnp.arange(nb, dtype=I32) * bm
    blk_expert = jnp.minimum(jnp.sum((pad_end[None, :] <= blk_start[:, None]).astype(I32), axis=1), e - 1)
    n_used = (pad_end[-1] // bm).astype(I32).reshape(1)
    is_act = (counts > 0).astype(I32)
    exp_ord = jnp.cumsum(is_act) - is_act
    n_active = jnp.sum(is_act).astype(I32).reshape(1)
    eids = jnp.arange(e, dtype=I32)
    active = jnp.sum(jnp.where((exp_ord[None, :] == eids[:, None]) & (is_act[None, :] > 0), eids[None, :], 0),
                     axis=1).astype(I32)
    blk_ord = exp_ord[blk_expert].astype(I32)
    blk_in_exp = jnp.arange(nb, dtype=I32) - (pad_off // bm)[blk_expert]
    blk_valid = jnp.clip(counts[blk_expert] - blk_in_exp * bm, 0, bm).astype(I32)

    dest = _dest(top_idx, pos, pad_off)
    xs = _dispatch(counts, pad_off, n_used, dest, f_packed, nb * bm)
    ys = _experts(xs, blk_ord, blk_valid, active, n_active, n_used, exp_w1, exp_w3, exp_w2)
    out = _combine(ys, dest, top_w.T, f_packed, x1, g2, ffn_post_norm, shared_w1.astype(BF16),
                   shared_w3.astype(BF16), shared_w2.astype(BF16), t)
    return out.reshape(b, t, d)


def kernel(x, c, ctx, c_ctx, ada_w, ada_b, mix_pre_norm, mix_post_norm, ffn_pre_norm, ffn_post_norm,
           w_in, ret_decay_fwd, ret_decay_bwd, attn_q_norm, attn_k_norm, w_ret_o, w_attn_o, w_out,
           router_w, router_b, exp_w1, exp_w3, exp_w2, shared_w1, shared_w3, shared_w2):
    assert ada_w.shape[0] == 1, "single-layer operation"
    return _layer(x, c, ctx, c_ctx, ada_w[0], ada_b[0], mix_pre_norm[0], mix_post_norm[0], ffn_pre_norm[0],
                  ffn_post_norm[0], w_in[0], ret_decay_fwd[0], ret_decay_bwd[0], attn_q_norm[0],
                  attn_k_norm[0], w_ret_o[0], w_attn_o[0], w_out[0], router_w[0], router_b[0],
                  exp_w1[0], exp_w3[0], exp_w2[0], shared_w1[0], shared_w3[0], shared_w2[0])
```
